```python
import jax, jax.numpy as jnp
from jax import lax
import numpy as np

D_MODEL = 2048
BATCH = 2
SEQ = 8192
DEPTH = 1

GDN_HEADS = 8
GDN_DK = 128
GDN_DV = 128
GDN_CONV = 4
GDN_CHUNK = 64
FOX_HEADS = 8
FOX_DH = 128
FOX_BLOCK = 128
MEM_LEN = 256
MEM_HEADS = 4
MEM_DH = 256
D_FF = 4 * D_MODEL
N_BRANCH = 3
EPS = 1e-6

GDN_QK = GDN_HEADS * GDN_DK
GDN_V = GDN_HEADS * GDN_DV
GDN_QKV = 2 * GDN_QK + GDN_V
FOX_W = FOX_HEADS * FOX_DH
MEM_W = MEM_HEADS * MEM_DH
IN_SPLITS = (GDN_QKV, GDN_V, GDN_HEADS, GDN_HEADS, FOX_W, FOX_W, FOX_W, FOX_HEADS, MEM_W, N_BRANCH * D_MODEL)
D_IN = 2 * GDN_QK + 2 * GDN_V + 2 * GDN_HEADS + 3 * FOX_W + FOX_HEADS + MEM_W + N_BRANCH * D_MODEL

kernel_name = "hybrid_gdn_fox_memory_block"


def rms_norm(x, g):
    xf = x.astype(jnp.float32)
    y = xf * lax.rsqrt(jnp.mean(xf * xf, axis=-1, keepdims=True) + EPS)
    return (y * g.astype(jnp.float32)).astype(x.dtype)


def l2_norm(x):
    return x * lax.rsqrt(jnp.sum(x * x, axis=-1, keepdims=True) + EPS)


def to_heads(t, n_heads):
    b, s, _ = t.shape
    return t.reshape(b, s, n_heads, -1).transpose(0, 2, 1, 3)


def causal_conv_silu(x, w):
    k_w = w.shape[0]
    s = x.shape[1]
    xp = jnp.pad(x, ((0, 0), (k_w - 1, 0), (0, 0)))
    y = xp[:, 0:s] * w[0]
    for i in range(1, k_w):
        y = y + xp[:, i:i + s] * w[i]
    return jax.nn.silu(y)


def gated_delta_rule(q, k, v, g, beta):
    b, h, s, dk = q.shape
    dv = v.shape[-1]
    c = GDN_CHUNK
    n = s // c
    q = q.reshape(b, h, n, c, dk)
    k = k.reshape(b, h, n, c, dk)
    v = v.reshape(b, h, n, c, dv)
    beta = beta.reshape(b, h, n, c)
    gam = jnp.cumsum(g.reshape(b, h, n, c), axis=-1)
    idx = jnp.arange(c)
    strict = idx[:, None] > idx[None, :]
    incl = idx[:, None] >= idx[None, :]
    diff = gam[..., :, None] - gam[..., None, :]
    dec_strict = jnp.where(strict, jnp.exp(jnp.where(strict, diff, 0.0)), 0.0)
    dec_incl = jnp.where(incl, jnp.exp(jnp.where(incl, diff, 0.0)), 0.0)
    m_low = beta[..., :, None] * jnp.einsum('bhnid,bhnjd->bhnij', k, k) * dec_strict
    a_mat = jnp.eye(c, dtype=jnp.float32) + m_low
    rhs = jnp.concatenate([(beta * jnp.exp(gam))[..., None] * k, beta[..., None] * v], axis=-1)
    sol = lax.linalg.triangular_solve(a_mat, rhs, left_side=True, lower=True, unit_diagonal=True)
    w_c, u_c = sol[..., :dk], sol[..., dk:]
    qk = jnp.einsum('bhnid,bhnjd->bhnij', q, k) * dec_incl
    q_dec = q * jnp.exp(gam)[..., None]
    k_dec = k * jnp.exp(gam[..., -1:] - gam)[..., None]
    chunk_dec = jnp.exp(gam[..., -1])

    def step(state, xs):
        w_i, u_i, qk_i, qd_i, kd_i, cd_i = xs
        u = u_i - jnp.einsum('bhid,bhde->bhie', w_i, state)
        o = jnp.einsum('bhid,bhde->bhie', qd_i, state) + jnp.einsum('bhij,bhje->bhie', qk_i, u)
        state = cd_i[..., None, None] * state + jnp.einsum('bhid,bhie->bhde', kd_i, u)
        return state, o

    xs = tuple(jnp.moveaxis(t, 2, 0) for t in (w_c, u_c, qk, q_dec, k_dec, chunk_dec))
    _, o = lax.scan(step, jnp.zeros((b, h, dk, dv), jnp.float32), xs)
    return jnp.moveaxis(o, 0, 2).reshape(b, h, s, dv)


def forgetting_attention(q, k, v, log_f):
    _, _, s, d = q.shape
    cum = jnp.cumsum(log_f, axis=-1)
    scale = d ** -0.5
    outs = []
    for start in range(0, s, FOX_BLOCK):
        end = start + FOX_BLOCK
        logits = jnp.einsum('bhqd,bhkd->bhqk', q[:, :, start:end], k[:, :, :end]).astype(jnp.float32) * scale
        logits = logits + cum[:, :, start:end, None] - cum[:, :, None, :end]
        mask = (start + jnp.arange(FOX_BLOCK))[:, None] >= jnp.arange(end)[None, :]
        p = jax.nn.softmax(jnp.where(mask, logits, -jnp.inf), axis=-1)
        outs.append(jnp.einsum('bhqk,bhkd->bhqd', p.astype(v.dtype), v[:, :, :end]))
    return jnp.concatenate(outs, axis=2)


def setup_inputs(seed: int = 0) -> dict:
    key = jax.random.key(seed)
    ks = jax.random.split(key, 24)
    L, D = DEPTH, D_MODEL
    nrm = lambda k, shape, fan_in: jax.random.normal(k, shape, jnp.float32) * (fan_in ** -0.5)
    gain = lambda k, shape: 1.0 + 0.02 * jax.random.normal(k, shape, jnp.float32)
    a_log = jnp.log(jax.random.uniform(ks[5], (L, GDN_HEADS), jnp.float32, 1.0, 16.0))
    dt = jnp.exp(jax.random.uniform(ks[6], (L, GDN_HEADS), jnp.float32, np.log(1e-3), np.log(1e-1)))
    dt_bias = dt + jnp.log(-jnp.expm1(-dt))
    return {
        "x": jax.random.normal(ks[0], (BATCH, SEQ, D), jnp.float32),
        "mem": jax.random.normal(ks[1], (BATCH, MEM_LEN, D), jnp.float32),
        "g_mix": gain(ks[2], (L, D)),
        "w_in": nrm(ks[3], (L, D, D_IN), D),
        "conv_w": nrm(ks[4], (L, GDN_CONV, GDN_QKV), GDN_CONV),
        "a_log": a_log,
        "dt_bias": dt_bias,
        "gdn_norm_g": gain(ks[7], (L, GDN_DV)),
        "fox_b_f": jax.random.uniform(ks[8], (L, FOX_HEADS), jnp.float32, 1.0, 4.0),
        "fox_q_norm": gain(ks[9], (L, FOX_DH)),
        "fox_k_norm": gain(ks[10], (L, FOX_DH)),
        "g_mem": gain(ks[11], (L, D)),
        "w_mem_kv": nrm(ks[12], (L, D, 2 * MEM_W), D),
        "mem_q_norm": gain(ks[13], (L, MEM_DH)),
        "mem_k_norm": gain(ks[14], (L, MEM_DH)),
        "w_up_gdn": nrm(ks[15], (L, GDN_V, D), GDN_V),
        "w_up_fox": nrm(ks[16], (L, FOX_W, D), FOX_W),
        "w_up_mem": nrm(ks[17], (L, MEM_W, D), MEM_W),
        "w_out": nrm(ks[18], (L, D, D), D),
        "g_mlp": gain(ks[19], (L, D)),
        "w_ff1": nrm(ks[20], (L, D, D_FF), D),
        "w_ff2": nrm(ks[21], (L, D_FF, D), D_FF),
    }


def reference(x, mem, g_mix, w_in, conv_w, a_log, dt_bias, gdn_norm_g, fox_b_f, fox_q_norm, fox_k_norm,
              g_mem, w_mem_kv, mem_q_norm, mem_k_norm, w_up_gdn, w_up_fox, w_up_mem, w_out, g_mlp, w_ff1, w_ff2):
    b, s, _ = x.shape
    splits = np.cumsum(IN_SPLITS)[:-1].tolist()
    f32 = jnp.float32
    for l in range(DEPTH):
        h = rms_norm(x, g_mix[l])
        proj = h @ w_in[l]
        qkv_a, z_a, b_a, a_a, q_b, k_b, v_b, f_b, q_m, gates = jnp.split(proj, splits, axis=-1)

        qkv_a = causal_conv_silu(qkv_a, conv_w[l])
        q_a, k_a, v_a = jnp.split(qkv_a, [GDN_QK, 2 * GDN_QK], axis=-1)
        q_a = l2_norm(to_heads(q_a, GDN_HEADS).astype(f32)) * (GDN_DK ** -0.5)
        k_a = l2_norm(to_heads(k_a, GDN_HEADS).astype(f32))
        v_a = to_heads(v_a, GDN_HEADS).astype(f32)
        beta = jax.nn.sigmoid(b_a.astype(f32)).transpose(0, 2, 1)
        g_dec = (-jnp.exp(a_log[l].astype(f32)) * jax.nn.softplus(a_a.astype(f32) + dt_bias[l].astype(f32))).transpose(0, 2, 1)
        o_a = gated_delta_rule(q_a, k_a, v_a, g_dec, beta).transpose(0, 2, 1, 3)
        o_a = rms_norm(o_a, gdn_norm_g[l]) * jax.nn.silu(z_a.astype(f32).reshape(b, s, GDN_HEADS, GDN_DV))
        o_a = o_a.reshape(b, s, GDN_V).astype(x.dtype)

        q_bh = rms_norm(to_heads(q_b, FOX_HEADS), fox_q_norm[l])
        k_bh = rms_norm(to_heads(k_b, FOX_HEADS), fox_k_norm[l])
        v_bh = to_heads(v_b, FOX_HEADS)
        log_f = jax.nn.log_sigmoid(f_b.astype(f32) + fox_b_f[l].astype(f32)).transpose(0, 2, 1)
        o_b = forgetting_attention(q_bh, k_bh, v_bh, log_f)
        o_b = o_b.transpose(0, 2, 1, 3).reshape(b, s, FOX_W)

        kv_m = rms_norm(mem, g_mem[l]) @ w_mem_kv[l]
        k_m, v_m = jnp.split(kv_m, 2, axis=-1)
        q_mh = rms_norm(to_heads(q_m, MEM_HEADS), mem_q_norm[l])
        k_mh = rms_norm(to_heads(k_m, MEM_HEADS), mem_k_norm[l])
        v_mh = to_heads(v_m, MEM_HEADS)
        logits_m = jnp.einsum('bhqd,bhkd->bhqk', q_mh, k_mh).astype(f32) * (MEM_DH ** -0.5)
        p_m = jax.nn.softmax(logits_m, axis=-1).astype(v_mh.dtype)
        o_m = jnp.einsum('bhqk,bhkd->bhqd', p_m, v_mh).transpose(0, 2, 1, 3).reshape(b, s, MEM_W)

        gate_a, gate_b, gate_m = jnp.split(jax.nn.sigmoid(gates), N_BRANCH, axis=-1)
        y = gate_a * (o_a @ w_up_gdn[l]) + gate_b * (o_b @ w_up_fox[l]) + gate_m * (o_m @ w_up_mem[l])
        x = x + y @ w_out[l]

        h2 = rms_norm(x, g_mlp[l])
        x = x + jnp.square(jax.nn.relu(h2 @ w_ff1[l])) @ w_ff2[l]
    return x
```

```python
import functools

import jax
import jax.numpy as jnp
from jax import lax
from jax.experimental import pallas as pl
from jax.experimental.pallas import tpu as pltpu

F32 = jnp.float32
BF16 = jnp.bfloat16
EPS = 1e-6

GDN_HEADS = 8
GDN_DK = 128
GDN_DV = 128
GDN_CONV = 4
GDN_CHUNK = 64
FOX_HEADS = 8
FOX_DH = 128
MEM_HEADS = 4
MEM_DH = 256
N_BRANCH = 3
LANES = 128
NEG_BIG = -1e30

VMEM_LIMIT = 56 * 1024 * 1024


def _tile(n, pref):
    return pref if n % pref == 0 else n


def _cparams(sem):
    return pltpu.CompilerParams(dimension_semantics=sem, vmem_limit_bytes=VMEM_LIMIT)


def _dot(a, b):
    return jnp.dot(a.astype(BF16), b.astype(BF16), preferred_element_type=F32)


def _dot_nt(a, b):
    return lax.dot_general(a.astype(BF16), b.astype(BF16), (((1,), (1,)), ((), ())),
                           preferred_element_type=F32)


def _dot_tn(a, b):
    return lax.dot_general(a.astype(BF16), b.astype(BF16), (((0,), (0,)), ((), ())),
                           preferred_element_type=F32)


def _split2(a):
    hi = a.astype(BF16)
    lo = (a - hi.astype(F32)).astype(BF16)
    return hi, lo


def _split3(a):
    hi = a.astype(BF16).astype(F32)
    r = a - hi
    mid = r.astype(BF16).astype(F32)
    lo = (r - mid).astype(BF16).astype(F32)
    return hi, mid, lo


def _dot3(a, b):
    a_hi, a_lo = _split2(a)
    b_hi, b_lo = _split2(b)
    d = functools.partial(jnp.dot, preferred_element_type=F32)
    return d(a_hi, b_hi) + d(a_hi, b_lo) + d(a_lo, b_hi)


def _dot_exact_lhs(l_bf16, v):
    hi, mid, lo = _split3(v)
    d = functools.partial(jnp.dot, preferred_element_type=F32)
    return d(l_bf16, hi.astype(BF16)) + d(l_bf16, mid.astype(BF16)) + d(l_bf16, lo.astype(BF16))


def _lane_col(a, idx):
    lane = lax.broadcasted_iota(jnp.int32, a.shape, 1)
    return jnp.sum(jnp.where(lane == idx, a, 0.0), axis=1, keepdims=True)


def _softplus(x):
    return jnp.maximum(x, 0.0) + jnp.log1p(jnp.exp(-jnp.abs(x)))


def _norm_small_kernel(x_ref, g_ref, ws_ref, bias_ref, alog_ref, h_ref, sm_ref, carry_ref, *, tm):
    s = pl.program_id(1)

    @pl.when(s == 0)
    def _():
        carry_ref[...] = jnp.zeros_like(carry_ref)

    x = x_ref[0]
    h = x * lax.rsqrt(jnp.mean(x * x, axis=-1, keepdims=True) + EPS) * g_ref[...]
    h_ref[0] = h.astype(BF16)

    pre = _dot3(h, ws_ref[...]) + bias_ref[...]
    lane = lax.broadcasted_iota(jnp.int32, pre.shape, 1)
    nh = GDN_HEADS
    beta = jax.nn.sigmoid(pre)
    gdec = -jnp.exp(alog_ref[...]) * _softplus(pre)
    logf = -_softplus(-pre)
    vals = jnp.where(lane < nh, beta, jnp.where(lane < 2 * nh, gdec, jnp.where(lane < 3 * nh, logf, 0.0)))

    row = lax.broadcasted_iota(jnp.int32, (tm, tm), 0)
    col = lax.broadcasted_iota(jnp.int32, (tm, tm), 1)
    low = col <= row
    l_full = jnp.where(low, 1.0, 0.0).astype(BF16)
    sh = GDN_CHUNK.bit_length() - 1
    same_chunk = (row >> sh) == (col >> sh)
    l_blk = jnp.where(low, jnp.where(same_chunk, 1.0, 0.0), 0.0).astype(BF16)
    cs_blk = _dot_exact_lhs(l_blk, vals)
    cs_full = _dot_exact_lhs(l_full, vals) + carry_ref[0:1, :]
    sm_ref[0] = jnp.where(lane < nh, vals, jnp.where(lane < 2 * nh, cs_blk, jnp.where(lane < 3 * nh, cs_full, 0.0)))
    carry_ref[...] = jnp.broadcast_to(cs_full[tm - 1:tm, :], carry_ref.shape)


def _norm_small(x, g, w_small, bias, alog):
    b, s, d = x.shape
    tm = _tile(s, 512)
    return pl.pallas_call(
        functools.partial(_norm_small_kernel, tm=tm),
        out_shape=(jax.ShapeDtypeStruct((b, s, d), BF16), jax.ShapeDtypeStruct((b, s, LANES), F32)),
        grid=(b, s // tm),
        in_specs=[
            pl.BlockSpec((1, tm, d), lambda i, j: (i, j, 0)),
            pl.BlockSpec((1, d), lambda i, j: (0, 0)),
            pl.BlockSpec((d, LANES), lambda i, j: (0, 0)),
            pl.BlockSpec((1, LANES), lambda i, j: (0, 0)),
            pl.BlockSpec((1, LANES), lambda i, j: (0, 0)),
        ],
        out_specs=(
            pl.BlockSpec((1, tm, d), lambda i, j: (i, j, 0)),
            pl.BlockSpec((1, tm, LANES), lambda i, j: (i, j, 0)),
        ),
        scratch_shapes=[pltpu.VMEM((8, LANES), F32)],
        compiler_params=_cparams(("arbitrary", "arbitrary")),
        name="norm_small",
    )(x, g, w_small, bias, alog)


def _rmsnorm_kernel(x_ref, g_ref, o_ref):
    x = x_ref[...]
    o_ref[...] = (x * lax.rsqrt(jnp.mean(x * x, axis=-1, keepdims=True) + EPS) * g_ref[...]).astype(o_ref.dtype)


def _rmsnorm(x2d, g):
    t, d = x2d.shape
    tm = _tile(t, 512)
    return pl.pallas_call(
        _rmsnorm_kernel,
        out_shape=jax.ShapeDtypeStruct((t, d), BF16),
        grid=(t // tm,),
        in_specs=[pl.BlockSpec((tm, d), lambda i: (i, 0)), pl.BlockSpec((1, d), lambda i: (0, 0))],
        out_specs=pl.BlockSpec((tm, d), lambda i: (i, 0)),
        compiler_params=_cparams(("arbitrary",)),
        name="rmsnorm",
    )(x2d, g)


def _mm_kernel(a_ref, w_ref, *rest, act, has_res):
    if has_res:
        r_ref, o_ref = rest
    else:
        (o_ref,) = rest
    acc = jnp.dot(a_ref[...], w_ref[...], preferred_element_type=F32)
    if act == "sigmoid":
        acc = jax.nn.sigmoid(acc)
    elif act == "relu2":
        r = jnp.maximum(acc, 0.0)
        acc = r * r
    if has_res:
        acc = acc + r_ref[...]
    o_ref[...] = acc.astype(o_ref.dtype)


def _matmul(a, w, *, out_dtype, tm, tn, act=None, residual=None, name="matmul"):
    m, k = a.shape
    n = w.shape[1]
    tm = _tile(m, tm)
    tn = _tile(n, tn)
    in_specs = [pl.BlockSpec((tm, k), lambda i, j: (i, 0)), pl.BlockSpec((k, tn), lambda i, j: (0, j))]
    args = [a, w]
    if residual is not None:
        in_specs.append(pl.BlockSpec((tm, tn), lambda i, j: (i, j)))
        args.append(residual)
    return pl.pallas_call(
        functools.partial(_mm_kernel, act=act, has_res=residual is not None),
        out_shape=jax.ShapeDtypeStruct((m, n), out_dtype),
        grid=(m // tm, n // tn),
        in_specs=in_specs,
        out_specs=pl.BlockSpec((tm, tn), lambda i, j: (i, j)),
        compiler_params=_cparams(("arbitrary", "arbitrary")),
        name=name,
    )(*args)


def _gdn_kernel(q_ref, k_ref, v_ref, z_ref, sm_ref, cwq_ref, cwk_ref, cwv_ref, gn_ref, o_ref,
                state_ref, tail_ref, ext_ref, w_s, u_s, qk_s, qd_s, kd_s, cd_s, o_s, *, ts, nc):
    hd = pl.program_id(1)
    si = pl.program_id(2)
    c_sz = GDN_CHUNK

    @pl.when(si == 0)
    def _():
        state_ref[...] = jnp.zeros_like(state_ref)
        tail_ref[...] = jnp.zeros_like(tail_ref)

    def conv_silu(x_ref, w_ref, idx):
        x = x_ref[0]
        ext_ref[0:8, :] = tail_ref[idx]
        ext_ref[8:8 + ts, :] = x
        w = w_ref[...]
        y = (ext_ref[5:5 + ts, :] * w[0:1, :] + ext_ref[6:6 + ts, :] * w[1:2, :]
             + ext_ref[7:7 + ts, :] * w[2:3, :] + x * w[3:4, :])
        tail_ref[idx] = x[ts - 8:ts, :]
        return y * jax.nn.sigmoid(y)

    q = conv_silu(q_ref, cwq_ref, 0)
    k = conv_silu(k_ref, cwk_ref, 1)
    v = conv_silu(v_ref, cwv_ref, 2)
    q = q * lax.rsqrt(jnp.sum(q * q, axis=-1, keepdims=True) + EPS) * (GDN_DK ** -0.5)
    k = k * lax.rsqrt(jnp.sum(k * k, axis=-1, keepdims=True) + EPS)

    sm = sm_ref[0]
    beta = _lane_col(sm, hd)
    gam = _lane_col(sm, hd + GDN_HEADS)
    egam = jnp.exp(gam)
    rhs_w = k * (beta * egam)
    rhs_u = v * beta
    qd = q * egam

    lane = lax.broadcasted_iota(jnp.int32, (c_sz, LANES), 1)
    ri = lax.broadcasted_iota(jnp.int32, (c_sz, c_sz), 0)
    ci = lax.broadcasted_iota(jnp.int32, (c_sz, c_sz), 1)
    strict = ri > ci
    incl = ri >= ci
    eye = jnp.where(ri == ci, 1.0, 0.0)
    n_lvl = c_sz.bit_length() - 1
    lvl_masks = []
    for l in range(n_lvl):
        same = (ri >> (l + 1)) == (ci >> (l + 1))
        sel = jnp.where(same, jnp.where(((ri >> l) & 1) == 1, jnp.where(((ci >> l) & 1) == 0, 1.0, 0.0), 0.0), 0.0)
        lvl_masks.append(sel)

    for c in range(nc):
        sl = slice(c * c_sz, (c + 1) * c_sz)
        g_c = gam[sl]
        g_last = g_c[c_sz - 1:c_sz, :]
        kc = k[sl]
        hi, mid, lo = _split3(g_c)
        ga = jnp.where(lane == 0, hi, jnp.where(lane == 1, mid, jnp.where(lane == 2, lo, jnp.where(lane < 6, 1.0, 0.0))))
        gb = jnp.where(lane < 3, 1.0, jnp.where(lane == 3, -hi, jnp.where(lane == 4, -mid, jnp.where(lane == 5, -lo, 0.0))))
        diff = _dot_nt(ga, gb)
        e = jnp.exp(jnp.where(incl, diff, 0.0))
        dec_s = jnp.where(strict, e, 0.0)
        dec_i = jnp.where(incl, e, 0.0)
        mlow = beta[sl] * _dot_nt(kc, kc) * dec_s
        qk_s[c] = _dot_nt(q[sl], kc) * dec_i
        x = eye - lvl_masks[0] * mlow
        for l in range(1, n_lvl):
            x = x - _dot3(x, _dot3(lvl_masks[l] * mlow, x))
        wu = _dot3(x, jnp.concatenate([rhs_w[sl], rhs_u[sl]], axis=1))
        w_s[c] = wu[:, :GDN_DK]
        u_s[c] = wu[:, GDN_DK:]
        qd_s[c] = qd[sl]
        kd_s[c] = kc * jnp.exp(g_last - g_c)
        cd_s[c] = jnp.broadcast_to(jnp.exp(g_last), (8, LANES))

    def body(c, state):
        sb = state.astype(BF16)
        u = u_s[c] - _dot(w_s[c], sb)
        o = _dot(qd_s[c], sb) + _dot(qk_s[c], u)
        o_s[pl.ds(pl.multiple_of(c * c_sz, c_sz), c_sz), :] = o
        return state * cd_s[c][0:1, :] + _dot_tn(kd_s[c], u)

    state_ref[...] = lax.fori_loop(0, nc, body, state_ref[...])

    o = o_s[...]
    z = z_ref[0]
    on = o * lax.rsqrt(jnp.mean(o * o, axis=-1, keepdims=True) + EPS) * gn_ref[...]
    o_ref[0] = (on * (z * jax.nn.sigmoid(z))).astype(o_ref.dtype)


def _gdn(qkvz, sm, conv_w, gn):
    b, s, _ = qkvz.shape
    nh = GDN_HEADS
    ts = _tile(s, 512)
    nc = ts // GDN_CHUNK
    blk = lambda off: pl.BlockSpec((1, ts, LANES), lambda i, h, j: (i, j, off + h))
    cw = lambda off: pl.BlockSpec((GDN_CONV, LANES), lambda i, h, j: (0, off + h))
    return pl.pallas_call(
        functools.partial(_gdn_kernel, ts=ts, nc=nc),
        out_shape=jax.ShapeDtypeStruct((b, s, nh * GDN_DV), BF16),
        grid=(b, nh, s // ts),
        in_specs=[
            blk(0), blk(nh), blk(2 * nh), blk(3 * nh),
            pl.BlockSpec((1, ts, LANES), lambda i, h, j: (i, j, 0)),
            cw(0), cw(nh), cw(2 * nh),
            pl.BlockSpec((1, GDN_DV), lambda i, h, j: (0, 0)),
        ],
        out_specs=pl.BlockSpec((1, ts, LANES), lambda i, h, j: (i, j, h)),
        scratch_shapes=[
            pltpu.VMEM((GDN_DK, GDN_DV), F32),
            pltpu.VMEM((3, 8, LANES), F32),
            pltpu.VMEM((ts + 8, LANES), F32),
            pltpu.VMEM((nc, GDN_CHUNK, GDN_DK), F32),
            pltpu.VMEM((nc, GDN_CHUNK, GDN_DV), F32),
            pltpu.VMEM((nc, GDN_CHUNK, GDN_CHUNK), F32),
            pltpu.VMEM((nc, GDN_CHUNK, GDN_DK), F32),
            pltpu.VMEM((nc, GDN_CHUNK, GDN_DK), F32),
            pltpu.VMEM((nc, 8, LANES), F32),
            pltpu.VMEM((ts, GDN_DV), F32),
        ],
        compiler_params=_cparams(("arbitrary", "arbitrary", "arbitrary")),
        name="gdn",
    )(qkvz, qkvz, qkvz, qkvz, sm, conv_w, conv_w, conv_w, gn)


def _fox_kernel(q_ref, k_ref, v_ref, smq_ref, smk_ref, gq_ref, gk_ref, o_ref,
                kaug_ref, m_s, l_s, acc_s, *, tq, nk):
    hd = pl.program_id(1)
    qi = pl.program_id(2)
    cidx = hd + 2 * GDN_HEADS
    lane = lax.broadcasted_iota(jnp.int32, (tq, LANES), 1)

    @pl.when(qi == 0)
    def _():
        def build(j, carry):
            rows = pl.ds(pl.multiple_of(j * tq, tq), tq)
            kk = k_ref[0, rows, :].astype(F32)
            kn = kk * lax.rsqrt(jnp.mean(kk * kk, axis=-1, keepdims=True) + EPS) * gk_ref[...]
            hi, mid, lo = _split3(_lane_col(smk_ref[0, rows, :], cidx))
            aug = jnp.where(lane < 3, 1.0, jnp.where(lane == 3, -hi, jnp.where(lane == 4, -mid, jnp.where(lane == 5, -lo, 0.0))))
            kaug_ref[rows, 0:FOX_DH] = kn.astype(BF16)
            kaug_ref[rows, FOX_DH:2 * FOX_DH] = aug.astype(BF16)
            return carry
        lax.fori_loop(0, nk, build, 0)

    qq = q_ref[0].astype(F32)
    qn = qq * lax.rsqrt(jnp.mean(qq * qq, axis=-1, keepdims=True) + EPS) * gq_ref[...] * (FOX_DH ** -0.5)
    hi, mid, lo = _split3(_lane_col(smq_ref[0], cidx))
    aug = jnp.where(lane == 0, hi, jnp.where(lane == 1, mid, jnp.where(lane == 2, lo, jnp.where(lane < 6, 1.0, 0.0))))
    q_aug = jnp.concatenate([qn.astype(BF16), aug.astype(BF16)], axis=1)

    m_s[...] = jnp.full_like(m_s, NEG_BIG)
    l_s[...] = jnp.zeros_like(l_s)
    acc_s[...] = jnp.zeros_like(acc_s)

    def step(j, masked):
        rows = pl.ds(pl.multiple_of(j * tq, tq), tq)
        s = lax.dot_general(q_aug, kaug_ref[rows, :], (((1,), (1,)), ((), ())), preferred_element_type=F32)
        if masked:
            ri = lax.broadcasted_iota(jnp.int32, (tq, tq), 0)
            ci = lax.broadcasted_iota(jnp.int32, (tq, tq), 1)
            s = jnp.where(ri >= ci, s, NEG_BIG)
        m_prev = m_s[...]
        m_next = jnp.maximum(m_prev, jnp.max(s, axis=1, keepdims=True))
        p = jnp.exp(s - m_next[:, 0:1])
        alpha = jnp.exp(m_prev - m_next)
        l_s[...] = alpha * l_s[...] + jnp.sum(p, axis=1, keepdims=True)
        acc_s[...] = alpha * acc_s[...] + jnp.dot(p.astype(BF16), v_ref[0, rows, :], preferred_element_type=F32)
        m_s[...] = m_next

    def body(j, carry):
        step(j, False)
        return carry

    lax.fori_loop(0, qi, body, 0)
    step(qi, True)
    o_ref[0] = (acc_s[...] / l_s[...]).astype(o_ref.dtype)


def _fox(qkv, sm, gq, gk):
    b, s, _ = qkv.shape
    nh = FOX_HEADS
    tq = _tile(s, 512)
    nk = s // tq
    return pl.pallas_call(
        functools.partial(_fox_kernel, tq=tq, nk=nk),
        out_shape=jax.ShapeDtypeStruct((b, s, nh * FOX_DH), BF16),
        grid=(b, nh, s // tq),
        in_specs=[
            pl.BlockSpec((1, tq, FOX_DH), lambda i, h, j: (i, j, h)),
            pl.BlockSpec((1, s, FOX_DH), lambda i, h, j: (i, 0, nh + h)),
            pl.BlockSpec((1, s, FOX_DH), lambda i, h, j: (i, 0, 2 * nh + h)),
            pl.BlockSpec((1, tq, LANES), lambda i, h, j: (i, j, 0)),
            pl.BlockSpec((1, s, LANES), lambda i, h, j: (i, 0, 0)),
            pl.BlockSpec((1, FOX_DH), lambda i, h, j: (0, 0)),
            pl.BlockSpec((1, FOX_DH), lambda i, h, j: (0, 0)),
        ],
        out_specs=pl.BlockSpec((1, tq, FOX_DH), lambda i, h, j: (i, j, h)),
        scratch_shapes=[
            pltpu.VMEM((s, 2 * FOX_DH), BF16),
            pltpu.VMEM((tq, LANES), F32),
            pltpu.VMEM((tq, LANES), F32),
            pltpu.VMEM((tq, FOX_DH), F32),
        ],
        compiler_params=_cparams(("arbitrary", "arbitrary", "arbitrary")),
        name="fox",
    )(qkv, qkv, qkv, sm, sm, gq, gk)


def _memkv_kernel(mem_ref, g_ref, w_ref, gk_ref, o_ref, *, n_k_tiles):
    j = pl.program_id(1)
    m = mem_ref[0]
    hn = (m * lax.rsqrt(jnp.mean(m * m, axis=-1, keepdims=True) + EPS) * g_ref[...]).astype(BF16)
    r = jnp.dot(hn, w_ref[...], preferred_element_type=F32)
    parts = []
    for t in range(r.shape[1] // MEM_DH):
        rt = r[:, t * MEM_DH:(t + 1) * MEM_DH]
        parts.append(rt * lax.rsqrt(jnp.mean(rt * rt, axis=-1, keepdims=True) + EPS) * gk_ref[...])
    normed = jnp.concatenate(parts, axis=1)
    is_k = jnp.where(j < n_k_tiles, 1.0, 0.0)
    o_ref[0] = (is_k * normed + (1.0 - is_k) * r).astype(o_ref.dtype)


def _memkv(mem, g, w, gk):
    b, ml, d = mem.shape
    n = w.shape[1]
    tn = 2 * MEM_DH
    return pl.pallas_call(
        functools.partial(_memkv_kernel, n_k_tiles=(n // 2) // tn),
        out_shape=jax.ShapeDtypeStruct((b, ml, n), BF16),
        grid=(b, n // tn),
        in_specs=[
            pl.BlockSpec((1, ml, d), lambda i, j: (i, 0, 0)),
            pl.BlockSpec((1, d), lambda i, j: (0, 0)),
            pl.BlockSpec((d, tn), lambda i, j: (0, j)),
            pl.BlockSpec((1, MEM_DH), lambda i, j: (0, 0)),
        ],
        out_specs=pl.BlockSpec((1, ml, tn), lambda i, j: (i, 0, j)),
        compiler_params=_cparams(("arbitrary", "arbitrary")),
        name="memkv",
    )(mem, g, w, gk)


def _mem_kernel(q_ref, k_ref, v_ref, gq_ref, o_ref):
    for hd in range(MEM_HEADS):
        cs = slice(hd * MEM_DH, (hd + 1) * MEM_DH)
        qq = q_ref[0, :, cs].astype(F32)
        qn = qq * lax.rsqrt(jnp.mean(qq * qq, axis=-1, keepdims=True) + EPS) * gq_ref[...] * (MEM_DH ** -0.5)
        s = _dot_nt(qn, k_ref[0, :, cs])
        p = jnp.exp(s - jnp.max(s, axis=1, keepdims=True))
        p = p / jnp.sum(p, axis=1, keepdims=True)
        o_ref[0, :, cs] = jnp.dot(p.astype(BF16), v_ref[0, :, cs], preferred_element_type=F32).astype(o_ref.dtype)


def _mem_attn(qsrc, q_col_block, kv, gq):
    b, s, _ = qsrc.shape
    ml = kv.shape[1]
    wq = MEM_HEADS * MEM_DH
    tq = _tile(s, 512)
    return pl.pallas_call(
        _mem_kernel,
        out_shape=jax.ShapeDtypeStruct((b, s, wq), BF16),
        grid=(b, s // tq),
        in_specs=[
            pl.BlockSpec((1, tq, wq), lambda i, j: (i, j, q_col_block)),
            pl.BlockSpec((1, ml, wq), lambda i, j: (i, 0, 0)),
            pl.BlockSpec((1, ml, wq), lambda i, j: (i, 0, 1)),
            pl.BlockSpec((1, MEM_DH), lambda i, j: (0, 0)),
        ],
        out_specs=pl.BlockSpec((1, tq, wq), lambda i, j: (i, j, 0)),
        compiler_params=_cparams(("arbitrary", "arbitrary")),
        name="mem_attn",
    )(qsrc, kv, kv, gq)


def _merge_kernel(oa_ref, ob_ref, om_ref, wa_ref, wb_ref, wm_ref, ga_ref, gb_ref, gm_ref, y_ref):
    d = functools.partial(jnp.dot, preferred_element_type=F32)
    y = (ga_ref[...].astype(F32) * d(oa_ref[...], wa_ref[...])
         + gb_ref[...].astype(F32) * d(ob_ref[...], wb_ref[...])
         + gm_ref[...].astype(F32) * d(om_ref[...], wm_ref[...]))
    y_ref[...] = y.astype(y_ref.dtype)


def _merge(oa, ob, om, wa, wb, wm, gates):
    t, ka = oa.shape
    dm = wa.shape[1]
    tm = _tile(t, 1024)
    tn = _tile(dm, 512)
    nb = dm // tn
    a_spec = lambda kk: pl.BlockSpec((tm, kk), lambda i, j: (i, 0))
    w_spec = lambda kk: pl.BlockSpec((kk, tn), lambda i, j: (0, j))
    g_spec = lambda off: pl.BlockSpec((tm, tn), lambda i, j: (i, off * nb + j))
    return pl.pallas_call(
        _merge_kernel,
        out_shape=jax.ShapeDtypeStruct((t, dm), BF16),
        grid=(t // tm, nb),
        in_specs=[a_spec(ka), a_spec(ob.shape[1]), a_spec(om.shape[1]),
                  w_spec(ka), w_spec(ob.shape[1]), w_spec(om.shape[1]),
                  g_spec(0), g_spec(1), g_spec(2)],
        out_specs=pl.BlockSpec((tm, tn), lambda i, j: (i, j)),
        compiler_params=_cparams(("arbitrary", "arbitrary")),
        name="merge",
    )(oa, ob, om, wa, wb, wm, gates, gates, gates)


def _layer(x, mem, g_mix, w_in, conv_w, a_log, dt_bias, gdn_norm_g, fox_b_f, fox_q_norm, fox_k_norm,
           g_mem, w_mem_kv, mem_q_norm, mem_k_norm, w_up_gdn, w_up_fox, w_up_mem, w_out, g_mlp, w_ff1, w_ff2):
    b, s, d = x.shape
    t = b * s
    gdn_qk = GDN_HEADS * GDN_DK
    gdn_v = GDN_HEADS * GDN_DV
    fox_w = FOX_HEADS * FOX_DH
    mem_w = MEM_HEADS * MEM_DH
    o_z = 2 * gdn_qk + gdn_v
    o_beta = o_z + gdn_v
    o_dec = o_beta + GDN_HEADS
    o_fq = o_dec + GDN_HEADS
    o_ff = o_fq + 3 * fox_w
    o_mq = o_ff + FOX_HEADS
    o_gate = o_mq + mem_w

    w_gdn = w_in[:, :o_beta].astype(BF16)
    w_att = jnp.concatenate([w_in[:, o_fq:o_ff], w_in[:, o_mq:o_gate]], axis=1).astype(BF16)
    w_gate = w_in[:, o_gate:].astype(BF16)
    n_small = 2 * GDN_HEADS + FOX_HEADS
    w_small = jnp.concatenate([w_in[:, o_beta:o_fq], w_in[:, o_ff:o_mq],
                               jnp.zeros((d, LANES - n_small), F32)], axis=1)
    zpad = jnp.zeros((LANES - n_small,), F32)
    bias = jnp.concatenate([jnp.zeros((GDN_HEADS,), F32), dt_bias.astype(F32), fox_b_f.astype(F32), zpad])[None, :]
    alog = jnp.concatenate([jnp.zeros((GDN_HEADS,), F32), a_log.astype(F32), jnp.zeros((FOX_HEADS,), F32), zpad])[None, :]

    h, sm = _norm_small(x, g_mix[None, :], w_small, bias, alog)
    h2d = h.reshape(t, d)

    qkvz = _matmul(h2d, w_gdn, out_dtype=F32, tm=1024, tn=512, name="proj_gdn").reshape(b, s, -1)
    att = _matmul(h2d, w_att, out_dtype=BF16, tm=1024, tn=512, name="proj_att").reshape(b, s, -1)
    gates = _matmul(h2d, w_gate, out_dtype=BF16, tm=1024, tn=512, act="sigmoid", name="proj_gate")

    o_a = _gdn(qkvz, sm, conv_w, gdn_norm_g[None, :])
    o_b = _fox(att, sm, fox_q_norm[None, :], fox_k_norm[None, :])
    kv_m = _memkv(mem, g_mem[None, :], w_mem_kv.astype(BF16), mem_k_norm[None, :])
    o_m = _mem_attn(att, (3 * fox_w) // mem_w, kv_m, mem_q_norm[None, :])

    y = _merge(o_a.reshape(t, -1), o_b.reshape(t, -1), o_m.reshape(t, -1),
               w_up_gdn.astype(BF16), w_up_fox.astype(BF16), w_up_mem.astype(BF16), gates)
    x1 = _matmul(y, w_out.astype(BF16), out_dtype=F32, tm=1024, tn=512, residual=x.reshape(t, d), name="out_proj")

    h2 = _rmsnorm(x1, g_mlp[None, :])
    u = _matmul(h2, w_ff1.astype(BF16), out_dtype=BF16, tm=1024, tn=512, act="relu2", name="ff1")
    out = _matmul(u, w_ff2.astype(BF16), out_dtype=F32, tm=512, tn=512, residual=x1, name="ff2")
    return out.reshape(b, s, d)


def kernel(x, mem, g_mix, w_in, conv_w, a_log, dt_bias, gdn_norm_g, fox_b_f, fox_q_norm, fox_k_norm, g_mem, w_mem_kv, mem_q_norm, mem_k_norm, w_up_gdn, w_up_fox, w_up_mem, w_out, g_mlp, w_ff1, w_ff2):
    depth = w_in.shape[0]
    for l in range(depth):
        x = _layer(x, mem, g_mix[l], w_in[l], conv_w[l], a_log[l], dt_bias[l], gdn_norm_g[l], fox_b_f[l],
                   fox_q_norm[l], fox_k_norm[l], g_mem[l], w_mem_kv[l], mem_q_norm[l], mem_k_norm[l],
                   w_up_gdn[l], w_up_fox[l], w_up_mem[l], w_out[l], g_mlp[l], w_ff1[l], w_ff2[l])
    return x
```

```python
import functools
import math

import jax
import jax.numpy as jnp
from jax import lax
from jax.experimental import pallas as pl
from jax.experimental.pallas import tpu as pltpu

F32 = jnp.float32
BF16 = jnp.bfloat16
EPS = 1e-6

GDN_HEADS = 8
GDN_DK = 128
GDN_DV = 128
GDN_CONV = 4
GDN_CHUNK = 128
GDN_HEADS_PER_STEP = 4
FOX_HEADS = 8
FOX_DH = 128
FOX_SUBTILES = 2
MEM_HEADS = 4
MEM_DH = 256
N_BRANCH = 3
LANES = 128
NEG_BIG = -1e30
LOG2E = math.log2(math.e)

VMEM_LIMIT = 56 * 1024 * 1024


def _tile(n, pref):
    return pref if n % pref == 0 else n


def _cparams(sem):
    return pltpu.CompilerParams(dimension_semantics=sem, vmem_limit_bytes=VMEM_LIMIT)


def _dot(a, b):
    return jnp.dot(a.astype(BF16), b.astype(BF16), preferred_element_type=F32)


def _dot_nt(a, b):
    return lax.dot_general(a.astype(BF16), b.astype(BF16), (((1,), (1,)), ((), ())),
                           preferred_element_type=F32)


def _split2(a):
    hi = a.astype(BF16)
    lo = (a - hi.astype(F32)).astype(BF16)
    return hi, lo


def _split3(a):
    hi = a.astype(BF16).astype(F32)
    r = a - hi
    mid = r.astype(BF16).astype(F32)
    lo = (r - mid).astype(BF16).astype(F32)
    return hi, mid, lo


def _dot3(a, b):
    a_hi, a_lo = _split2(a)
    b_hi, b_lo = _split2(b)
    d = functools.partial(jnp.dot, preferred_element_type=F32)
    return d(a_hi, b_hi) + d(a_hi, b_lo) + d(a_lo, b_hi)


def _dot2(a, b_bf16):
    a_hi, a_lo = _split2(a)
    d = functools.partial(jnp.dot, preferred_element_type=F32)
    return d(a_hi, b_bf16) + d(a_lo, b_bf16)


def _dot_exact_lhs(l_bf16, v):
    hi, mid, lo = _split3(v)
    d = functools.partial(jnp.dot, preferred_element_type=F32)
    return d(l_bf16, hi.astype(BF16)) + d(l_bf16, mid.astype(BF16)) + d(l_bf16, lo.astype(BF16))


def _lane_col(a, idx):
    lane = lax.broadcasted_iota(jnp.int32, a.shape, 1)
    return jnp.sum(jnp.where(lane == idx, a, 0.0), axis=1, keepdims=True)


def _softplus(x):
    return jnp.maximum(x, 0.0) + jnp.log1p(jnp.exp(-jnp.abs(x)))


def _norm_small_kernel(x_ref, g_ref, ws_ref, bias_ref, alog_ref, h_ref, sm_ref, carry_ref, *, tm):
    s = pl.program_id(1)

    @pl.when(s == 0)
    def _():
        carry_ref[...] = jnp.zeros_like(carry_ref)

    x = x_ref[0]
    h = x * lax.rsqrt(jnp.mean(x * x, axis=-1, keepdims=True) + EPS) * g_ref[...]
    h_ref[0] = h.astype(BF16)

    pre = _dot3(h, ws_ref[...]) + bias_ref[...]
    lane = lax.broadcasted_iota(jnp.int32, pre.shape, 1)
    nh = GDN_HEADS
    beta = jax.nn.sigmoid(pre)
    gdec = -jnp.exp(alog_ref[...]) * _softplus(pre)
    logf = -_softplus(-pre)
    vals = jnp.where(lane < nh, beta, jnp.where(lane < 2 * nh, gdec, jnp.where(lane < 3 * nh, logf, 0.0)))

    row = lax.broadcasted_iota(jnp.int32, (tm, tm), 0)
    col = lax.broadcasted_iota(jnp.int32, (tm, tm), 1)
    low = col <= row
    l_full = jnp.where(low, 1.0, 0.0).astype(BF16)
    sh = GDN_CHUNK.bit_length() - 1
    same_chunk = (row >> sh) == (col >> sh)
    l_blk = jnp.where(low, jnp.where(same_chunk, 1.0, 0.0), 0.0).astype(BF16)
    cs_blk = _dot_exact_lhs(l_blk, vals)
    cs_full = _dot_exact_lhs(l_full, vals) + carry_ref[0:1, :]
    sm_ref[0] = jnp.where(lane < nh, vals, jnp.where(lane < 2 * nh, cs_blk, jnp.where(lane < 3 * nh, cs_full, 0.0)))
    carry_ref[...] = jnp.broadcast_to(cs_full[tm - 1:tm, :], carry_ref.shape)


def _norm_small(x, g, w_small, bias, alog):
    b, s, d = x.shape
    tm = _tile(s, 512)
    return pl.pallas_call(
        functools.partial(_norm_small_kernel, tm=tm),
        out_shape=(jax.ShapeDtypeStruct((b, s, d), BF16), jax.ShapeDtypeStruct((b, s, LANES), F32)),
        grid=(b, s // tm),
        in_specs=[
            pl.BlockSpec((1, tm, d), lambda i, j: (i, j, 0)),
            pl.BlockSpec((1, d), lambda i, j: (0, 0)),
            pl.BlockSpec((d, LANES), lambda i, j: (0, 0)),
            pl.BlockSpec((1, LANES), lambda i, j: (0, 0)),
            pl.BlockSpec((1, LANES), lambda i, j: (0, 0)),
        ],
        out_specs=(
            pl.BlockSpec((1, tm, d), lambda i, j: (i, j, 0)),
            pl.BlockSpec((1, tm, LANES), lambda i, j: (i, j, 0)),
        ),
        scratch_shapes=[pltpu.VMEM((8, LANES), F32)],
        compiler_params=_cparams(("arbitrary", "arbitrary")),
        name="norm_small",
    )(x, g, w_small, bias, alog)


def _rmsnorm_kernel(x_ref, g_ref, o_ref):
    x = x_ref[...]
    o_ref[...] = (x * lax.rsqrt(jnp.mean(x * x, axis=-1, keepdims=True) + EPS) * g_ref[...]).astype(o_ref.dtype)


def _rmsnorm(x2d, g):
    t, d = x2d.shape
    tm = _tile(t, 512)
    return pl.pallas_call(
        _rmsnorm_kernel,
        out_shape=jax.ShapeDtypeStruct((t, d), BF16),
        grid=(t // tm,),
        in_specs=[pl.BlockSpec((tm, d), lambda i: (i, 0)), pl.BlockSpec((1, d), lambda i: (0, 0))],
        out_specs=pl.BlockSpec((tm, d), lambda i: (i, 0)),
        compiler_params=_cparams(("arbitrary",)),
        name="rmsnorm",
    )(x2d, g)


def _mm_kernel(a_ref, w_ref, *rest, act, has_res):
    if has_res:
        r_ref, o_ref = rest
    else:
        (o_ref,) = rest
    acc = jnp.dot(a_ref[...], w_ref[...], preferred_element_type=F32)
    if act == "sigmoid":
        acc = jax.nn.sigmoid(acc)
    elif act == "relu2":
        r = jnp.maximum(acc, 0.0)
        acc = r * r
    if has_res:
        acc = acc + r_ref[...]
    o_ref[...] = acc.astype(o_ref.dtype)


def _matmul(a, w, *, out_dtype, tm, tn, act=None, residual=None, name="matmul"):
    m, k = a.shape
    n = w.shape[1]
    tm = _tile(m, tm)
    tn = _tile(n, tn)
    in_specs = [pl.BlockSpec((tm, k), lambda i, j: (i, 0)), pl.BlockSpec((k, tn), lambda i, j: (0, j))]
    args = [a, w]
    if residual is not None:
        in_specs.append(pl.BlockSpec((tm, tn), lambda i, j: (i, j)))
        args.append(residual)
    return pl.pallas_call(
        functools.partial(_mm_kernel, act=act, has_res=residual is not None),
        out_shape=jax.ShapeDtypeStruct((m, n), out_dtype),
        grid=(m // tm, n // tn),
        in_specs=in_specs,
        out_specs=pl.BlockSpec((tm, tn), lambda i, j: (i, j)),
        compiler_params=_cparams(("arbitrary", "arbitrary")),
        name=name,
    )(*args)


def _gdn_kernel(q_ref, k_ref, v_ref, z_ref, sm_ref, cwq_ref, cwk_ref, cwv_ref, gn_ref, o_ref,
                state_ref, tail_ref, ext_ref, qs, ks, vs, *, ts, nc, hb):
    hg = pl.program_id(1)
    si = pl.program_id(2)
    c_sz = GDN_CHUNK

    @pl.when(si == 0)
    def _():
        state_ref[...] = jnp.zeros_like(state_ref)
        tail_ref[...] = jnp.zeros_like(tail_ref)

    def conv_silu(x_ref, w_ref, idx):
        x = x_ref[0]
        ext_ref[0:8, :] = tail_ref[idx]
        ext_ref[8:8 + ts, :] = x
        w = w_ref[...]
        y = (ext_ref[5:5 + ts, :] * w[0:1, :] + ext_ref[6:6 + ts, :] * w[1:2, :]
             + ext_ref[7:7 + ts, :] * w[2:3, :] + x * w[3:4, :])
        tail_ref[idx] = x[ts - 8:ts, :]
        return y * jax.nn.sigmoid(y)

    q = conv_silu(q_ref, cwq_ref, 0)
    k = conv_silu(k_ref, cwk_ref, 1)
    vs[...] = conv_silu(v_ref, cwv_ref, 2)
    for hh in range(hb):
        cs = slice(hh * LANES, (hh + 1) * LANES)
        qh = q[:, cs]
        kh = k[:, cs]
        qs[:, cs] = qh * lax.rsqrt(jnp.sum(qh * qh, axis=-1, keepdims=True) + EPS) * (GDN_DK ** -0.5)
        ks[:, cs] = kh * lax.rsqrt(jnp.sum(kh * kh, axis=-1, keepdims=True) + EPS)

    ri = lax.broadcasted_iota(jnp.int32, (c_sz, c_sz), 0)
    ci = lax.broadcasted_iota(jnp.int32, (c_sz, c_sz), 1)
    strict = ri > ci
    incl = ri >= ci
    eye = jnp.where(ri == ci, 1.0, 0.0)
    n_lvl = c_sz.bit_length() - 1
    lvl_masks = []
    for l in range(n_lvl):
        same = (ri >> (l + 1)) == (ci >> (l + 1))
        lvl_masks.append(jnp.where(
            same, jnp.where(((ri >> l) & 1) == 1, jnp.where(((ci >> l) & 1) == 0, 1.0, 0.0), 0.0), 0.0))

    def body(c, carry):
        rows = pl.ds(pl.multiple_of(c * c_sz, c_sz), c_sz)
        smc = sm_ref[0, rows, :]
        mlows, rhss, lhs2s, qds, cds = [], [], [], [], []
        for hh in range(hb):
            cs = slice(hh * LANES, (hh + 1) * LANES)
            hd = hg * hb + hh
            beta = _lane_col(smc, hd)
            gam = _lane_col(smc, hd + GDN_HEADS)
            kc = ks[rows, cs]
            qc = qs[rows, cs]
            egam = jnp.exp(gam)
            g_last = gam[c_sz - 1:c_sz, :]
            gcol = jnp.broadcast_to(gam, (c_sz, c_sz))
            diff = gcol - gcol.T
            e = jnp.exp(jnp.where(incl, diff, 0.0))
            mlows.append(beta * _dot_nt(kc, kc) * jnp.where(strict, e, 0.0))
            qk = _dot_nt(qc, kc) * jnp.where(incl, e, 0.0)
            kd_t = (kc * jnp.exp(g_last - gam)).T
            lhs2s.append(jnp.concatenate([qk, kd_t], axis=0).astype(BF16))
            rhss.append(jnp.concatenate([kc * (beta * egam), vs[rows, cs] * beta], axis=1).astype(BF16))
            qds.append(qc * egam)
            cds.append(jnp.exp(g_last))
        xs = [eye - lvl_masks[0] * m for m in mlows]
        for l in range(1, n_lvl):
            tl = [_dot(lvl_masks[l] * m, x) for m, x in zip(mlows, xs)]
            xs = [x - _dot(x, t) for x, t in zip(xs, tl)]
        rs = [eye - x - _dot3(m, x) for m, x in zip(mlows, xs)]
        xs = [x + _dot(x, r) for x, r in zip(xs, rs)]
        wus = [_dot2(x, r) for x, r in zip(xs, rhss)]
        for hh in range(hb):
            cs = slice(hh * LANES, (hh + 1) * LANES)
            state = state_ref[hh]
            wu = wus[hh]
            a = _dot(jnp.concatenate([wu[:, :GDN_DK], qds[hh]], axis=0), state)
            u = wu[:, GDN_DK:] - a[:c_sz]
            b2 = jnp.dot(lhs2s[hh], u.astype(BF16), preferred_element_type=F32)
            o = a[c_sz:] + b2[:c_sz]
            state_ref[hh] = state * cds[hh] + b2[c_sz:]
            z = z_ref[0, rows, cs]
            on = o * lax.rsqrt(jnp.mean(o * o, axis=-1, keepdims=True) + EPS) * gn_ref[...]
            o_ref[0, rows, cs] = (on * (z * jax.nn.sigmoid(z))).astype(o_ref.dtype)
        return carry

    lax.fori_loop(0, nc, body, 0)


def _gdn(qkvz, sm, conv_w, gn):
    b, s, _ = qkvz.shape
    nh = GDN_HEADS
    hb = GDN_HEADS_PER_STEP
    ng = nh // hb
    wb = hb * LANES
    ts = _tile(s, 512)
    nc = ts // GDN_CHUNK
    blk = lambda off: pl.BlockSpec((1, ts, wb), lambda i, h, j: (i, j, off + h))
    cw = lambda off: pl.BlockSpec((GDN_CONV, wb), lambda i, h, j: (0, off + h))
    return pl.pallas_call(
        functools.partial(_gdn_kernel, ts=ts, nc=nc, hb=hb),
        out_shape=jax.ShapeDtypeStruct((b, s, nh * GDN_DV), BF16),
        grid=(b, ng, s // ts),
        in_specs=[
            blk(0), blk(ng), blk(2 * ng), blk(3 * ng),
            pl.BlockSpec((1, ts, LANES), lambda i, h, j: (i, j, 0)),
            cw(0), cw(ng), cw(2 * ng),
            pl.BlockSpec((1, GDN_DV), lambda i, h, j: (0, 0)),
        ],
        out_specs=pl.BlockSpec((1, ts, wb), lambda i, h, j: (i, j, h)),
        scratch_shapes=[
            pltpu.VMEM((hb, GDN_DK, GDN_DV), F32),
            pltpu.VMEM((3, 8, wb), F32),
            pltpu.VMEM((ts + 8, wb), F32),
            pltpu.VMEM((ts, wb), F32),
            pltpu.VMEM((ts, wb), F32),
            pltpu.VMEM((ts, wb), F32),
        ],
        compiler_params=_cparams(("arbitrary", "arbitrary", "arbitrary")),
        name="gdn",
    )(qkvz, qkvz, qkvz, qkvz, sm, conv_w, conv_w, conv_w, gn)


def _fox_kernel(q_ref, k_ref, v_ref, smq_ref, smk_ref, gq_ref, gk_ref, o_ref,
                kaug_ref, vaug_ref, m_s, acc_s, *, tq, nk):
    hd = pl.program_id(1)
    qi = pl.program_id(2)
    cidx = hd + 2 * GDN_HEADS
    lane = lax.broadcasted_iota(jnp.int32, (tq, LANES), 1)

    @pl.when(qi == 0)
    def _():
        def build(j, carry):
            rows = pl.ds(pl.multiple_of(j * tq, tq), tq)
            kk = k_ref[0, rows, :].astype(F32)
            kn = kk * lax.rsqrt(jnp.mean(kk * kk, axis=-1, keepdims=True) + EPS) * gk_ref[...]
            hi, mid, lo = _split3(_lane_col(smk_ref[0, rows, :], cidx) * LOG2E)
            aug = jnp.where(lane < 3, 1.0, jnp.where(lane == 3, -hi, jnp.where(lane == 4, -mid, jnp.where(lane == 5, -lo, 0.0))))
            kaug_ref[rows, 0:FOX_DH] = kn.astype(BF16)
            kaug_ref[rows, FOX_DH:2 * FOX_DH] = aug.astype(BF16)
            vaug_ref[rows, 0:FOX_DH] = v_ref[0, rows, :]
            vaug_ref[rows, FOX_DH:2 * FOX_DH] = jnp.ones((tq, FOX_DH), BF16)
            return carry
        lax.fori_loop(0, nk, build, 0)

    qq = q_ref[0].astype(F32)
    qn = qq * lax.rsqrt(jnp.mean(qq * qq, axis=-1, keepdims=True) + EPS) * gq_ref[...] * (FOX_DH ** -0.5 * LOG2E)
    hi, mid, lo = _split3(_lane_col(smq_ref[0], cidx) * LOG2E)
    aug = jnp.where(lane == 0, hi, jnp.where(lane == 1, mid, jnp.where(lane == 2, lo, jnp.where(lane < 6, 1.0, 0.0))))
    q_aug = jnp.concatenate([qn.astype(BF16), aug.astype(BF16)], axis=1)

    m_s[...] = jnp.full_like(m_s, NEG_BIG)
    acc_s[...] = jnp.zeros_like(acc_s)

    tr = tq // FOX_SUBTILES
    n_lt = tq // LANES

    def scores(j):
        rows = pl.ds(pl.multiple_of(j * tq, tq), tq)
        return lax.dot_general(q_aug, kaug_ref[rows, :], (((1,), (1,)), ((), ())), preferred_element_type=F32)

    def accumulate(s, j, masked):
        rows = pl.ds(pl.multiple_of(j * tq, tq), tq)
        v_blk = vaug_ref[rows, :]
        for r in range(FOX_SUBTILES):
            rs = slice(r * tr, (r + 1) * tr)
            sr = s[rs]
            if masked:
                ri = lax.broadcasted_iota(jnp.int32, (tr, tq), 0) + r * tr
                ci = lax.broadcasted_iota(jnp.int32, (tr, tq), 1)
                sr = jnp.where(ri >= ci, sr, NEG_BIG)
            tiles = [sr[:, c * LANES:(c + 1) * LANES] for c in range(n_lt)]
            mx = tiles[0]
            for t in tiles[1:]:
                mx = jnp.maximum(mx, t)
            m_prev = m_s[rs, :]
            m_next = jnp.maximum(m_prev, jnp.max(mx, axis=1, keepdims=True))
            p = jnp.concatenate([jnp.exp2(t - m_next) for t in tiles], axis=1).astype(BF16)
            alpha = jnp.exp2(m_prev - m_next)
            acc_s[rs, :] = (jnp.concatenate([alpha, alpha], axis=1) * acc_s[rs, :]
                            + jnp.dot(p, v_blk, preferred_element_type=F32))
            m_s[rs, :] = m_next

    def body(j, s_cur):
        s_next = scores(j + 1)
        accumulate(s_cur, j, False)
        return s_next

    s_last = lax.fori_loop(0, qi, body, scores(0))
    accumulate(s_last, qi, True)
    acc = acc_s[...]
    o_ref[0] = (acc[:, :FOX_DH] / acc[:, FOX_DH:]).astype(o_ref.dtype)


def _fox(qkv, sm, gq, gk):
    b, s, _ = qkv.shape
    nh = FOX_HEADS
    tq = _tile(s, 512)
    nk = s // tq
    return pl.pallas_call(
        functools.partial(_fox_kernel, tq=tq, nk=nk),
        out_shape=jax.ShapeDtypeStruct((b, s, nh * FOX_DH), BF16),
        grid=(b, nh, s // tq),
        in_specs=[
            pl.BlockSpec((1, tq, FOX_DH), lambda i, h, j: (i, j, h)),
            pl.BlockSpec((1, s, FOX_DH), lambda i, h, j: (i, 0, nh + h)),
            pl.BlockSpec((1, s, FOX_DH), lambda i, h, j: (i, 0, 2 * nh + h)),
            pl.BlockSpec((1, tq, LANES), lambda i, h, j: (i, j, 0)),
            pl.BlockSpec((1, s, LANES), lambda i, h, j: (i, 0, 0)),
            pl.BlockSpec((1, FOX_DH), lambda i, h, j: (0, 0)),
            pl.BlockSpec((1, FOX_DH), lambda i, h, j: (0, 0)),
        ],
        out_specs=pl.BlockSpec((1, tq, FOX_DH), lambda i, h, j: (i, j, h)),
        scratch_shapes=[
            pltpu.VMEM((s, 2 * FOX_DH), BF16),
            pltpu.VMEM((s, 2 * FOX_DH), BF16),
            pltpu.VMEM((tq, LANES), F32),
            pltpu.VMEM((tq, 2 * FOX_DH), F32),
        ],
        compiler_params=_cparams(("arbitrary", "arbitrary", "arbitrary")),
        name="fox",
    )(qkv, qkv, qkv, sm, sm, gq, gk)


def _memkv_kernel(mem_ref, g_ref, w_ref, gk_ref, o_ref, *, n_k_tiles):
    j = pl.program_id(1)
    m = mem_ref[0]
    hn = (m * lax.rsqrt(jnp.mean(m * m, axis=-1, keepdims=True) + EPS) * g_ref[...]).astype(BF16)
    r = jnp.dot(hn, w_ref[...], preferred_element_type=F32)
    parts = []
    for t in range(r.shape[1] // MEM_DH):
        rt = r[:, t * MEM_DH:(t + 1) * MEM_DH]
        parts.append(rt * lax.rsqrt(jnp.mean(rt * rt, axis=-1, keepdims=True) + EPS) * gk_ref[...])
    normed = jnp.concatenate(parts, axis=1)
    is_k = jnp.where(j < n_k_tiles, 1.0, 0.0)
    o_ref[0] = (is_k * normed + (1.0 - is_k) * r).astype(o_ref.dtype)


def _memkv(mem, g, w, gk):
    b, ml, d = mem.shape
    n = w.shape[1]
    tn = 2 * MEM_DH
    return pl.pallas_call(
        functools.partial(_memkv_kernel, n_k_tiles=(n // 2) // tn),
        out_shape=jax.ShapeDtypeStruct((b, ml, n), BF16),
        grid=(b, n // tn),
        in_specs=[
            pl.BlockSpec((1, ml, d), lambda i, j: (i, 0, 0)),
            pl.BlockSpec((1, d), lambda i, j: (0, 0)),
            pl.BlockSpec((d, tn), lambda i, j: (0, j)),
            pl.BlockSpec((1, MEM_DH), lambda i, j: (0, 0)),
        ],
        out_specs=pl.BlockSpec((1, ml, tn), lambda i, j: (i, 0, j)),
        compiler_params=_cparams(("arbitrary", "arbitrary")),
        name="memkv",
    )(mem, g, w, gk)


def _mem_kernel(q_ref, k_ref, v_ref, gq_ref, o_ref):
    for hd in range(MEM_HEADS):
        cs = slice(hd * MEM_DH, (hd + 1) * MEM_DH)
        qq = q_ref[0, :, cs].astype(F32)
        qn = qq * lax.rsqrt(jnp.mean(qq * qq, axis=-1, keepdims=True) + EPS) * gq_ref[...] * (MEM_DH ** -0.5)
        s = _dot_nt(qn, k_ref[0, :, cs])
        p = jnp.exp(s - jnp.max(s, axis=1, keepdims=True))
        p = p / jnp.sum(p, axis=1, keepdims=True)
        o_ref[0, :, cs] = jnp.dot(p.astype(BF16), v_ref[0, :, cs], preferred_element_type=F32).astype(o_ref.dtype)


def _mem_attn(qsrc, q_col_block, kv, gq):
    b, s, _ = qsrc.shape
    ml = kv.shape[1]
    wq = MEM_HEADS * MEM_DH
    tq = _tile(s, 512)
    return pl.pallas_call(
        _mem_kernel,
        out_shape=jax.ShapeDtypeStruct((b, s, wq), BF16),
        grid=(b, s // tq),
        in_specs=[
            pl.BlockSpec((1, tq, wq), lambda i, j: (i, j, q_col_block)),
            pl.BlockSpec((1, ml, wq), lambda i, j: (i, 0, 0)),
            pl.BlockSpec((1, ml, wq), lambda i, j: (i, 0, 1)),
            pl.BlockSpec((1, MEM_DH), lambda i, j: (0, 0)),
        ],
        out_specs=pl.BlockSpec((1, tq, wq), lambda i, j: (i, j, 0)),
        compiler_params=_cparams(("arbitrary", "arbitrary")),
        name="mem_attn",
    )(qsrc, kv, kv, gq)


def _merge_kernel(oa_ref, ob_ref, om_ref, wa_ref, wb_ref, wm_ref, ga_ref, gb_ref, gm_ref, y_ref):
    d = functools.partial(jnp.dot, preferred_element_type=F32)
    y = (ga_ref[...].astype(F32) * d(oa_ref[...], wa_ref[...])
         + gb_ref[...].astype(F32) * d(ob_ref[...], wb_ref[...])
         + gm_ref[...].astype(F32) * d(om_ref[...], wm_ref[...]))
    y_ref[...] = y.astype(y_ref.dtype)


def _merge(oa, ob, om, wa, wb, wm, gates):
    t, ka = oa.shape
    dm = wa.shape[1]
    tm = _tile(t, 1024)
    tn = _tile(dm, 512)
    nb = dm // tn
    a_spec = lambda kk: pl.BlockSpec((tm, kk), lambda i, j: (i, 0))
    w_spec = lambda kk: pl.BlockSpec((kk, tn), lambda i, j: (0, j))
    g_spec = lambda off: pl.BlockSpec((tm, tn), lambda i, j: (i, off * nb + j))
    return pl.pallas_call(
        _merge_kernel,
        out_shape=jax.ShapeDtypeStruct((t, dm), BF16),
        grid=(t // tm, nb),
        in_specs=[a_spec(ka), a_spec(ob.shape[1]), a_spec(om.shape[1]),
                  w_spec(ka), w_spec(ob.shape[1]), w_spec(om.shape[1]),
                  g_spec(0), g_spec(1), g_spec(2)],
        out_specs=pl.BlockSpec((tm, tn), lambda i, j: (i, j)),
        compiler_params=_cparams(("arbitrary", "arbitrary")),
        name="merge",
    )(oa, ob, om, wa, wb, wm, gates, gates, gates)


def _layer(x, mem, g_mix, w_in, conv_w, a_log, dt_bias, gdn_norm_g, fox_b_f, fox_q_norm, fox_k_norm,
           g_mem, w_mem_kv, mem_q_norm, mem_k_norm, w_up_gdn, w_up_fox, w_up_mem, w_out, g_mlp, w_ff1, w_ff2):
    b, s, d = x.shape
    t = b * s
    gdn_qk = GDN_HEADS * GDN_DK
    gdn_v = GDN_HEADS * GDN_DV
    fox_w = FOX_HEADS * FOX_DH
    mem_w = MEM_HEADS * MEM_DH
    o_z = 2 * gdn_qk + gdn_v
    o_beta = o_z + gdn_v
    o_dec = o_beta + GDN_HEADS
    o_fq = o_dec + GDN_HEADS
    o_ff = o_fq + 3 * fox_w
    o_mq = o_ff + FOX_HEADS
    o_gate = o_mq + mem_w

    w_gdn = w_in[:, :o_beta].astype(BF16)
    w_att = jnp.concatenate([w_in[:, o_fq:o_ff], w_in[:, o_mq:o_gate]], axis=1).astype(BF16)
    w_gate = w_in[:, o_gate:].astype(BF16)
    n_small = 2 * GDN_HEADS + FOX_HEADS
    w_small = jnp.concatenate([w_in[:, o_beta:o_fq], w_in[:, o_ff:o_mq],
                               jnp.zeros((d, LANES - n_small), F32)], axis=1)
    zpad = jnp.zeros((LANES - n_small,), F32)
    bias = jnp.concatenate([jnp.zeros((GDN_HEADS,), F32), dt_bias.astype(F32), fox_b_f.astype(F32), zpad])[None, :]
    alog = jnp.concatenate([jnp.zeros((GDN_HEADS,), F32), a_log.astype(F32), jnp.zeros((FOX_HEADS,), F32), zpad])[None, :]

    h, sm = _norm_small(x, g_mix[None, :], w_small, bias, alog)
    h2d = h.reshape(t, d)

    qkvz = _matmul(h2d, w_gdn, out_dtype=F32, tm=1024, tn=512, name="proj_gdn").reshape(b, s, -1)
    att = _matmul(h2d, w_att, out_dtype=BF16, tm=1024, tn=512, name="proj_att").reshape(b, s, -1)
    gates = _matmul(h2d, w_gate, out_dtype=BF16, tm=1024, tn=512, act="sigmoid", name="proj_gate")

    o_a = _gdn(qkvz, sm, conv_w, gdn_norm_g[None, :])
    o_b = _fox(att, sm, fox_q_norm[None, :], fox_k_norm[None, :])
    kv_m = _memkv(mem, g_mem[None, :], w_mem_kv.astype(BF16), mem_k_norm[None, :])
    o_m = _mem_attn(att, (3 * fox_w) // mem_w, kv_m, mem_q_norm[None, :])

    y = _merge(o_a.reshape(t, -1), o_b.reshape(t, -1), o_m.reshape(t, -1),
               w_up_gdn.astype(BF16), w_up_fox.astype(BF16), w_up_mem.astype(BF16), gates)
    x1 = _matmul(y, w_out.astype(BF16), out_dtype=F32, tm=1024, tn=512, residual=x.reshape(t, d), name="out_proj")

    h2 = _rmsnorm(x1, g_mlp[None, :])
    u = _matmul(h2, w_ff1.astype(BF16), out_dtype=BF16, tm=1024, tn=512, act="relu2", name="ff1")
    out = _matmul(u, w_ff2.astype(BF16), out_dtype=F32, tm=512, tn=512, residual=x1, name="ff2")
    return out.reshape(b, s, d)


def kernel(x, mem, g_mix, w_in, conv_w, a_log, dt_bias, gdn_norm_g, fox_b_f, fox_q_norm, fox_k_norm, g_mem, w_mem_kv, mem_q_norm, mem_k_norm, w_up_gdn, w_up_fox, w_up_mem, w_out, g_mlp, w_ff1, w_ff2):
    depth = w_in.shape[0]
    for l in range(depth):
        x = _layer(x, mem, g_mix[l], w_in[l], conv_w[l], a_log[l], dt_bias[l], gdn_norm_g[l], fox_b_f[l],
                   fox_q_norm[l], fox_k_norm[l], g_mem[l], w_mem_kv[l], mem_q_norm[l], mem_k_norm[l],
                   w_up_gdn[l], w_up_fox[l], w_up_mem[l], w_out[l], g_mlp[l], w_ff1[l], w_ff2[l])
    return x
```

```python
import functools
import math

import jax
import jax.numpy as jnp
from jax import lax
from jax.experimental import pallas as pl
from jax.experimental.pallas import tpu as pltpu

F32 = jnp.float32
BF16 = jnp.bfloat16
EPS = 1e-6

GDN_HEADS = 8
GDN_DK = 128
GDN_DV = 128
GDN_CONV = 4
GDN_CHUNK = 128
GDN_HEADS_PER_STEP = 8
FOX_HEADS = 8
FOX_DH = 128
FOX_SUBTILES = 2
MEM_HEADS = 4
MEM_DH = 256
N_BRANCH = 3
LANES = 128
NEG_BIG = -1e30
LOG2E = math.log2(math.e)

VMEM_LIMIT = 56 * 1024 * 1024


def _tile(n, pref):
    return pref if n % pref == 0 else n


def _cparams(sem):
    return pltpu.CompilerParams(dimension_semantics=sem, vmem_limit_bytes=VMEM_LIMIT)


def _dot(a, b):
    return jnp.dot(a.astype(BF16), b.astype(BF16), preferred_element_type=F32)


def _dot_nt(a, b):
    return lax.dot_general(a.astype(BF16), b.astype(BF16), (((1,), (1,)), ((), ())),
                           preferred_element_type=F32)


def _split2(a):
    hi = a.astype(BF16)
    lo = (a - hi.astype(F32)).astype(BF16)
    return hi, lo


def _split3(a):
    hi = a.astype(BF16).astype(F32)
    r = a - hi
    mid = r.astype(BF16).astype(F32)
    lo = (r - mid).astype(BF16).astype(F32)
    return hi, mid, lo


def _dot3(a, b):
    a_hi, a_lo = _split2(a)
    b_hi, b_lo = _split2(b)
    d = functools.partial(jnp.dot, preferred_element_type=F32)
    return d(a_hi, b_hi) + d(a_hi, b_lo) + d(a_lo, b_hi)


def _dot3_nt(a, b):
    a_hi, a_lo = _split2(a)
    b_hi, b_lo = _split2(b)
    d = lambda p, q: lax.dot_general(p, q, (((1,), (1,)), ((), ())), preferred_element_type=F32)
    return d(a_hi, b_hi) + d(a_hi, b_lo) + d(a_lo, b_hi)


def _dot2(a, b_bf16):
    a_hi, a_lo = _split2(a)
    d = functools.partial(jnp.dot, preferred_element_type=F32)
    return d(a_hi, b_bf16) + d(a_lo, b_bf16)


def _dot_exact_lhs(l_bf16, v):
    hi, mid, lo = _split3(v)
    d = functools.partial(jnp.dot, preferred_element_type=F32)
    return d(l_bf16, hi.astype(BF16)) + d(l_bf16, mid.astype(BF16)) + d(l_bf16, lo.astype(BF16))


def _lane_col(a, idx):
    lane = lax.broadcasted_iota(jnp.int32, a.shape, 1)
    return jnp.sum(jnp.where(lane == idx, a, 0.0), axis=1, keepdims=True)


def _softplus(x):
    return jnp.maximum(x, 0.0) + jnp.log1p(jnp.exp(-jnp.abs(x)))


def _norm_small_kernel(x_ref, g_ref, ws_ref, bias_ref, alog_ref, h_ref, sm_ref, carry_ref, *, tm):
    s = pl.program_id(1)

    @pl.when(s == 0)
    def _():
        carry_ref[...] = jnp.zeros_like(carry_ref)

    x = x_ref[0]
    h = x * lax.rsqrt(jnp.mean(x * x, axis=-1, keepdims=True) + EPS) * g_ref[...]
    h_ref[0] = h.astype(BF16)

    pre = _dot3_nt(h, ws_ref[...]) + bias_ref[...]
    lane = lax.broadcasted_iota(jnp.int32, pre.shape, 1)
    nh = GDN_HEADS
    beta = jax.nn.sigmoid(pre)
    gdec = -jnp.exp(alog_ref[...]) * _softplus(pre)
    logf = -_softplus(-pre)
    vals = jnp.where(lane < nh, beta, jnp.where(lane < 2 * nh, gdec, jnp.where(lane < 3 * nh, logf, 0.0)))

    row = lax.broadcasted_iota(jnp.int32, (tm, tm), 0)
    col = lax.broadcasted_iota(jnp.int32, (tm, tm), 1)
    low = col <= row
    l_full = jnp.where(low, 1.0, 0.0).astype(BF16)
    sh = GDN_CHUNK.bit_length() - 1
    same_chunk = (row >> sh) == (col >> sh)
    l_blk = jnp.where(low, jnp.where(same_chunk, 1.0, 0.0), 0.0).astype(BF16)
    cs_blk = _dot_exact_lhs(l_blk, vals)
    cs_full = _dot_exact_lhs(l_full, vals) + carry_ref[0:1, :]
    sm_ref[0] = jnp.where(lane < nh, vals, jnp.where(lane < 2 * nh, cs_blk, jnp.where(lane < 3 * nh, cs_full, 0.0)))
    carry_ref[...] = jnp.broadcast_to(cs_full[tm - 1:tm, :], carry_ref.shape)


def _norm_small(x, g, w_small, bias, alog):
    b, s, d = x.shape
    tm = _tile(s, 512)
    return pl.pallas_call(
        functools.partial(_norm_small_kernel, tm=tm),
        out_shape=(jax.ShapeDtypeStruct((b, s, d), BF16), jax.ShapeDtypeStruct((b, s, LANES), F32)),
        grid=(b, s // tm),
        in_specs=[
            pl.BlockSpec((1, tm, d), lambda i, j: (i, j, 0)),
            pl.BlockSpec((1, d), lambda i, j: (0, 0)),
            pl.BlockSpec((LANES, d), lambda i, j: (0, 0)),
            pl.BlockSpec((1, LANES), lambda i, j: (0, 0)),
            pl.BlockSpec((1, LANES), lambda i, j: (0, 0)),
        ],
        out_specs=(
            pl.BlockSpec((1, tm, d), lambda i, j: (i, j, 0)),
            pl.BlockSpec((1, tm, LANES), lambda i, j: (i, j, 0)),
        ),
        scratch_shapes=[pltpu.VMEM((8, LANES), F32)],
        compiler_params=_cparams(("arbitrary", "arbitrary")),
        name="norm_small",
    )(x, g, w_small, bias, alog)


def _rmsnorm_kernel(x_ref, g_ref, o_ref):
    x = x_ref[...]
    o_ref[...] = (x * lax.rsqrt(jnp.mean(x * x, axis=-1, keepdims=True) + EPS) * g_ref[...]).astype(o_ref.dtype)


def _rmsnorm(x2d, g):
    t, d = x2d.shape
    tm = _tile(t, 512)
    return pl.pallas_call(
        _rmsnorm_kernel,
        out_shape=jax.ShapeDtypeStruct((t, d), BF16),
        grid=(t // tm,),
        in_specs=[pl.BlockSpec((tm, d), lambda i: (i, 0)), pl.BlockSpec((1, d), lambda i: (0, 0))],
        out_specs=pl.BlockSpec((tm, d), lambda i: (i, 0)),
        compiler_params=_cparams(("arbitrary",)),
        name="rmsnorm",
    )(x2d, g)


def _mm_kernel(a_ref, w_ref, *rest, act, has_res, w_rows_are_outputs):
    if has_res:
        r_ref, o_ref = rest
    else:
        (o_ref,) = rest
    contract = (((1,), (1,)), ((), ())) if w_rows_are_outputs else (((1,), (0,)), ((), ()))
    acc = lax.dot_general(a_ref[...], w_ref[...], contract, preferred_element_type=F32)
    if act == "sigmoid":
        acc = jax.nn.sigmoid(acc)
    elif act == "relu2":
        r = jnp.maximum(acc, 0.0)
        acc = r * r
    if has_res:
        acc = acc + r_ref[...]
    o_ref[...] = acc.astype(o_ref.dtype)


def _matmul(a, w, *, out_dtype, tm, tn, act=None, residual=None, w_rows_are_outputs=False, name="matmul"):
    m, k = a.shape
    n = w.shape[0] if w_rows_are_outputs else w.shape[1]
    tm = _tile(m, tm)
    tn = _tile(n, tn)
    w_spec = (pl.BlockSpec((tn, k), lambda i, j: (j, 0)) if w_rows_are_outputs
              else pl.BlockSpec((k, tn), lambda i, j: (0, j)))
    in_specs = [pl.BlockSpec((tm, k), lambda i, j: (i, 0)), w_spec]
    args = [a, w]
    if residual is not None:
        in_specs.append(pl.BlockSpec((tm, tn), lambda i, j: (i, j)))
        args.append(residual)
    return pl.pallas_call(
        functools.partial(_mm_kernel, act=act, has_res=residual is not None,
                          w_rows_are_outputs=w_rows_are_outputs),
        out_shape=jax.ShapeDtypeStruct((m, n), out_dtype),
        grid=(m // tm, n // tn),
        in_specs=in_specs,
        out_specs=pl.BlockSpec((tm, tn), lambda i, j: (i, j)),
        compiler_params=_cparams(("arbitrary", "arbitrary")),
        name=name,
    )(*args)


def _gdn_kernel(q_ref, k_ref, v_ref, z_ref, sm_ref, cwq_ref, cwk_ref, cwv_ref, gn_ref, o_ref,
                state_ref, tail_ref, ext_ref, qs, ks, vs, *, ts, nc, hb):
    hg = pl.program_id(1)
    si = pl.program_id(2)
    c_sz = GDN_CHUNK

    @pl.when(si == 0)
    def _():
        state_ref[...] = jnp.zeros_like(state_ref)
        tail_ref[...] = jnp.zeros_like(tail_ref)

    def conv_silu(x_ref, w_ref, idx):
        x = x_ref[0]
        ext_ref[0:8, :] = tail_ref[idx]
        ext_ref[8:8 + ts, :] = x
        w = w_ref[...]
        y = (ext_ref[5:5 + ts, :] * w[0:1, :] + ext_ref[6:6 + ts, :] * w[1:2, :]
             + ext_ref[7:7 + ts, :] * w[2:3, :] + x * w[3:4, :])
        tail_ref[idx] = x[ts - 8:ts, :]
        return y * jax.nn.sigmoid(y)

    q = conv_silu(q_ref, cwq_ref, 0)
    k = conv_silu(k_ref, cwk_ref, 1)
    vs[...] = conv_silu(v_ref, cwv_ref, 2)
    for hh in range(hb):
        cs = slice(hh * LANES, (hh + 1) * LANES)
        qh = q[:, cs]
        kh = k[:, cs]
        qs[:, cs] = qh * lax.rsqrt(jnp.sum(qh * qh, axis=-1, keepdims=True) + EPS) * (GDN_DK ** -0.5)
        ks[:, cs] = kh * lax.rsqrt(jnp.sum(kh * kh, axis=-1, keepdims=True) + EPS)

    ri = lax.broadcasted_iota(jnp.int32, (c_sz, c_sz), 0)
    ci = lax.broadcasted_iota(jnp.int32, (c_sz, c_sz), 1)
    strict = ri > ci
    incl = ri >= ci
    eye = jnp.where(ri == ci, 1.0, 0.0)
    n_lvl = c_sz.bit_length() - 1
    lvl_masks = []
    for l in range(n_lvl):
        same = (ri >> (l + 1)) == (ci >> (l + 1))
        lvl_masks.append(jnp.where(
            same, jnp.where(((ri >> l) & 1) == 1, jnp.where(((ci >> l) & 1) == 0, 1.0, 0.0), 0.0), 0.0))

    def body(c, carry):
        rows = pl.ds(pl.multiple_of(c * c_sz, c_sz), c_sz)
        smc = sm_ref[0, rows, :]
        mlows, rhss, lhs2s, qds, cds = [], [], [], [], []
        for hh in range(hb):
            cs = slice(hh * LANES, (hh + 1) * LANES)
            hd = hg * hb + hh
            beta = _lane_col(smc, hd)
            gam = _lane_col(smc, hd + GDN_HEADS)
            kc = ks[rows, cs]
            qc = qs[rows, cs]
            egam = jnp.exp(gam)
            g_last = gam[c_sz - 1:c_sz, :]
            gcol = jnp.broadcast_to(gam, (c_sz, c_sz))
            diff = gcol - gcol.T
            e = jnp.exp(jnp.where(incl, diff, 0.0))
            mlows.append(beta * _dot_nt(kc, kc) * jnp.where(strict, e, 0.0))
            qk = _dot_nt(qc, kc) * jnp.where(incl, e, 0.0)
            kd_t = (kc * jnp.exp(g_last - gam)).T
            lhs2s.append(jnp.concatenate([qk, kd_t], axis=0).astype(BF16))
            rhss.append(jnp.concatenate([kc * (beta * egam), vs[rows, cs] * beta], axis=1).astype(BF16))
            qds.append(qc * egam)
            cds.append(jnp.exp(g_last))
        xs = [eye - lvl_masks[0] * m for m in mlows]
        for l in range(1, n_lvl):
            tl = [_dot(lvl_masks[l] * m, x) for m, x in zip(mlows, xs)]
            xs = [x - _dot(x, t) for x, t in zip(xs, tl)]
        rs = [eye - x - _dot3(m, x) for m, x in zip(mlows, xs)]
        xs = [x + _dot(x, r) for x, r in zip(xs, rs)]
        wus = [_dot2(x, r) for x, r in zip(xs, rhss)]
        for hh in range(hb):
            cs = slice(hh * LANES, (hh + 1) * LANES)
            state = state_ref[hh]
            wu = wus[hh]
            a = _dot(jnp.concatenate([wu[:, :GDN_DK], qds[hh]], axis=0), state)
            u = wu[:, GDN_DK:] - a[:c_sz]
            b2 = jnp.dot(lhs2s[hh], u.astype(BF16), preferred_element_type=F32)
            o = a[c_sz:] + b2[:c_sz]
            state_ref[hh] = state * cds[hh] + b2[c_sz:]
            z = z_ref[0, rows, cs]
            on = o * lax.rsqrt(jnp.mean(o * o, axis=-1, keepdims=True) + EPS) * gn_ref[...]
            o_ref[0, rows, cs] = (on * (z * jax.nn.sigmoid(z))).astype(o_ref.dtype)
        return carry

    lax.fori_loop(0, nc, body, 0)


def _gdn(qkvz, sm, conv_w, gn):
    b, s, _ = qkvz.shape
    nh = GDN_HEADS
    hb = GDN_HEADS_PER_STEP
    ng = nh // hb
    wb = hb * LANES
    ts = _tile(s, 512)
    nc = ts // GDN_CHUNK
    blk = lambda off: pl.BlockSpec((1, ts, wb), lambda i, h, j: (i, j, off + h))
    cw = lambda off: pl.BlockSpec((GDN_CONV, wb), lambda i, h, j: (0, off + h))
    return pl.pallas_call(
        functools.partial(_gdn_kernel, ts=ts, nc=nc, hb=hb),
        out_shape=jax.ShapeDtypeStruct((b, s, nh * GDN_DV), BF16),
        grid=(b, ng, s // ts),
        in_specs=[
            blk(0), blk(ng), blk(2 * ng), blk(3 * ng),
            pl.BlockSpec((1, ts, LANES), lambda i, h, j: (i, j, 0)),
            cw(0), cw(ng), cw(2 * ng),
            pl.BlockSpec((1, GDN_DV), lambda i, h, j: (0, 0)),
        ],
        out_specs=pl.BlockSpec((1, ts, wb), lambda i, h, j: (i, j, h)),
        scratch_shapes=[
            pltpu.VMEM((hb, GDN_DK, GDN_DV), F32),
            pltpu.VMEM((3, 8, wb), F32),
            pltpu.VMEM((ts + 8, wb), F32),
            pltpu.VMEM((ts, wb), F32),
            pltpu.VMEM((ts, wb), F32),
            pltpu.VMEM((ts, wb), F32),
        ],
        compiler_params=_cparams(("arbitrary", "arbitrary", "arbitrary")),
        name="gdn",
    )(qkvz, qkvz, qkvz, qkvz, sm, conv_w, conv_w, conv_w, gn)


def _fox_kernel(q_ref, k_ref, v_ref, smq_ref, smk_ref, gq_ref, gk_ref, o_ref,
                kaug_ref, vaug_ref, m_s, acc_s, sa_ref, sb_ref, *, tq, nk):
    hd = pl.program_id(1)
    qi = pl.program_id(2)
    cidx = hd + 2 * GDN_HEADS
    lane = lax.broadcasted_iota(jnp.int32, (tq, LANES), 1)

    @pl.when(qi == 0)
    def _():
        def build(j, carry):
            rows = pl.ds(pl.multiple_of(j * tq, tq), tq)
            kk = k_ref[0, rows, :].astype(F32)
            kn = kk * lax.rsqrt(jnp.mean(kk * kk, axis=-1, keepdims=True) + EPS) * gk_ref[...]
            hi, mid, lo = _split3(_lane_col(smk_ref[0, rows, :], cidx) * LOG2E)
            aug = jnp.where(lane < 3, 1.0, jnp.where(lane == 3, -hi, jnp.where(lane == 4, -mid, jnp.where(lane == 5, -lo, 0.0))))
            kaug_ref[rows, 0:FOX_DH] = kn.astype(BF16)
            kaug_ref[rows, FOX_DH:2 * FOX_DH] = aug.astype(BF16)
            vaug_ref[rows, 0:FOX_DH] = v_ref[0, rows, :]
            vaug_ref[rows, FOX_DH:2 * FOX_DH] = jnp.ones((tq, FOX_DH), BF16)
            return carry
        lax.fori_loop(0, nk, build, 0)

    qq = q_ref[0].astype(F32)
    qn = qq * lax.rsqrt(jnp.mean(qq * qq, axis=-1, keepdims=True) + EPS) * gq_ref[...] * (FOX_DH ** -0.5 * LOG2E)
    hi, mid, lo = _split3(_lane_col(smq_ref[0], cidx) * LOG2E)
    aug = jnp.where(lane == 0, hi, jnp.where(lane == 1, mid, jnp.where(lane == 2, lo, jnp.where(lane < 6, 1.0, 0.0))))
    q_aug = jnp.concatenate([qn.astype(BF16), aug.astype(BF16)], axis=1)

    m_s[...] = jnp.full_like(m_s, NEG_BIG)
    acc_s[...] = jnp.zeros_like(acc_s)

    tr = tq // FOX_SUBTILES
    n_lt = tq // LANES

    def scores(s_ref, j):
        rows = pl.ds(pl.multiple_of(j * tq, tq), tq)
        s_ref[...] = lax.dot_general(q_aug, kaug_ref[rows, :], (((1,), (1,)), ((), ())),
                                     preferred_element_type=F32)

    def accumulate(s_ref, j, masked):
        rows = pl.ds(pl.multiple_of(j * tq, tq), tq)
        v_blk = vaug_ref[rows, :]
        for r in range(FOX_SUBTILES):
            rs = slice(r * tr, (r + 1) * tr)
            tiles = [s_ref[rs, c * LANES:(c + 1) * LANES] for c in range(n_lt)]
            if masked:
                ri = lax.broadcasted_iota(jnp.int32, (tr, LANES), 0) + r * tr
                ci = lax.broadcasted_iota(jnp.int32, (tr, LANES), 1)
                tiles = [jnp.where(ri >= ci + c * LANES, t, NEG_BIG) for c, t in enumerate(tiles)]
            mx = tiles[0]
            for t in tiles[1:]:
                mx = jnp.maximum(mx, t)
            m_prev = m_s[rs, :]
            m_next = jnp.maximum(m_prev, jnp.max(mx, axis=1, keepdims=True))
            p = jnp.concatenate([jnp.exp2(t - m_next) for t in tiles], axis=1).astype(BF16)
            alpha = jnp.exp2(m_prev - m_next)
            acc_s[rs, :] = (jnp.concatenate([alpha, alpha], axis=1) * acc_s[rs, :]
                            + jnp.dot(p, v_blk, preferred_element_type=F32))
            m_s[rs, :] = m_next

    scores(sa_ref, 0)

    def body(i, carry):
        j = 2 * i
        scores(sb_ref, j + 1)
        accumulate(sa_ref, j, False)
        scores(sa_ref, j + 2)
        accumulate(sb_ref, j + 1, False)
        return carry

    lax.fori_loop(0, qi // 2, body, 0)

    @pl.when(qi % 2 == 0)
    def _():
        accumulate(sa_ref, qi, True)

    @pl.when(qi % 2 == 1)
    def _():
        scores(sb_ref, qi)
        accumulate(sa_ref, qi - 1, False)
        accumulate(sb_ref, qi, True)

    acc = acc_s[...]
    o_ref[0] = (acc[:, :FOX_DH] / acc[:, FOX_DH:]).astype(o_ref.dtype)


def _fox(qkv, sm, gq, gk):
    b, s, _ = qkv.shape
    nh = FOX_HEADS
    tq = _tile(s, 512)
    nk = s // tq
    return pl.pallas_call(
        functools.partial(_fox_kernel, tq=tq, nk=nk),
        out_shape=jax.ShapeDtypeStruct((b, s, nh * FOX_DH), BF16),
        grid=(b, nh, s // tq),
        in_specs=[
            pl.BlockSpec((1, tq, FOX_DH), lambda i, h, j: (i, j, h)),
            pl.BlockSpec((1, s, FOX_DH), lambda i, h, j: (i, 0, nh + h)),
            pl.BlockSpec((1, s, FOX_DH), lambda i, h, j: (i, 0, 2 * nh + h)),
            pl.BlockSpec((1, tq, LANES), lambda i, h, j: (i, j, 0)),
            pl.BlockSpec((1, s, LANES), lambda i, h, j: (i, 0, 0)),
            pl.BlockSpec((1, FOX_DH), lambda i, h, j: (0, 0)),
            pl.BlockSpec((1, FOX_DH), lambda i, h, j: (0, 0)),
        ],
        out_specs=pl.BlockSpec((1, tq, FOX_DH), lambda i, h, j: (i, j, h)),
        scratch_shapes=[
            pltpu.VMEM((s, 2 * FOX_DH), BF16),
            pltpu.VMEM((s, 2 * FOX_DH), BF16),
            pltpu.VMEM((tq, LANES), F32),
            pltpu.VMEM((tq, 2 * FOX_DH), F32),
            pltpu.VMEM((tq, tq), F32),
            pltpu.VMEM((tq, tq), F32),
        ],
        compiler_params=_cparams(("arbitrary", "arbitrary", "arbitrary")),
        name="fox",
    )(qkv, qkv, qkv, sm, sm, gq, gk)


def _memkv_kernel(mem_ref, g_ref, w_ref, gk_ref, o_ref, *, n_k_tiles):
    j = pl.program_id(1)
    m = mem_ref[0]
    hn = (m * lax.rsqrt(jnp.mean(m * m, axis=-1, keepdims=True) + EPS) * g_ref[...]).astype(BF16)
    r = jnp.dot(hn, w_ref[...], preferred_element_type=F32)
    parts = []
    for t in range(r.shape[1] // MEM_DH):
        rt = r[:, t * MEM_DH:(t + 1) * MEM_DH]
        parts.append(rt * lax.rsqrt(jnp.mean(rt * rt, axis=-1, keepdims=True) + EPS) * gk_ref[...])
    normed = jnp.concatenate(parts, axis=1)
    is_k = jnp.where(j < n_k_tiles, 1.0, 0.0)
    o_ref[0] = (is_k * normed + (1.0 - is_k) * r).astype(o_ref.dtype)


def _memkv(mem, g, w, gk):
    b, ml, d = mem.shape
    n = w.shape[1]
    tn = 2 * MEM_DH
    return pl.pallas_call(
        functools.partial(_memkv_kernel, n_k_tiles=(n // 2) // tn),
        out_shape=jax.ShapeDtypeStruct((b, ml, n), BF16),
        grid=(b, n // tn),
        in_specs=[
            pl.BlockSpec((1, ml, d), lambda i, j: (i, 0, 0)),
            pl.BlockSpec((1, d), lambda i, j: (0, 0)),
            pl.BlockSpec((d, tn), lambda i, j: (0, j)),
            pl.BlockSpec((1, MEM_DH), lambda i, j: (0, 0)),
        ],
        out_specs=pl.BlockSpec((1, ml, tn), lambda i, j: (i, 0, j)),
        compiler_params=_cparams(("arbitrary", "arbitrary")),
        name="memkv",
    )(mem, g, w, gk)


def _mem_kernel(q_ref, k_ref, v_ref, gq_ref, o_ref):
    for hd in range(MEM_HEADS):
        cs = slice(hd * MEM_DH, (hd + 1) * MEM_DH)
        qq = q_ref[0, :, cs].astype(F32)
        qn = qq * lax.rsqrt(jnp.mean(qq * qq, axis=-1, keepdims=True) + EPS) * gq_ref[...] * (MEM_DH ** -0.5)
        s = _dot_nt(qn, k_ref[0, :, cs])
        p = jnp.exp(s - jnp.max(s, axis=1, keepdims=True))
        p = p / jnp.sum(p, axis=1, keepdims=True)
        o_ref[0, :, cs] = jnp.dot(p.astype(BF16), v_ref[0, :, cs], preferred_element_type=F32).astype(o_ref.dtype)


def _mem_attn(qsrc, q_col_block, kv, gq):
    b, s, _ = qsrc.shape
    ml = kv.shape[1]
    wq = MEM_HEADS * MEM_DH
    tq = _tile(s, 512)
    return pl.pallas_call(
        _mem_kernel,
        out_shape=jax.ShapeDtypeStruct((b, s, wq), BF16),
        grid=(b, s // tq),
        in_specs=[
            pl.BlockSpec((1, tq, wq), lambda i, j: (i, j, q_col_block)),
            pl.BlockSpec((1, ml, wq), lambda i, j: (i, 0, 0)),
            pl.BlockSpec((1, ml, wq), lambda i, j: (i, 0, 1)),
            pl.BlockSpec((1, MEM_DH), lambda i, j: (0, 0)),
        ],
        out_specs=pl.BlockSpec((1, tq, wq), lambda i, j: (i, j, 0)),
        compiler_params=_cparams(("arbitrary", "arbitrary")),
        name="mem_attn",
    )(qsrc, kv, kv, gq)


def _merge_kernel(oa_ref, ob_ref, om_ref, wa_ref, wb_ref, wm_ref, ga_ref, gb_ref, gm_ref, y_ref):
    d = functools.partial(jnp.dot, preferred_element_type=F32)
    y = (ga_ref[...].astype(F32) * d(oa_ref[...], wa_ref[...])
         + gb_ref[...].astype(F32) * d(ob_ref[...], wb_ref[...])
         + gm_ref[...].astype(F32) * d(om_ref[...], wm_ref[...]))
    y_ref[...] = y.astype(y_ref.dtype)


def _merge(oa, ob, om, wa, wb, wm, gates):
    t, ka = oa.shape
    dm = wa.shape[1]
    tm = _tile(t, 1024)
    tn = _tile(dm, 512)
    nb = dm // tn
    a_spec = lambda kk: pl.BlockSpec((tm, kk), lambda i, j: (i, 0))
    w_spec = lambda kk: pl.BlockSpec((kk, tn), lambda i, j: (0, j))
    g_spec = lambda off: pl.BlockSpec((tm, tn), lambda i, j: (i, off * nb + j))
    return pl.pallas_call(
        _merge_kernel,
        out_shape=jax.ShapeDtypeStruct((t, dm), BF16),
        grid=(t // tm, nb),
        in_specs=[a_spec(ka), a_spec(ob.shape[1]), a_spec(om.shape[1]),
                  w_spec(ka), w_spec(ob.shape[1]), w_spec(om.shape[1]),
                  g_spec(0), g_spec(1), g_spec(2)],
        out_specs=pl.BlockSpec((tm, tn), lambda i, j: (i, j)),
        compiler_params=_cparams(("arbitrary", "arbitrary")),
        name="merge",
    )(oa, ob, om, wa, wb, wm, gates, gates, gates)


def _layer(x, mem, g_mix, w_in, conv_w, a_log, dt_bias, gdn_norm_g, fox_b_f, fox_q_norm, fox_k_norm,
           g_mem, w_mem_kv, mem_q_norm, mem_k_norm, w_up_gdn, w_up_fox, w_up_mem, w_out, g_mlp, w_ff1, w_ff2):
    b, s, d = x.shape
    t = b * s
    gdn_qk = GDN_HEADS * GDN_DK
    gdn_v = GDN_HEADS * GDN_DV
    fox_w = FOX_HEADS * FOX_DH
    mem_w = MEM_HEADS * MEM_DH
    o_z = 2 * gdn_qk + gdn_v
    o_beta = o_z + gdn_v
    o_dec = o_beta + GDN_HEADS
    o_fq = o_dec + GDN_HEADS
    o_ff = o_fq + 3 * fox_w
    o_mq = o_ff + FOX_HEADS
    o_gate = o_mq + mem_w

    wt = w_in.T
    w_gdn = wt[:o_beta].astype(BF16)
    w_att = jnp.concatenate([wt[o_fq:o_ff], wt[o_mq:o_gate]], axis=0).astype(BF16)
    w_gate = wt[o_gate:].astype(BF16)
    n_small = 2 * GDN_HEADS + FOX_HEADS
    w_small = jnp.concatenate([wt[o_beta:o_fq], wt[o_ff:o_mq],
                               jnp.zeros((LANES - n_small, d), F32)], axis=0)
    zpad = jnp.zeros((LANES - n_small,), F32)
    bias = jnp.concatenate([jnp.zeros((GDN_HEADS,), F32), dt_bias.astype(F32), fox_b_f.astype(F32), zpad])[None, :]
    alog = jnp.concatenate([jnp.zeros((GDN_HEADS,), F32), a_log.astype(F32), jnp.zeros((FOX_HEADS,), F32), zpad])[None, :]

    h, sm = _norm_small(x, g_mix[None, :], w_small, bias, alog)
    h2d = h.reshape(t, d)

    qkvz = _matmul(h2d, w_gdn, out_dtype=F32, tm=2048, tn=512, w_rows_are_outputs=True,
                   name="proj_gdn").reshape(b, s, -1)
    att = _matmul(h2d, w_att, out_dtype=BF16, tm=2048, tn=512, w_rows_are_outputs=True,
                  name="proj_att").reshape(b, s, -1)
    gates = _matmul(h2d, w_gate, out_dtype=BF16, tm=2048, tn=512, act="sigmoid", w_rows_are_outputs=True,
                    name="proj_gate")

    o_a = _gdn(qkvz, sm, conv_w, gdn_norm_g[None, :])
    o_b = _fox(att, sm, fox_q_norm[None, :], fox_k_norm[None, :])
    kv_m = _memkv(mem, g_mem[None, :], w_mem_kv.astype(BF16), mem_k_norm[None, :])
    o_m = _mem_attn(att, (3 * fox_w) // mem_w, kv_m, mem_q_norm[None, :])

    y = _merge(o_a.reshape(t, -1), o_b.reshape(t, -1), o_m.reshape(t, -1),
               w_up_gdn.astype(BF16), w_up_fox.astype(BF16), w_up_mem.astype(BF16), gates)
    x1 = _matmul(y, w_out.astype(BF16), out_dtype=F32, tm=2048, tn=512, residual=x.reshape(t, d), name="out_proj")

    h2 = _rmsnorm(x1, g_mlp[None, :])
    u = _matmul(h2, w_ff1.astype(BF16), out_dtype=BF16, tm=2048, tn=512, act="relu2", name="ff1")
    out = _matmul(u, w_ff2.astype(BF16), out_dtype=F32, tm=512, tn=512, residual=x1, name="ff2")
    return out.reshape(b, s, d)


def kernel(x, mem, g_mix, w_in, conv_w, a_log, dt_bias, gdn_norm_g, fox_b_f, fox_q_norm, fox_k_norm, g_mem, w_mem_kv, mem_q_norm, mem_k_norm, w_up_gdn, w_up_fox, w_up_mem, w_out, g_mlp, w_ff1, w_ff2):
    depth = w_in.shape[0]
    for l in range(depth):
        x = _layer(x, mem, g_mix[l], w_in[l], conv_w[l], a_log[l], dt_bias[l], gdn_norm_g[l], fox_b_f[l],
                   fox_q_norm[l], fox_k_norm[l], g_mem[l], w_mem_kv[l], mem_q_norm[l], mem_k_norm[l],
                   w_up_gdn[l], w_up_fox[l], w_up_mem[l], w_out[l], g_mlp[l], w_ff1[l], w_ff2[l])
    return x
```

```python
import functools
import math

import jax
import jax.numpy as jnp
from jax import lax
from jax.experimental import pallas as pl
from jax.experimental.pallas import tpu as pltpu

F32 = jnp.float32
BF16 = jnp.bfloat16
EPS = 1e-6

GDN_HEADS = 8
GDN_DK = 128
GDN_DV = 128
GDN_CONV = 4
GDN_CHUNK = 128
GDN_HEADS_PER_STEP = 8
FOX_HEADS = 8
FOX_DH = 128
FOX_TQ = 1024
FOX_TK = 512
MEM_HEADS = 4
MEM_DH = 256
N_BRANCH = 3
LANES = 128
NEG_BIG = -1e30
LOG2E = math.log2(math.e)

VMEM_LIMIT = 56 * 1024 * 1024


def _tile(n, pref):
    return pref if n % pref == 0 else n


def _cparams(sem):
    return pltpu.CompilerParams(dimension_semantics=sem, vmem_limit_bytes=VMEM_LIMIT)


def _dot(a, b):
    return jnp.dot(a.astype(BF16), b.astype(BF16), preferred_element_type=F32)


def _dot_nt(a, b):
    return lax.dot_general(a.astype(BF16), b.astype(BF16), (((1,), (1,)), ((), ())),
                           preferred_element_type=F32)


def _split2(a):
    hi = a.astype(BF16)
    lo = (a - hi.astype(F32)).astype(BF16)
    return hi, lo


def _split3(a):
    hi = a.astype(BF16).astype(F32)
    r = a - hi
    mid = r.astype(BF16).astype(F32)
    lo = (r - mid).astype(BF16).astype(F32)
    return hi, mid, lo


def _dot3(a, b):
    a_hi, a_lo = _split2(a)
    b_hi, b_lo = _split2(b)
    d = functools.partial(jnp.dot, preferred_element_type=F32)
    return d(a_hi, b_hi) + d(a_hi, b_lo) + d(a_lo, b_hi)


def _dot3_nt(a, b):
    a_hi, a_lo = _split2(a)
    b_hi, b_lo = _split2(b)
    d = lambda p, q: lax.dot_general(p, q, (((1,), (1,)), ((), ())), preferred_element_type=F32)
    return d(a_hi, b_hi) + d(a_hi, b_lo) + d(a_lo, b_hi)


def _dot2(a, b_bf16):
    a_hi, a_lo = _split2(a)
    d = functools.partial(jnp.dot, preferred_element_type=F32)
    return d(a_hi, b_bf16) + d(a_lo, b_bf16)


def _dot_exact_lhs(l_bf16, v):
    hi, mid, lo = _split3(v)
    d = functools.partial(jnp.dot, preferred_element_type=F32)
    return d(l_bf16, hi.astype(BF16)) + d(l_bf16, mid.astype(BF16)) + d(l_bf16, lo.astype(BF16))


def _lane_col(a, idx):
    lane = lax.broadcasted_iota(jnp.int32, a.shape, 1)
    return jnp.sum(jnp.where(lane == idx, a, 0.0), axis=1, keepdims=True)


def _softplus(x):
    return jnp.maximum(x, 0.0) + jnp.log1p(jnp.exp(-jnp.abs(x)))


def _norm_small_kernel(x_ref, g_ref, ws_ref, bias_ref, alog_ref, h_ref, sm_ref, carry_ref, *, tm):
    s = pl.program_id(1)

    @pl.when(s == 0)
    def _():
        carry_ref[...] = jnp.zeros_like(carry_ref)

    x = x_ref[0]
    h = x * lax.rsqrt(jnp.mean(x * x, axis=-1, keepdims=True) + EPS) * g_ref[...]
    h_ref[0] = h.astype(BF16)

    pre = _dot3_nt(h, ws_ref[...]) + bias_ref[...]
    lane = lax.broadcasted_iota(jnp.int32, pre.shape, 1)
    nh = GDN_HEADS
    beta = jax.nn.sigmoid(pre)
    gdec = -jnp.exp(alog_ref[...]) * _softplus(pre)
    logf = -_softplus(-pre)
    vals = jnp.where(lane < nh, beta, jnp.where(lane < 2 * nh, gdec, jnp.where(lane < 3 * nh, logf, 0.0)))

    row = lax.broadcasted_iota(jnp.int32, (tm, tm), 0)
    col = lax.broadcasted_iota(jnp.int32, (tm, tm), 1)
    low = col <= row
    l_full = jnp.where(low, 1.0, 0.0).astype(BF16)
    sh = GDN_CHUNK.bit_length() - 1
    same_chunk = (row >> sh) == (col >> sh)
    l_blk = jnp.where(low, jnp.where(same_chunk, 1.0, 0.0), 0.0).astype(BF16)
    cs_blk = _dot_exact_lhs(l_blk, vals)
    cs_full = _dot_exact_lhs(l_full, vals) + carry_ref[0:1, :]
    sm_ref[0] = jnp.where(lane < nh, vals, jnp.where(lane < 2 * nh, cs_blk, jnp.where(lane < 3 * nh, cs_full, 0.0)))
    carry_ref[...] = jnp.broadcast_to(cs_full[tm - 1:tm, :], carry_ref.shape)


def _norm_small(x, g, w_small, bias, alog):
    b, s, d = x.shape
    tm = _tile(s, 512)
    return pl.pallas_call(
        functools.partial(_norm_small_kernel, tm=tm),
        out_shape=(jax.ShapeDtypeStruct((b, s, d), BF16), jax.ShapeDtypeStruct((b, s, LANES), F32)),
        grid=(b, s // tm),
        in_specs=[
            pl.BlockSpec((1, tm, d), lambda i, j: (i, j, 0)),
            pl.BlockSpec((1, d), lambda i, j: (0, 0)),
            pl.BlockSpec((LANES, d), lambda i, j: (0, 0)),
            pl.BlockSpec((1, LANES), lambda i, j: (0, 0)),
            pl.BlockSpec((1, LANES), lambda i, j: (0, 0)),
        ],
        out_specs=(
            pl.BlockSpec((1, tm, d), lambda i, j: (i, j, 0)),
            pl.BlockSpec((1, tm, LANES), lambda i, j: (i, j, 0)),
        ),
        scratch_shapes=[pltpu.VMEM((8, LANES), F32)],
        compiler_params=_cparams(("arbitrary", "arbitrary")),
        name="norm_small",
    )(x, g, w_small, bias, alog)


def _rmsnorm_kernel(x_ref, g_ref, o_ref):
    x = x_ref[...]
    o_ref[...] = (x * lax.rsqrt(jnp.mean(x * x, axis=-1, keepdims=True) + EPS) * g_ref[...]).astype(o_ref.dtype)


def _rmsnorm(x2d, g):
    t, d = x2d.shape
    tm = _tile(t, 512)
    return pl.pallas_call(
        _rmsnorm_kernel,
        out_shape=jax.ShapeDtypeStruct((t, d), BF16),
        grid=(t // tm,),
        in_specs=[pl.BlockSpec((tm, d), lambda i: (i, 0)), pl.BlockSpec((1, d), lambda i: (0, 0))],
        out_specs=pl.BlockSpec((tm, d), lambda i: (i, 0)),
        compiler_params=_cparams(("arbitrary",)),
        name="rmsnorm",
    )(x2d, g)


def _mm_kernel(a_ref, w_ref, *rest, act, has_res, w_rows_are_outputs):
    if has_res:
        r_ref, o_ref = rest
    else:
        (o_ref,) = rest
    contract = (((1,), (1,)), ((), ())) if w_rows_are_outputs else (((1,), (0,)), ((), ()))
    acc = lax.dot_general(a_ref[...], w_ref[...].astype(BF16), contract, preferred_element_type=F32)
    if act == "sigmoid":
        acc = jax.nn.sigmoid(acc)
    elif act == "relu2":
        r = jnp.maximum(acc, 0.0)
        acc = r * r
    if has_res:
        acc = acc + r_ref[...]
    o_ref[...] = acc.astype(o_ref.dtype)


def _matmul(a, w, *, out_dtype, tm, tn, act=None, residual=None, w_rows_are_outputs=False,
            w_row_ranges=None, name="matmul"):
    m, k = a.shape
    if w_row_ranges is not None:
        assert w_rows_are_outputs
        n = sum(r for _, r in w_row_ranges)
    else:
        n = w.shape[0] if w_rows_are_outputs else w.shape[1]
    tm = _tile(m, tm)
    tn = _tile(n, tn)
    if w_row_ranges is not None:
        assert all(r % tn == 0 and f % 8 == 0 for f, r in w_row_ranges)

        def w_rows(i, j):
            start, first_blk = jnp.int32(0), 0
            for f, r in w_row_ranges:
                start = jnp.where(j >= first_blk, f + (j - first_blk) * tn, start)
                first_blk += r // tn
            return pl.multiple_of(start, 8), 0

        w_spec = pl.BlockSpec((pl.Element(tn), pl.Element(k)), w_rows)
    elif w_rows_are_outputs:
        w_spec = pl.BlockSpec((tn, k), lambda i, j: (j, 0))
    else:
        w_spec = pl.BlockSpec((k, tn), lambda i, j: (0, j))
    in_specs = [pl.BlockSpec((tm, k), lambda i, j: (i, 0)), w_spec]
    args = [a, w]
    if residual is not None:
        in_specs.append(pl.BlockSpec((tm, tn), lambda i, j: (i, j)))
        args.append(residual)
    return pl.pallas_call(
        functools.partial(_mm_kernel, act=act, has_res=residual is not None,
                          w_rows_are_outputs=w_rows_are_outputs),
        out_shape=jax.ShapeDtypeStruct((m, n), out_dtype),
        grid=(m // tm, n // tn),
        in_specs=in_specs,
        out_specs=pl.BlockSpec((tm, tn), lambda i, j: (i, j)),
        compiler_params=_cparams(("arbitrary", "arbitrary")),
        name=name,
    )(*args)


def _gdn_kernel(q_ref, k_ref, v_ref, z_ref, sm_ref, cwq_ref, cwk_ref, cwv_ref, gn_ref, o_ref,
                state_ref, tail_ref, ext_ref, qs, ks, vs, *, ts, nc, hb):
    hg = pl.program_id(1)
    si = pl.program_id(2)
    c_sz = GDN_CHUNK

    @pl.when(si == 0)
    def _():
        state_ref[...] = jnp.zeros_like(state_ref)
        tail_ref[...] = jnp.zeros_like(tail_ref)

    def conv_silu(x_ref, w_ref, idx):
        x = x_ref[0]
        ext_ref[0:8, :] = tail_ref[idx]
        ext_ref[8:8 + ts, :] = x
        w = w_ref[...]
        y = (ext_ref[5:5 + ts, :] * w[0:1, :] + ext_ref[6:6 + ts, :] * w[1:2, :]
             + ext_ref[7:7 + ts, :] * w[2:3, :] + x * w[3:4, :])
        tail_ref[idx] = x[ts - 8:ts, :]
        return y * jax.nn.sigmoid(y)

    q = conv_silu(q_ref, cwq_ref, 0)
    k = conv_silu(k_ref, cwk_ref, 1)
    vs[...] = conv_silu(v_ref, cwv_ref, 2)
    for hh in range(hb):
        cs = slice(hh * LANES, (hh + 1) * LANES)
        qh = q[:, cs]
        kh = k[:, cs]
        qs[:, cs] = qh * lax.rsqrt(jnp.sum(qh * qh, axis=-1, keepdims=True) + EPS) * (GDN_DK ** -0.5)
        ks[:, cs] = kh * lax.rsqrt(jnp.sum(kh * kh, axis=-1, keepdims=True) + EPS)

    ri = lax.broadcasted_iota(jnp.int32, (c_sz, c_sz), 0)
    ci = lax.broadcasted_iota(jnp.int32, (c_sz, c_sz), 1)
    strict = ri > ci
    incl = ri >= ci
    eye = jnp.where(ri == ci, 1.0, 0.0)
    n_lvl = c_sz.bit_length() - 1
    lvl_masks = []
    for l in range(n_lvl):
        same = (ri >> (l + 1)) == (ci >> (l + 1))
        lvl_masks.append(jnp.where(
            same, jnp.where(((ri >> l) & 1) == 1, jnp.where(((ci >> l) & 1) == 0, 1.0, 0.0), 0.0), 0.0))

    def body(c, carry):
        rows = pl.ds(pl.multiple_of(c * c_sz, c_sz), c_sz)
        smc = sm_ref[0, rows, :]
        mlows, rhss, lhs2s, qds, cds = [], [], [], [], []
        for hh in range(hb):
            cs = slice(hh * LANES, (hh + 1) * LANES)
            hd = hg * hb + hh
            beta = _lane_col(smc, hd)
            gam = _lane_col(smc, hd + GDN_HEADS)
            kc = ks[rows, cs]
            qc = qs[rows, cs]
            egam = jnp.exp(gam)
            g_last = gam[c_sz - 1:c_sz, :]
            gcol = jnp.broadcast_to(gam, (c_sz, c_sz))
            diff = gcol - gcol.T
            e = jnp.exp(jnp.where(incl, diff, 0.0))
            mlows.append(beta * _dot_nt(kc, kc) * jnp.where(strict, e, 0.0))
            qk = _dot_nt(qc, kc) * jnp.where(incl, e, 0.0)
            kd_t = (kc * jnp.exp(g_last - gam)).T
            lhs2s.append(jnp.concatenate([qk, kd_t], axis=0).astype(BF16))
            rhss.append(jnp.concatenate([kc * (beta * egam), vs[rows, cs] * beta], axis=1).astype(BF16))
            qds.append(qc * egam)
            cds.append(jnp.exp(g_last))
        xs = [eye - lvl_masks[0] * m for m in mlows]
        for l in range(1, n_lvl):
            tl = [_dot(lvl_masks[l] * m, x) for m, x in zip(mlows, xs)]
            xs = [x - _dot(x, t) for x, t in zip(xs, tl)]
        rs = [eye - x - _dot3(m, x) for m, x in zip(mlows, xs)]
        xs = [x + _dot(x, r) for x, r in zip(xs, rs)]
        wus = [_dot2(x, r) for x, r in zip(xs, rhss)]
        for hh in range(hb):
            cs = slice(hh * LANES, (hh + 1) * LANES)
            state = state_ref[hh]
            wu = wus[hh]
            a = _dot(jnp.concatenate([wu[:, :GDN_DK], qds[hh]], axis=0), state)
            u = wu[:, GDN_DK:] - a[:c_sz]
            b2 = jnp.dot(lhs2s[hh], u.astype(BF16), preferred_element_type=F32)
            o = a[c_sz:] + b2[:c_sz]
            state_ref[hh] = state * cds[hh] + b2[c_sz:]
            z = z_ref[0, rows, cs]
            on = o * lax.rsqrt(jnp.mean(o * o, axis=-1, keepdims=True) + EPS) * gn_ref[...]
            o_ref[0, rows, cs] = (on * (z * jax.nn.sigmoid(z))).astype(o_ref.dtype)
        return carry

    lax.fori_loop(0, nc, body, 0)


def _gdn(qkvz, sm, conv_w, gn):
    b, s, _ = qkvz.shape
    nh = GDN_HEADS
    hb = GDN_HEADS_PER_STEP
    ng = nh // hb
    wb = hb * LANES
    ts = _tile(s, 512)
    nc = ts // GDN_CHUNK
    blk = lambda off: pl.BlockSpec((1, ts, wb), lambda i, h, j: (i, j, off + h))
    cw = lambda off: pl.BlockSpec((GDN_CONV, wb), lambda i, h, j: (0, off + h))
    return pl.pallas_call(
        functools.partial(_gdn_kernel, ts=ts, nc=nc, hb=hb),
        out_shape=jax.ShapeDtypeStruct((b, s, nh * GDN_DV), BF16),
        grid=(b, ng, s // ts),
        in_specs=[
            blk(0), blk(ng), blk(2 * ng), blk(3 * ng),
            pl.BlockSpec((1, ts, LANES), lambda i, h, j: (i, j, 0)),
            cw(0), cw(ng), cw(2 * ng),
            pl.BlockSpec((1, GDN_DV), lambda i, h, j: (0, 0)),
        ],
        out_specs=pl.BlockSpec((1, ts, wb), lambda i, h, j: (i, j, h)),
        scratch_shapes=[
            pltpu.VMEM((hb, GDN_DK, GDN_DV), F32),
            pltpu.VMEM((3, 8, wb), F32),
            pltpu.VMEM((ts + 8, wb), F32),
            pltpu.VMEM((ts, wb), F32),
            pltpu.VMEM((ts, wb), F32),
            pltpu.VMEM((ts, wb), F32),
        ],
        compiler_params=_cparams(("arbitrary", "arbitrary", "arbitrary")),
        name="gdn",
    )(qkvz, qkvz, qkvz, qkvz, sm, conv_w, conv_w, conv_w, gn)


def _fox_kernel(q_ref, k_ref, v_ref, smq_ref, smk_ref, gq_ref, gk_ref, o_ref,
                kaug_ref, vaug_ref, m_s, acc_s, sa_ref, sb_ref, *, tq, tk, nk):
    hd = pl.program_id(1)
    qi = pl.program_id(2)
    cidx = hd + 2 * GDN_HEADS
    n_sub = tq // tk
    n_lt = tk // LANES

    @pl.when(qi == 0)
    def _():
        lane = lax.broadcasted_iota(jnp.int32, (tk, LANES), 1)

        def build(j, carry):
            rows = pl.ds(pl.multiple_of(j * tk, tk), tk)
            kk = k_ref[0, rows, :].astype(F32)
            kn = kk * lax.rsqrt(jnp.mean(kk * kk, axis=-1, keepdims=True) + EPS) * gk_ref[...]
            hi, mid, lo = _split3(_lane_col(smk_ref[0, rows, :], cidx) * LOG2E)
            aug = jnp.where(lane < 3, 1.0, jnp.where(lane == 3, -hi, jnp.where(lane == 4, -mid, jnp.where(lane == 5, -lo, 0.0))))
            kaug_ref[rows, 0:FOX_DH] = kn.astype(BF16)
            kaug_ref[rows, FOX_DH:2 * FOX_DH] = aug.astype(BF16)
            vaug_ref[rows, 0:FOX_DH] = v_ref[0, rows, :]
            vaug_ref[rows, FOX_DH:2 * FOX_DH] = jnp.ones((tk, FOX_DH), BF16)
            return carry
        lax.fori_loop(0, nk, build, 0)

    lane = lax.broadcasted_iota(jnp.int32, (tq, LANES), 1)
    qq = q_ref[0].astype(F32)
    qn = qq * lax.rsqrt(jnp.mean(qq * qq, axis=-1, keepdims=True) + EPS) * gq_ref[...] * (FOX_DH ** -0.5 * LOG2E)
    hi, mid, lo = _split3(_lane_col(smq_ref[0], cidx) * LOG2E)
    aug = jnp.where(lane == 0, hi, jnp.where(lane == 1, mid, jnp.where(lane == 2, lo, jnp.where(lane < 6, 1.0, 0.0))))
    q_aug = jnp.concatenate([qn.astype(BF16), aug.astype(BF16)], axis=1)

    m_s[...] = jnp.full_like(m_s, NEG_BIG)
    acc_s[...] = jnp.zeros_like(acc_s)

    def scores(s_ref, j):
        rows = pl.ds(pl.multiple_of(j * tk, tk), tk)
        s_ref[...] = lax.dot_general(q_aug, kaug_ref[rows, :], (((1,), (1,)), ((), ())),
                                     preferred_element_type=F32)

    def accumulate(s_ref, j, diag):
        rows = pl.ds(pl.multiple_of(j * tk, tk), tk)
        v_blk = vaug_ref[rows, :]
        for r in range(n_sub):
            if diag is not None and r < diag:
                continue
            rs = slice(r * tk, (r + 1) * tk)
            tiles = [s_ref[rs, c * LANES:(c + 1) * LANES] for c in range(n_lt)]
            if diag is not None and r == diag:
                ri = lax.broadcasted_iota(jnp.int32, (tk, LANES), 0)
                ci = lax.broadcasted_iota(jnp.int32, (tk, LANES), 1)
                tiles = [jnp.where(ri >= ci + c * LANES, t, NEG_BIG) for c, t in enumerate(tiles)]
            mx = tiles[0]
            for t in tiles[1:]:
                mx = jnp.maximum(mx, t)
            m_prev = m_s[rs, :]
            m_next = jnp.maximum(m_prev, jnp.max(mx, axis=1, keepdims=True))
            p = jnp.concatenate([jnp.exp2(t - m_next) for t in tiles], axis=1).astype(BF16)
            alpha = jnp.exp2(m_prev - m_next)
            acc_s[rs, :] = (jnp.concatenate([alpha, alpha], axis=1) * acc_s[rs, :]
                            + jnp.dot(p, v_blk, preferred_element_type=F32))
            m_s[rs, :] = m_next

    scores(sa_ref, 0)

    def body(i, carry):
        j = 2 * i
        scores(sb_ref, j + 1)
        accumulate(sa_ref, j, None)
        scores(sa_ref, j + 2)
        accumulate(sb_ref, j + 1, None)
        return carry

    n_below = n_sub * qi
    lax.fori_loop(0, n_below // 2, body, 0)
    bufs = (sa_ref, sb_ref)
    for e in range(n_sub):
        if e + 1 < n_sub:
            scores(bufs[(e + 1) % 2], n_below + e + 1)
        accumulate(bufs[e % 2], n_below + e, e)

    acc = acc_s[...]
    o_ref[0] = (acc[:, :FOX_DH] / acc[:, FOX_DH:]).astype(o_ref.dtype)


def _fox(qkv, sm, gq, gk):
    b, s, _ = qkv.shape
    nh = FOX_HEADS
    tq = _tile(s, FOX_TQ)
    tk = _tile(tq, FOX_TK)
    assert (tq // tk) % 2 == 0, "the score pipeline consumes key blocks in pairs"
    nk = s // tk
    return pl.pallas_call(
        functools.partial(_fox_kernel, tq=tq, tk=tk, nk=nk),
        out_shape=jax.ShapeDtypeStruct((b, s, nh * FOX_DH), BF16),
        grid=(b, nh, s // tq),
        in_specs=[
            pl.BlockSpec((1, tq, FOX_DH), lambda i, h, j: (i, j, h)),
            pl.BlockSpec((1, s, FOX_DH), lambda i, h, j: (i, 0, nh + h)),
            pl.BlockSpec((1, s, FOX_DH), lambda i, h, j: (i, 0, 2 * nh + h)),
            pl.BlockSpec((1, tq, LANES), lambda i, h, j: (i, j, 0)),
            pl.BlockSpec((1, s, LANES), lambda i, h, j: (i, 0, 0)),
            pl.BlockSpec((1, FOX_DH), lambda i, h, j: (0, 0)),
            pl.BlockSpec((1, FOX_DH), lambda i, h, j: (0, 0)),
        ],
        out_specs=pl.BlockSpec((1, tq, FOX_DH), lambda i, h, j: (i, j, h)),
        scratch_shapes=[
            pltpu.VMEM((s, 2 * FOX_DH), BF16),
            pltpu.VMEM((s, 2 * FOX_DH), BF16),
            pltpu.VMEM((tq, LANES), F32),
            pltpu.VMEM((tq, 2 * FOX_DH), F32),
            pltpu.VMEM((tq, tk), F32),
            pltpu.VMEM((tq, tk), F32),
        ],
        compiler_params=_cparams(("arbitrary", "arbitrary", "arbitrary")),
        name="fox",
    )(qkv, qkv, qkv, sm, sm, gq, gk)


def _memkv_kernel(mem_ref, g_ref, w_ref, gk_ref, o_ref, *, n_k_tiles):
    j = pl.program_id(1)
    m = mem_ref[0]
    hn = (m * lax.rsqrt(jnp.mean(m * m, axis=-1, keepdims=True) + EPS) * g_ref[...]).astype(BF16)
    r = jnp.dot(hn, w_ref[...].astype(BF16), preferred_element_type=F32)
    parts = []
    for t in range(r.shape[1] // MEM_DH):
        rt = r[:, t * MEM_DH:(t + 1) * MEM_DH]
        parts.append(rt * lax.rsqrt(jnp.mean(rt * rt, axis=-1, keepdims=True) + EPS) * gk_ref[...])
    normed = jnp.concatenate(parts, axis=1)
    is_k = jnp.where(j < n_k_tiles, 1.0, 0.0)
    o_ref[0] = (is_k * normed + (1.0 - is_k) * r).astype(o_ref.dtype)


def _memkv(mem, g, w, gk):
    b, ml, d = mem.shape
    n = w.shape[1]
    tn = 2 * MEM_DH
    return pl.pallas_call(
        functools.partial(_memkv_kernel, n_k_tiles=(n // 2) // tn),
        out_shape=jax.ShapeDtypeStruct((b, ml, n), BF16),
        grid=(b, n // tn),
        in_specs=[
            pl.BlockSpec((1, ml, d), lambda i, j: (i, 0, 0)),
            pl.BlockSpec((1, d), lambda i, j: (0, 0)),
            pl.BlockSpec((d, tn), lambda i, j: (0, j)),
            pl.BlockSpec((1, MEM_DH), lambda i, j: (0, 0)),
        ],
        out_specs=pl.BlockSpec((1, ml, tn), lambda i, j: (i, 0, j)),
        compiler_params=_cparams(("arbitrary", "arbitrary")),
        name="memkv",
    )(mem, g, w, gk)


def _mem_kernel(q_ref, k_ref, v_ref, gq_ref, o_ref):
    for hd in range(MEM_HEADS):
        cs = slice(hd * MEM_DH, (hd + 1) * MEM_DH)
        qq = q_ref[0, :, cs].astype(F32)
        qn = qq * lax.rsqrt(jnp.mean(qq * qq, axis=-1, keepdims=True) + EPS) * gq_ref[...] * (MEM_DH ** -0.5)
        s = _dot_nt(qn, k_ref[0, :, cs])
        p = jnp.exp(s - jnp.max(s, axis=1, keepdims=True))
        p = p / jnp.sum(p, axis=1, keepdims=True)
        o_ref[0, :, cs] = jnp.dot(p.astype(BF16), v_ref[0, :, cs], preferred_element_type=F32).astype(o_ref.dtype)


def _mem_attn(qsrc, q_col_block, kv, gq):
    b, s, _ = qsrc.shape
    ml = kv.shape[1]
    wq = MEM_HEADS * MEM_DH
    tq = _tile(s, 512)
    return pl.pallas_call(
        _mem_kernel,
        out_shape=jax.ShapeDtypeStruct((b, s, wq), BF16),
        grid=(b, s // tq),
        in_specs=[
            pl.BlockSpec((1, tq, wq), lambda i, j: (i, j, q_col_block)),
            pl.BlockSpec((1, ml, wq), lambda i, j: (i, 0, 0)),
            pl.BlockSpec((1, ml, wq), lambda i, j: (i, 0, 1)),
            pl.BlockSpec((1, MEM_DH), lambda i, j: (0, 0)),
        ],
        out_specs=pl.BlockSpec((1, tq, wq), lambda i, j: (i, j, 0)),
        compiler_params=_cparams(("arbitrary", "arbitrary")),
        name="mem_attn",
    )(qsrc, kv, kv, gq)


def _merge_kernel(oa_ref, ob_ref, om_ref, wa_ref, wb_ref, wm_ref, ga_ref, gb_ref, gm_ref, y_ref):
    d = lambda o_ref, w_ref: jnp.dot(o_ref[...], w_ref[...].astype(BF16), preferred_element_type=F32)
    y = (ga_ref[...].astype(F32) * d(oa_ref, wa_ref)
         + gb_ref[...].astype(F32) * d(ob_ref, wb_ref)
         + gm_ref[...].astype(F32) * d(om_ref, wm_ref))
    y_ref[...] = y.astype(y_ref.dtype)


def _merge(oa, ob, om, wa, wb, wm, gates):
    t, ka = oa.shape
    dm = wa.shape[1]
    tm = _tile(t, 1024)
    tn = _tile(dm, 512)
    nb = dm // tn
    a_spec = lambda kk: pl.BlockSpec((tm, kk), lambda i, j: (i, 0))
    w_spec = lambda kk: pl.BlockSpec((kk, tn), lambda i, j: (0, j))
    g_spec = lambda off: pl.BlockSpec((tm, tn), lambda i, j: (i, off * nb + j))
    return pl.pallas_call(
        _merge_kernel,
        out_shape=jax.ShapeDtypeStruct((t, dm), BF16),
        grid=(t // tm, nb),
        in_specs=[a_spec(ka), a_spec(ob.shape[1]), a_spec(om.shape[1]),
                  w_spec(ka), w_spec(ob.shape[1]), w_spec(om.shape[1]),
                  g_spec(0), g_spec(1), g_spec(2)],
        out_specs=pl.BlockSpec((tm, tn), lambda i, j: (i, j)),
        compiler_params=_cparams(("arbitrary", "arbitrary")),
        name="merge",
    )(oa, ob, om, wa, wb, wm, gates, gates, gates)


def _layer(x, mem, g_mix, w_in, conv_w, a_log, dt_bias, gdn_norm_g, fox_b_f, fox_q_norm, fox_k_norm,
           g_mem, w_mem_kv, mem_q_norm, mem_k_norm, w_up_gdn, w_up_fox, w_up_mem, w_out, g_mlp, w_ff1, w_ff2):
    b, s, d = x.shape
    t = b * s
    gdn_qk = GDN_HEADS * GDN_DK
    gdn_v = GDN_HEADS * GDN_DV
    fox_w = FOX_HEADS * FOX_DH
    mem_w = MEM_HEADS * MEM_DH
    o_z = 2 * gdn_qk + gdn_v
    o_beta = o_z + gdn_v
    o_dec = o_beta + GDN_HEADS
    o_fq = o_dec + GDN_HEADS
    o_ff = o_fq + 3 * fox_w
    o_mq = o_ff + FOX_HEADS
    o_gate = o_mq + mem_w

    wt = w_in.T
    n_small = 2 * GDN_HEADS + FOX_HEADS
    w_small = jnp.concatenate([wt[o_beta:o_fq], wt[o_ff:o_mq],
                               jnp.zeros((LANES - n_small, d), F32)], axis=0)
    zpad = jnp.zeros((LANES - n_small,), F32)
    bias = jnp.concatenate([jnp.zeros((GDN_HEADS,), F32), dt_bias.astype(F32), fox_b_f.astype(F32), zpad])[None, :]
    alog = jnp.concatenate([jnp.zeros((GDN_HEADS,), F32), a_log.astype(F32), jnp.zeros((FOX_HEADS,), F32), zpad])[None, :]

    h, sm = _norm_small(x, g_mix[None, :], w_small, bias, alog)
    h2d = h.reshape(t, d)

    proj = functools.partial(_matmul, h2d, wt, tm=2048, tn=512, w_rows_are_outputs=True)
    qkvz = proj(out_dtype=F32, w_row_ranges=[(0, o_beta)], name="proj_gdn").reshape(b, s, -1)
    att = proj(out_dtype=BF16, w_row_ranges=[(o_fq, 3 * fox_w), (o_mq, mem_w)],
               name="proj_att").reshape(b, s, -1)
    gates = proj(out_dtype=BF16, act="sigmoid", w_row_ranges=[(o_gate, N_BRANCH * d)], name="proj_gate")

    o_a = _gdn(qkvz, sm, conv_w, gdn_norm_g[None, :])
    o_b = _fox(att, sm, fox_q_norm[None, :], fox_k_norm[None, :])
    kv_m = _memkv(mem, g_mem[None, :], w_mem_kv, mem_k_norm[None, :])
    o_m = _mem_attn(att, (3 * fox_w) // mem_w, kv_m, mem_q_norm[None, :])

    y = _merge(o_a.reshape(t, -1), o_b.reshape(t, -1), o_m.reshape(t, -1), w_up_gdn, w_up_fox, w_up_mem, gates)
    x1 = _matmul(y, w_out, out_dtype=F32, tm=2048, tn=512, residual=x.reshape(t, d), name="out_proj")

    h2 = _rmsnorm(x1, g_mlp[None, :])
    u = _matmul(h2, w_ff1, out_dtype=BF16, tm=2048, tn=512, act="relu2", name="ff1")
    out = _matmul(u, w_ff2.astype(BF16), out_dtype=F32, tm=512, tn=512, residual=x1, name="ff2")
    return out.reshape(b, s, d)


def kernel(x, mem, g_mix, w_in, conv_w, a_log, dt_bias, gdn_norm_g, fox_b_f, fox_q_norm, fox_k_norm, g_mem, w_mem_kv, mem_q_norm, mem_k_norm, w_up_gdn, w_up_fox, w_up_mem, w_out, g_mlp, w_ff1, w_ff2):
    depth = w_in.shape[0]
    for l in range(depth):
        x = _layer(x, mem, g_mix[l], w_in[l], conv_w[l], a_log[l], dt_bias[l], gdn_norm_g[l], fox_b_f[l],
                   fox_q_norm[l], fox_k_norm[l], g_mem[l], w_mem_kv[l], mem_q_norm[l], mem_k_norm[l],
                   w_up_gdn[l], w_up_fox[l], w_up_mem[l], w_out[l], g_mlp[l], w_ff1[l], w_ff2[l])
    return x
```

```python
import functools
import math

import jax
import jax.numpy as jnp
from jax import lax
from jax.experimental import pallas as pl
from jax.experimental.pallas import tpu as pltpu

F32 = jnp.float32
BF16 = jnp.bfloat16
EPS = 1e-6

GDN_HEADS = 8
GDN_DK = 128
GDN_DV = 128
GDN_CONV = 4
GDN_CHUNK = 128
GDN_HEADS_PER_STEP = 8
GDN_ROWS_PER_STEP = 256
FOX_HEADS = 8
FOX_DH = 128
FOX_TQ = 1024
FOX_TK = 512
MEM_HEADS = 4
MEM_DH = 256
N_BRANCH = 3
LANES = 128
NEG_BIG = -1e30
LOG2E = math.log2(math.e)

VMEM_LIMIT = 56 * 1024 * 1024


def _tile(n, pref):
    return pref if n % pref == 0 else n


def _cparams(sem):
    return pltpu.CompilerParams(dimension_semantics=sem, vmem_limit_bytes=VMEM_LIMIT)


def _dot(a, b):
    return jnp.dot(a.astype(BF16), b.astype(BF16), preferred_element_type=F32)


def _dot_nt(a, b):
    return lax.dot_general(a.astype(BF16), b.astype(BF16), (((1,), (1,)), ((), ())),
                           preferred_element_type=F32)


def _split2(a):
    hi = a.astype(BF16)
    lo = (a - hi.astype(F32)).astype(BF16)
    return hi, lo


def _split3(a):
    hi = a.astype(BF16).astype(F32)
    r = a - hi
    mid = r.astype(BF16).astype(F32)
    lo = (r - mid).astype(BF16).astype(F32)
    return hi, mid, lo


def _dot3(a, b):
    a_hi, a_lo = _split2(a)
    b_hi, b_lo = _split2(b)
    d = functools.partial(jnp.dot, preferred_element_type=F32)
    return d(a_hi, b_hi) + d(a_hi, b_lo) + d(a_lo, b_hi)


def _dot3_nt(a, b):
    a_hi, a_lo = _split2(a)
    b_hi, b_lo = _split2(b)
    d = lambda p, q: lax.dot_general(p, q, (((1,), (1,)), ((), ())), preferred_element_type=F32)
    return d(a_hi, b_hi) + d(a_hi, b_lo) + d(a_lo, b_hi)


def _dot2(a, b_bf16):
    a_hi, a_lo = _split2(a)
    d = functools.partial(jnp.dot, preferred_element_type=F32)
    return d(a_hi, b_bf16) + d(a_lo, b_bf16)


def _dot_exact_lhs(l_bf16, v):
    hi, mid, lo = _split3(v)
    d = functools.partial(jnp.dot, preferred_element_type=F32)
    return d(l_bf16, hi.astype(BF16)) + d(l_bf16, mid.astype(BF16)) + d(l_bf16, lo.astype(BF16))


def _lane_col(a, idx):
    lane = lax.broadcasted_iota(jnp.int32, a.shape, 1)
    return jnp.sum(jnp.where(lane == idx, a, 0.0), axis=1, keepdims=True)


def _softplus(x):
    return jnp.maximum(x, 0.0) + jnp.log1p(jnp.exp(-jnp.abs(x)))


def _norm_small_kernel(x_ref, g_ref, ws_ref, bias_ref, alog_ref, h_ref, sm_ref, carry_ref, *, tm):
    s = pl.program_id(1)

    @pl.when(s == 0)
    def _():
        carry_ref[...] = jnp.zeros_like(carry_ref)

    x = x_ref[0]
    h = x * lax.rsqrt(jnp.mean(x * x, axis=-1, keepdims=True) + EPS) * g_ref[...]
    h_ref[0] = h.astype(BF16)

    pre = _dot3_nt(h, ws_ref[...]) + bias_ref[...]
    lane = lax.broadcasted_iota(jnp.int32, pre.shape, 1)
    nh = GDN_HEADS
    beta = jax.nn.sigmoid(pre)
    gdec = -jnp.exp(alog_ref[...]) * _softplus(pre)
    logf = -_softplus(-pre)
    vals = jnp.where(lane < nh, beta, jnp.where(lane < 2 * nh, gdec, jnp.where(lane < 3 * nh, logf, 0.0)))

    row = lax.broadcasted_iota(jnp.int32, (tm, tm), 0)
    col = lax.broadcasted_iota(jnp.int32, (tm, tm), 1)
    low = col <= row
    l_full = jnp.where(low, 1.0, 0.0).astype(BF16)
    sh = GDN_CHUNK.bit_length() - 1
    same_chunk = (row >> sh) == (col >> sh)
    l_blk = jnp.where(low, jnp.where(same_chunk, 1.0, 0.0), 0.0).astype(BF16)
    cs_blk = _dot_exact_lhs(l_blk, vals)
    cs_full = _dot_exact_lhs(l_full, vals) + carry_ref[0:1, :]
    sm_ref[0] = jnp.where(lane < nh, vals, jnp.where(lane < 2 * nh, cs_blk, jnp.where(lane < 3 * nh, cs_full, 0.0)))
    carry_ref[...] = jnp.broadcast_to(cs_full[tm - 1:tm, :], carry_ref.shape)


def _norm_small(x, g, w_small, bias, alog):
    b, s, d = x.shape
    tm = _tile(s, 512)
    return pl.pallas_call(
        functools.partial(_norm_small_kernel, tm=tm),
        out_shape=(jax.ShapeDtypeStruct((b, s, d), BF16), jax.ShapeDtypeStruct((b, s, LANES), F32)),
        grid=(b, s // tm),
        in_specs=[
            pl.BlockSpec((1, tm, d), lambda i, j: (i, j, 0)),
            pl.BlockSpec((1, d), lambda i, j: (0, 0)),
            pl.BlockSpec((LANES, d), lambda i, j: (0, 0)),
            pl.BlockSpec((1, LANES), lambda i, j: (0, 0)),
            pl.BlockSpec((1, LANES), lambda i, j: (0, 0)),
        ],
        out_specs=(
            pl.BlockSpec((1, tm, d), lambda i, j: (i, j, 0)),
            pl.BlockSpec((1, tm, LANES), lambda i, j: (i, j, 0)),
        ),
        scratch_shapes=[pltpu.VMEM((8, LANES), F32)],
        compiler_params=_cparams(("arbitrary", "arbitrary")),
        name="norm_small",
    )(x, g, w_small, bias, alog)


def _out_norm_kernel(y_ref, w_ref, x_ref, g_ref, x1_ref, h_ref):
    x1 = x_ref[...] + jnp.dot(y_ref[...], w_ref[...], preferred_element_type=F32)
    x1_ref[...] = x1
    h_ref[...] = (x1 * lax.rsqrt(jnp.mean(x1 * x1, axis=-1, keepdims=True) + EPS) * g_ref[...]).astype(h_ref.dtype)


def _out_norm(y, w_bf16, x2d, g):
    t, d = x2d.shape
    k = y.shape[1]
    tm = _tile(t, 512)
    row = lambda width: pl.BlockSpec((tm, width), lambda i: (i, 0))
    return pl.pallas_call(
        _out_norm_kernel,
        out_shape=(jax.ShapeDtypeStruct((t, d), F32), jax.ShapeDtypeStruct((t, d), BF16)),
        grid=(t // tm,),
        in_specs=[row(k), pl.BlockSpec((k, d), lambda i: (0, 0)), row(d), pl.BlockSpec((1, d), lambda i: (0, 0))],
        out_specs=(row(d), row(d)),
        compiler_params=_cparams(("arbitrary",)),
        name="out_norm",
    )(y, w_bf16, x2d, g)


def _mm_kernel(a_ref, w_ref, *rest, act, has_res, w_rows_are_outputs):
    if has_res:
        r_ref, o_ref = rest
    else:
        (o_ref,) = rest
    contract = (((1,), (1,)), ((), ())) if w_rows_are_outputs else (((1,), (0,)), ((), ()))
    acc = lax.dot_general(a_ref[...], w_ref[...].astype(BF16), contract, preferred_element_type=F32)
    if act == "sigmoid":
        acc = jax.nn.sigmoid(acc)
    elif act == "relu2":
        r = jnp.maximum(acc, 0.0)
        acc = r * r
    if has_res:
        acc = acc + r_ref[...]
    o_ref[...] = acc.astype(o_ref.dtype)


def _matmul(a, w, *, out_dtype, tm, tn, act=None, residual=None, w_rows_are_outputs=False,
            w_row_ranges=None, name="matmul"):
    m, k = a.shape
    if w_row_ranges is not None:
        assert w_rows_are_outputs
        n = sum(r for _, r in w_row_ranges)
    else:
        n = w.shape[0] if w_rows_are_outputs else w.shape[1]
    tm = _tile(m, tm)
    tn = _tile(n, tn)
    if w_row_ranges is not None:
        assert all(r % tn == 0 and f % 8 == 0 for f, r in w_row_ranges)

        def w_rows(i, j):
            start, first_blk = jnp.int32(0), 0
            for f, r in w_row_ranges:
                start = jnp.where(j >= first_blk, f + (j - first_blk) * tn, start)
                first_blk += r // tn
            return pl.multiple_of(start, 8), 0

        w_spec = pl.BlockSpec((pl.Element(tn), pl.Element(k)), w_rows)
    elif w_rows_are_outputs:
        w_spec = pl.BlockSpec((tn, k), lambda i, j: (j, 0))
    else:
        w_spec = pl.BlockSpec((k, tn), lambda i, j: (0, j))
    in_specs = [pl.BlockSpec((tm, k), lambda i, j: (i, 0)), w_spec]
    args = [a, w]
    if residual is not None:
        in_specs.append(pl.BlockSpec((tm, tn), lambda i, j: (i, j)))
        args.append(residual)
    return pl.pallas_call(
        functools.partial(_mm_kernel, act=act, has_res=residual is not None,
                          w_rows_are_outputs=w_rows_are_outputs),
        out_shape=jax.ShapeDtypeStruct((m, n), out_dtype),
        grid=(m // tm, n // tn),
        in_specs=in_specs,
        out_specs=pl.BlockSpec((tm, tn), lambda i, j: (i, j)),
        compiler_params=_cparams(("arbitrary", "arbitrary")),
        name=name,
    )(*args)


def _gdn_kernel(q_ref, k_ref, v_ref, z_ref, sm_ref, cwq_ref, cwk_ref, cwv_ref, gn_ref, o_ref,
                state_ref, tail_ref, ext_ref, qs, ks, vs, *, ts, nc, hb, nb):
    hg = pl.program_id(0)
    si = pl.program_id(1)
    c_sz = GDN_CHUNK

    @pl.when(si == 0)
    def _():
        state_ref[...] = jnp.zeros_like(state_ref)
        tail_ref[...] = jnp.zeros_like(tail_ref)

    def conv_silu(x_ref, w_ref, bi, idx):
        x = x_ref[bi]
        ext_ref[0:8, :] = tail_ref[bi, idx]
        ext_ref[8:8 + ts, :] = x
        w = w_ref[...]
        y = (ext_ref[5:5 + ts, :] * w[0:1, :] + ext_ref[6:6 + ts, :] * w[1:2, :]
             + ext_ref[7:7 + ts, :] * w[2:3, :] + x * w[3:4, :])
        tail_ref[bi, idx] = x[ts - 8:ts, :]
        return y * jax.nn.sigmoid(y)

    for bi in range(nb):
        q = conv_silu(q_ref, cwq_ref, bi, 0)
        k = conv_silu(k_ref, cwk_ref, bi, 1)
        vs[bi] = conv_silu(v_ref, cwv_ref, bi, 2)
        for hh in range(hb):
            cs = slice(hh * LANES, (hh + 1) * LANES)
            qh = q[:, cs]
            kh = k[:, cs]
            qs[bi, :, cs] = qh * lax.rsqrt(jnp.sum(qh * qh, axis=-1, keepdims=True) + EPS) * (GDN_DK ** -0.5)
            ks[bi, :, cs] = kh * lax.rsqrt(jnp.sum(kh * kh, axis=-1, keepdims=True) + EPS)

    ri = lax.broadcasted_iota(jnp.int32, (c_sz, c_sz), 0)
    ci = lax.broadcasted_iota(jnp.int32, (c_sz, c_sz), 1)
    strict = ri > ci
    incl = ri >= ci
    eye = jnp.where(ri == ci, 1.0, 0.0)
    n_lvl = c_sz.bit_length() - 1
    lvl_masks = []
    for l in range(n_lvl):
        same = (ri >> (l + 1)) == (ci >> (l + 1))
        lvl_masks.append(jnp.where(
            same, jnp.where(((ri >> l) & 1) == 1, jnp.where(((ci >> l) & 1) == 0, 1.0, 0.0), 0.0), 0.0))

    def body(c, carry):
        rows = pl.ds(pl.multiple_of(c * c_sz, c_sz), c_sz)
        chains = [(bi, hh) for bi in range(nb) for hh in range(hb)]
        mlows, rhss, lhs2s, qds, cds = [], [], [], [], []
        for bi, hh in chains:
            cs = slice(hh * LANES, (hh + 1) * LANES)
            hd = hg * hb + hh
            smc = sm_ref[bi, rows, :]
            beta = _lane_col(smc, hd)
            gam = _lane_col(smc, hd + GDN_HEADS)
            kc = ks[bi, rows, cs]
            qc = qs[bi, rows, cs]
            egam = jnp.exp(gam)
            g_last = gam[c_sz - 1:c_sz, :]
            gcol = jnp.broadcast_to(gam, (c_sz, c_sz))
            diff = gcol - gcol.T
            e = jnp.exp(jnp.where(incl, diff, 0.0))
            mlows.append(beta * _dot_nt(kc, kc) * jnp.where(strict, e, 0.0))
            qk = _dot_nt(qc, kc) * jnp.where(incl, e, 0.0)
            kd_t = (kc * jnp.exp(g_last - gam)).T
            lhs2s.append(jnp.concatenate([qk, kd_t], axis=0).astype(BF16))
            rhss.append(jnp.concatenate([kc * (beta * egam), vs[bi, rows, cs] * beta], axis=1).astype(BF16))
            qds.append(qc * egam)
            cds.append(jnp.exp(g_last))
        xs = [eye - lvl_masks[0] * m for m in mlows]
        for l in range(1, n_lvl):
            tl = [_dot(lvl_masks[l] * m, x) for m, x in zip(mlows, xs)]
            xs = [x - _dot(x, t) for x, t in zip(xs, tl)]
        rs = [eye - x - _dot3(m, x) for m, x in zip(mlows, xs)]
        xs = [x + _dot(x, r) for x, r in zip(xs, rs)]
        wus = [_dot2(x, r) for x, r in zip(xs, rhss)]
        for n, (bi, hh) in enumerate(chains):
            cs = slice(hh * LANES, (hh + 1) * LANES)
            state = state_ref[n]
            wu = wus[n]
            a = _dot(jnp.concatenate([wu[:, :GDN_DK], qds[n]], axis=0), state)
            u = wu[:, GDN_DK:] - a[:c_sz]
            b2 = jnp.dot(lhs2s[n], u.astype(BF16), preferred_element_type=F32)
            o = a[c_sz:] + b2[:c_sz]
            state_ref[n] = state * cds[n] + b2[c_sz:]
            z = z_ref[bi, rows, cs]
            on = o * lax.rsqrt(jnp.mean(o * o, axis=-1, keepdims=True) + EPS) * gn_ref[...]
            o_ref[bi, rows, cs] = (on * (z * jax.nn.sigmoid(z))).astype(o_ref.dtype)
        return carry

    lax.fori_loop(0, nc, body, 0)


def _gdn(qkvz, sm, conv_w, gn):
    b, s, _ = qkvz.shape
    nh = GDN_HEADS
    hb = GDN_HEADS_PER_STEP
    ng = nh // hb
    wb = hb * LANES
    ts = _tile(s, GDN_ROWS_PER_STEP)
    nc = ts // GDN_CHUNK
    blk = lambda off: pl.BlockSpec((b, ts, wb), lambda h, j: (0, j, off + h))
    cw = lambda off: pl.BlockSpec((GDN_CONV, wb), lambda h, j: (0, off + h))
    return pl.pallas_call(
        functools.partial(_gdn_kernel, ts=ts, nc=nc, hb=hb, nb=b),
        out_shape=jax.ShapeDtypeStruct((b, s, nh * GDN_DV), BF16),
        grid=(ng, s // ts),
        in_specs=[
            blk(0), blk(ng), blk(2 * ng), blk(3 * ng),
            pl.BlockSpec((b, ts, LANES), lambda h, j: (0, j, 0)),
            cw(0), cw(ng), cw(2 * ng),
            pl.BlockSpec((1, GDN_DV), lambda h, j: (0, 0)),
        ],
        out_specs=pl.BlockSpec((b, ts, wb), lambda h, j: (0, j, h)),
        scratch_shapes=[
            pltpu.VMEM((b * hb, GDN_DK, GDN_DV), F32),
            pltpu.VMEM((b, 3, 8, wb), F32),
            pltpu.VMEM((ts + 8, wb), F32),
            pltpu.VMEM((b, ts, wb), F32),
            pltpu.VMEM((b, ts, wb), F32),
            pltpu.VMEM((b, ts, wb), F32),
        ],
        compiler_params=_cparams(("arbitrary", "arbitrary")),
        name="gdn",
    )(qkvz, qkvz, qkvz, qkvz, sm, conv_w, conv_w, conv_w, gn)


def _fox_kernel(q_ref, k_ref, v_ref, smq_ref, smk_ref, gq_ref, gk_ref, o_ref,
                kaug_ref, vaug_ref, m_s, acc_s, sa_ref, sb_ref, *, tq, tk, nk):
    hd = pl.program_id(1)
    qi = pl.program_id(2)
    cidx = hd + 2 * GDN_HEADS
    n_sub = tq // tk
    n_lt = tk // LANES

    @pl.when(qi == 0)
    def _():
        lane = lax.broadcasted_iota(jnp.int32, (tk, LANES), 1)

        def build(j, carry):
            rows = pl.ds(pl.multiple_of(j * tk, tk), tk)
            kk = k_ref[0, rows, :].astype(F32)
            kn = kk * lax.rsqrt(jnp.mean(kk * kk, axis=-1, keepdims=True) + EPS) * gk_ref[...]
            hi, mid, lo = _split3(_lane_col(smk_ref[0, rows, :], cidx) * LOG2E)
            aug = jnp.where(lane < 3, 1.0, jnp.where(lane == 3, -hi, jnp.where(lane == 4, -mid, jnp.where(lane == 5, -lo, 0.0))))
            kaug_ref[rows, 0:FOX_DH] = kn.astype(BF16)
            kaug_ref[rows, FOX_DH:2 * FOX_DH] = aug.astype(BF16)
            vaug_ref[rows, 0:FOX_DH] = v_ref[0, rows, :]
            vaug_ref[rows, FOX_DH:2 * FOX_DH] = jnp.ones((tk, FOX_DH), BF16)
            return carry
        lax.fori_loop(0, nk, build, 0)

    lane = lax.broadcasted_iota(jnp.int32, (tq, LANES), 1)
    qq = q_ref[0].astype(F32)
    qn = qq * lax.rsqrt(jnp.mean(qq * qq, axis=-1, keepdims=True) + EPS) * gq_ref[...] * (FOX_DH ** -0.5 * LOG2E)
    hi, mid, lo = _split3(_lane_col(smq_ref[0], cidx) * LOG2E)
    aug = jnp.where(lane == 0, hi, jnp.where(lane == 1, mid, jnp.where(lane == 2, lo, jnp.where(lane < 6, 1.0, 0.0))))
    q_aug = jnp.concatenate([qn.astype(BF16), aug.astype(BF16)], axis=1)

    m_s[...] = jnp.full_like(m_s, NEG_BIG)
    acc_s[...] = jnp.zeros_like(acc_s)

    def scores(s_ref, j):
        rows = pl.ds(pl.multiple_of(j * tk, tk), tk)
        s_ref[...] = lax.dot_general(q_aug, kaug_ref[rows, :], (((1,), (1,)), ((), ())),
                                     preferred_element_type=F32)

    def accumulate(s_ref, j, diag):
        rows = pl.ds(pl.multiple_of(j * tk, tk), tk)
        v_blk = vaug_ref[rows, :]
        for r in range(n_sub):
            if diag is not None and r < diag:
                continue
            rs = slice(r * tk, (r + 1) * tk)
            tiles = [s_ref[rs, c * LANES:(c + 1) * LANES] for c in range(n_lt)]
            if diag is not None and r == diag:
                ri = lax.broadcasted_iota(jnp.int32, (tk, LANES), 0)
                ci = lax.broadcasted_iota(jnp.int32, (tk, LANES), 1)
                tiles = [jnp.where(ri >= ci + c * LANES, t, NEG_BIG) for c, t in enumerate(tiles)]
            mx = tiles[0]
            for t in tiles[1:]:
                mx = jnp.maximum(mx, t)
            m_prev = m_s[rs, :]
            m_next = jnp.maximum(m_prev, jnp.max(mx, axis=1, keepdims=True))
            p = jnp.concatenate([jnp.exp2(t - m_next) for t in tiles], axis=1).astype(BF16)
            alpha = jnp.exp2(m_prev - m_next)
            acc_s[rs, :] = (jnp.concatenate([alpha, alpha], axis=1) * acc_s[rs, :]
                            + jnp.dot(p, v_blk, preferred_element_type=F32))
            m_s[rs, :] = m_next

    scores(sa_ref, 0)

    def body(i, carry):
        j = 2 * i
        scores(sb_ref, j + 1)
        accumulate(sa_ref, j, None)
        scores(sa_ref, j + 2)
        accumulate(sb_ref, j + 1, None)
        return carry

    n_below = n_sub * qi
    lax.fori_loop(0, n_below // 2, body, 0)
    bufs = (sa_ref, sb_ref)
    for e in range(n_sub):
        if e + 1 < n_sub:
            scores(bufs[(e + 1) % 2], n_below + e + 1)
        accumulate(bufs[e % 2], n_below + e, e)

    acc = acc_s[...]
    o_ref[0] = (acc[:, :FOX_DH] / acc[:, FOX_DH:]).astype(o_ref.dtype)


def _fox(qkv, sm, gq, gk):
    b, s, _ = qkv.shape
    nh = FOX_HEADS
    tq = _tile(s, FOX_TQ)
    tk = _tile(tq, FOX_TK)
    assert (tq // tk) % 2 == 0, "the score pipeline consumes key blocks in pairs"
    nk = s // tk
    return pl.pallas_call(
        functools.partial(_fox_kernel, tq=tq, tk=tk, nk=nk),
        out_shape=jax.ShapeDtypeStruct((b, s, nh * FOX_DH), BF16),
        grid=(b, nh, s // tq),
        in_specs=[
            pl.BlockSpec((1, tq, FOX_DH), lambda i, h, j: (i, j, h)),
            pl.BlockSpec((1, s, FOX_DH), lambda i, h, j: (i, 0, nh + h)),
            pl.BlockSpec((1, s, FOX_DH), lambda i, h, j: (i, 0, 2 * nh + h)),
            pl.BlockSpec((1, tq, LANES), lambda i, h, j: (i, j, 0)),
            pl.BlockSpec((1, s, LANES), lambda i, h, j: (i, 0, 0)),
            pl.BlockSpec((1, FOX_DH), lambda i, h, j: (0, 0)),
            pl.BlockSpec((1, FOX_DH), lambda i, h, j: (0, 0)),
        ],
        out_specs=pl.BlockSpec((1, tq, FOX_DH), lambda i, h, j: (i, j, h)),
        scratch_shapes=[
            pltpu.VMEM((s, 2 * FOX_DH), BF16),
            pltpu.VMEM((s, 2 * FOX_DH), BF16),
            pltpu.VMEM((tq, LANES), F32),
            pltpu.VMEM((tq, 2 * FOX_DH), F32),
            pltpu.VMEM((tq, tk), F32),
            pltpu.VMEM((tq, tk), F32),
        ],
        compiler_params=_cparams(("arbitrary", "arbitrary", "arbitrary")),
        name="fox",
    )(qkv, qkv, qkv, sm, sm, gq, gk)


def _memkv_kernel(mem_ref, g_ref, w_ref, gk_ref, o_ref, *, n_k_tiles):
    j = pl.program_id(1)
    m = mem_ref[0]
    hn = (m * lax.rsqrt(jnp.mean(m * m, axis=-1, keepdims=True) + EPS) * g_ref[...]).astype(BF16)
    r = jnp.dot(hn, w_ref[...].astype(BF16), preferred_element_type=F32)
    parts = []
    for t in range(r.shape[1] // MEM_DH):
        rt = r[:, t * MEM_DH:(t + 1) * MEM_DH]
        parts.append(rt * lax.rsqrt(jnp.mean(rt * rt, axis=-1, keepdims=True) + EPS) * gk_ref[...])
    normed = jnp.concatenate(parts, axis=1)
    is_k = jnp.where(j < n_k_tiles, 1.0, 0.0)
    o_ref[0] = (is_k * normed + (1.0 - is_k) * r).astype(o_ref.dtype)


def _memkv(mem, g, w, gk):
    b, ml, d = mem.shape
    n = w.shape[1]
    tn = 2 * MEM_DH
    return pl.pallas_call(
        functools.partial(_memkv_kernel, n_k_tiles=(n // 2) // tn),
        out_shape=jax.ShapeDtypeStruct((b, ml, n), BF16),
        grid=(b, n // tn),
        in_specs=[
            pl.BlockSpec((1, ml, d), lambda i, j: (i, 0, 0)),
            pl.BlockSpec((1, d), lambda i, j: (0, 0)),
            pl.BlockSpec((d, tn), lambda i, j: (0, j)),
            pl.BlockSpec((1, MEM_DH), lambda i, j: (0, 0)),
        ],
        out_specs=pl.BlockSpec((1, ml, tn), lambda i, j: (i, 0, j)),
        compiler_params=_cparams(("arbitrary", "arbitrary")),
        name="memkv",
    )(mem, g, w, gk)


def _mem_kernel(q_ref, k_ref, v_ref, gq_ref, o_ref):
    for hd in range(MEM_HEADS):
        cs = slice(hd * MEM_DH, (hd + 1) * MEM_DH)
        qq = q_ref[0, :, cs].astype(F32)
        qn = qq * lax.rsqrt(jnp.mean(qq * qq, axis=-1, keepdims=True) + EPS) * gq_ref[...] * (MEM_DH ** -0.5)
        s = _dot_nt(qn, k_ref[0, :, cs])
        p = jnp.exp(s - jnp.max(s, axis=1, keepdims=True))
        p = p / jnp.sum(p, axis=1, keepdims=True)
        o_ref[0, :, cs] = jnp.dot(p.astype(BF16), v_ref[0, :, cs], preferred_element_type=F32).astype(o_ref.dtype)


def _mem_attn(qsrc, q_col_block, kv, gq):
    b, s, _ = qsrc.shape
    ml = kv.shape[1]
    wq = MEM_HEADS * MEM_DH
    tq = _tile(s, 512)
    return pl.pallas_call(
        _mem_kernel,
        out_shape=jax.ShapeDtypeStruct((b, s, wq), BF16),
        grid=(b, s // tq),
        in_specs=[
            pl.BlockSpec((1, tq, wq), lambda i, j: (i, j, q_col_block)),
            pl.BlockSpec((1, ml, wq), lambda i, j: (i, 0, 0)),
            pl.BlockSpec((1, ml, wq), lambda i, j: (i, 0, 1)),
            pl.BlockSpec((1, MEM_DH), lambda i, j: (0, 0)),
        ],
        out_specs=pl.BlockSpec((1, tq, wq), lambda i, j: (i, j, 0)),
        compiler_params=_cparams(("arbitrary", "arbitrary")),
        name="mem_attn",
    )(qsrc, kv, kv, gq)


def _merge_kernel(oa_ref, ob_ref, om_ref, wa_ref, wb_ref, wm_ref, ga_ref, gb_ref, gm_ref, y_ref):
    d = lambda o_ref, w_ref: jnp.dot(o_ref[...], w_ref[...].astype(BF16), preferred_element_type=F32)
    y = (ga_ref[...].astype(F32) * d(oa_ref, wa_ref)
         + gb_ref[...].astype(F32) * d(ob_ref, wb_ref)
         + gm_ref[...].astype(F32) * d(om_ref, wm_ref))
    y_ref[...] = y.astype(y_ref.dtype)


def _merge(oa, ob, om, wa, wb, wm, gates):
    t, ka = oa.shape
    dm = wa.shape[1]
    tm = _tile(t, 2048)
    tn = _tile(dm, 256)
    nb = dm // tn
    a_spec = lambda kk: pl.BlockSpec((tm, kk), lambda i, j: (i, 0))
    w_spec = lambda kk: pl.BlockSpec((kk, tn), lambda i, j: (0, j))
    g_spec = lambda off: pl.BlockSpec((tm, tn), lambda i, j: (i, off * nb + j))
    return pl.pallas_call(
        _merge_kernel,
        out_shape=jax.ShapeDtypeStruct((t, dm), BF16),
        grid=(t // tm, nb),
        in_specs=[a_spec(ka), a_spec(ob.shape[1]), a_spec(om.shape[1]),
                  w_spec(ka), w_spec(ob.shape[1]), w_spec(om.shape[1]),
                  g_spec(0), g_spec(1), g_spec(2)],
        out_specs=pl.BlockSpec((tm, tn), lambda i, j: (i, j)),
        compiler_params=_cparams(("arbitrary", "arbitrary")),
        name="merge",
    )(oa, ob, om, wa, wb, wm, gates, gates, gates)


def _layer(x, mem, g_mix, w_in, conv_w, a_log, dt_bias, gdn_norm_g, fox_b_f, fox_q_norm, fox_k_norm,
           g_mem, w_mem_kv, mem_q_norm, mem_k_norm, w_up_gdn, w_up_fox, w_up_mem, w_out, g_mlp, w_ff1, w_ff2):
    b, s, d = x.shape
    t = b * s
    gdn_qk = GDN_HEADS * GDN_DK
    gdn_v = GDN_HEADS * GDN_DV
    fox_w = FOX_HEADS * FOX_DH
    mem_w = MEM_HEADS * MEM_DH
    o_z = 2 * gdn_qk + gdn_v
    o_beta = o_z + gdn_v
    o_dec = o_beta + GDN_HEADS
    o_fq = o_dec + GDN_HEADS
    o_ff = o_fq + 3 * fox_w
    o_mq = o_ff + FOX_HEADS
    o_gate = o_mq + mem_w

    wt = w_in.T
    n_small = 2 * GDN_HEADS + FOX_HEADS
    w_small = jnp.concatenate([wt[o_beta:o_fq], wt[o_ff:o_mq],
                               jnp.zeros((LANES - n_small, d), F32)], axis=0)
    zpad = jnp.zeros((LANES - n_small,), F32)
    bias = jnp.concatenate([jnp.zeros((GDN_HEADS,), F32), dt_bias.astype(F32), fox_b_f.astype(F32), zpad])[None, :]
    alog = jnp.concatenate([jnp.zeros((GDN_HEADS,), F32), a_log.astype(F32), jnp.zeros((FOX_HEADS,), F32), zpad])[None, :]

    h, sm = _norm_small(x, g_mix[None, :], w_small, bias, alog)
    h2d = h.reshape(t, d)

    proj = functools.partial(_matmul, h2d, wt, tm=2048, tn=512, w_rows_are_outputs=True)
    qkvz = proj(out_dtype=F32, w_row_ranges=[(0, o_beta)], name="proj_gdn").reshape(b, s, -1)
    att = proj(out_dtype=BF16, w_row_ranges=[(o_fq, 3 * fox_w), (o_mq, mem_w)],
               name="proj_att").reshape(b, s, -1)
    gates = proj(out_dtype=BF16, act="sigmoid", w_row_ranges=[(o_gate, N_BRANCH * d)], name="proj_gate")

    o_a = _gdn(qkvz, sm, conv_w, gdn_norm_g[None, :])
    o_b = _fox(att, sm, fox_q_norm[None, :], fox_k_norm[None, :])
    kv_m = _memkv(mem, g_mem[None, :], w_mem_kv, mem_k_norm[None, :])
    o_m = _mem_attn(att, (3 * fox_w) // mem_w, kv_m, mem_q_norm[None, :])

    y = _merge(o_a.reshape(t, -1), o_b.reshape(t, -1), o_m.reshape(t, -1), w_up_gdn, w_up_fox, w_up_mem, gates)
    x1, h2 = _out_norm(y, w_out.astype(BF16), x.reshape(t, d), g_mlp[None, :])
    u = _matmul(h2, w_ff1, out_dtype=BF16, tm=2048, tn=512, act="relu2", name="ff1")
    out = _matmul(u, w_ff2.astype(BF16), out_dtype=F32, tm=1024, tn=256, residual=x1, name="ff2")
    return out.reshape(b, s, d)


def kernel(x, mem, g_mix, w_in, conv_w, a_log, dt_bias, gdn_norm_g, fox_b_f, fox_q_norm, fox_k_norm, g_mem, w_mem_kv, mem_q_norm, mem_k_norm, w_up_gdn, w_up_fox, w_up_mem, w_out, g_mlp, w_ff1, w_ff2):
    depth = w_in.shape[0]
    for l in range(depth):
        x = _layer(x, mem, g_mix[l], w_in[l], conv_w[l], a_log[l], dt_bias[l], gdn_norm_g[l], fox_b_f[l],
                   fox_q_norm[l], fox_k_norm[l], g_mem[l], w_mem_kv[l], mem_q_norm[l], mem_k_norm[l],
                   w_up_gdn[l], w_up_fox[l], w_up_mem[l], w_out[l], g_mlp[l], w_ff1[l], w_ff2[l])
    return x
```

```python
import functools
import math

import jax
import jax.numpy as jnp
from jax import lax
from jax.experimental import pallas as pl
from jax.experimental.pallas import tpu as pltpu

F32 = jnp.float32
BF16 = jnp.bfloat16
EPS = 1e-6

GDN_HEADS = 8
GDN_DK = 128
GDN_DV = 128
GDN_CONV = 4
GDN_CHUNK = 128
GDN_HEADS_PER_STEP = 8
GDN_ROWS_PER_STEP = 256
FOX_HEADS = 8
FOX_DH = 128
FOX_TQ = 1024
FOX_TK = 512
MEM_HEADS = 4
MEM_DH = 256
N_BRANCH = 3
LANES = 128
NEG_BIG = -1e30
LOG2E = math.log2(math.e)

VMEM_LIMIT = 56 * 1024 * 1024


def _tile(n, pref):
    return pref if n % pref == 0 else n


def _cparams(sem):
    return pltpu.CompilerParams(dimension_semantics=sem, vmem_limit_bytes=VMEM_LIMIT)


def _dot(a, b):
    return jnp.dot(a.astype(BF16), b.astype(BF16), preferred_element_type=F32)


def _dot_nt(a, b):
    return lax.dot_general(a.astype(BF16), b.astype(BF16), (((1,), (1,)), ((), ())),
                           preferred_element_type=F32)


def _split2(a):
    hi = a.astype(BF16)
    lo = (a - hi.astype(F32)).astype(BF16)
    return hi, lo


def _split3(a):
    hi = a.astype(BF16).astype(F32)
    r = a - hi
    mid = r.astype(BF16).astype(F32)
    lo = (r - mid).astype(BF16).astype(F32)
    return hi, mid, lo


def _dot3(a, b):
    a_hi, a_lo = _split2(a)
    b_hi, b_lo = _split2(b)
    d = functools.partial(jnp.dot, preferred_element_type=F32)
    return d(a_hi, b_hi) + d(a_hi, b_lo) + d(a_lo, b_hi)


def _dot3_nt(a, b):
    a_hi, a_lo = _split2(a)
    b_hi, b_lo = _split2(b)
    d = lambda p, q: lax.dot_general(p, q, (((1,), (1,)), ((), ())), preferred_element_type=F32)
    return d(a_hi, b_hi) + d(a_hi, b_lo) + d(a_lo, b_hi)


def _dot2(a, b_bf16):
    a_hi, a_lo = _split2(a)
    d = functools.partial(jnp.dot, preferred_element_type=F32)
    return d(a_hi, b_bf16) + d(a_lo, b_bf16)


def _dot_exact_lhs(l_bf16, v):
    hi, mid, lo = _split3(v)
    d = functools.partial(jnp.dot, preferred_element_type=F32)
    return d(l_bf16, hi.astype(BF16)) + d(l_bf16, mid.astype(BF16)) + d(l_bf16, lo.astype(BF16))


def _lane_col(a, idx):
    lane = lax.broadcasted_iota(jnp.int32, a.shape, 1)
    return jnp.sum(jnp.where(lane == idx, a, 0.0), axis=1, keepdims=True)


def _softplus(x):
    return jnp.maximum(x, 0.0) + jnp.log1p(jnp.exp(-jnp.abs(x)))


def _norm_small_kernel(x_ref, g_ref, ws_ref, bias_ref, alog_ref, h_ref, sm_ref, carry_ref, *, tm):
    s = pl.program_id(1)

    @pl.when(s == 0)
    def _():
        carry_ref[...] = jnp.zeros_like(carry_ref)

    x = x_ref[0]
    h = x * lax.rsqrt(jnp.mean(x * x, axis=-1, keepdims=True) + EPS) * g_ref[...]
    h_ref[0] = h.astype(BF16)

    pre = _dot3_nt(h, ws_ref[...]) + bias_ref[...]
    lane = lax.broadcasted_iota(jnp.int32, pre.shape, 1)
    nh = GDN_HEADS
    beta = jax.nn.sigmoid(pre)
    gdec = -jnp.exp(alog_ref[...]) * _softplus(pre)
    logf = -_softplus(-pre)
    vals = jnp.where(lane < nh, beta, jnp.where(lane < 2 * nh, gdec, jnp.where(lane < 3 * nh, logf, 0.0)))

    row = lax.broadcasted_iota(jnp.int32, (tm, tm), 0)
    col = lax.broadcasted_iota(jnp.int32, (tm, tm), 1)
    low = col <= row
    l_full = jnp.where(low, 1.0, 0.0).astype(BF16)
    sh = GDN_CHUNK.bit_length() - 1
    same_chunk = (row >> sh) == (col >> sh)
    l_blk = jnp.where(low, jnp.where(same_chunk, 1.0, 0.0), 0.0).astype(BF16)
    cs_blk = _dot_exact_lhs(l_blk, vals)
    cs_full = _dot_exact_lhs(l_full, vals) + carry_ref[0:1, :]
    sm_ref[0] = jnp.where(lane < nh, vals, jnp.where(lane < 2 * nh, cs_blk, jnp.where(lane < 3 * nh, cs_full, 0.0)))
    carry_ref[...] = jnp.broadcast_to(cs_full[tm - 1:tm, :], carry_ref.shape)


def _norm_small(x, g, w_small, bias, alog):
    b, s, d = x.shape
    tm = _tile(s, 512)
    return pl.pallas_call(
        functools.partial(_norm_small_kernel, tm=tm),
        out_shape=(jax.ShapeDtypeStruct((b, s, d), BF16), jax.ShapeDtypeStruct((b, s, LANES), F32)),
        grid=(b, s // tm),
        in_specs=[
            pl.BlockSpec((1, tm, d), lambda i, j: (i, j, 0)),
            pl.BlockSpec((1, d), lambda i, j: (0, 0)),
            pl.BlockSpec((LANES, d), lambda i, j: (0, 0)),
            pl.BlockSpec((1, LANES), lambda i, j: (0, 0)),
            pl.BlockSpec((1, LANES), lambda i, j: (0, 0)),
        ],
        out_specs=(
            pl.BlockSpec((1, tm, d), lambda i, j: (i, j, 0)),
            pl.BlockSpec((1, tm, LANES), lambda i, j: (i, j, 0)),
        ),
        scratch_shapes=[pltpu.VMEM((8, LANES), F32)],
        compiler_params=_cparams(("arbitrary", "arbitrary")),
        name="norm_small",
    )(x, g, w_small, bias, alog)


def _out_norm_kernel(y_ref, w_ref, x_ref, g_ref, x1_ref, h_ref):
    x1 = x_ref[...] + jnp.dot(y_ref[...], w_ref[...], preferred_element_type=F32)
    x1_ref[...] = x1
    h_ref[...] = (x1 * lax.rsqrt(jnp.mean(x1 * x1, axis=-1, keepdims=True) + EPS) * g_ref[...]).astype(h_ref.dtype)


def _out_norm(y, w_bf16, x2d, g):
    t, d = x2d.shape
    k = y.shape[1]
    tm = _tile(t, 512)
    row = lambda width: pl.BlockSpec((tm, width), lambda i: (i, 0))
    return pl.pallas_call(
        _out_norm_kernel,
        out_shape=(jax.ShapeDtypeStruct((t, d), F32), jax.ShapeDtypeStruct((t, d), BF16)),
        grid=(t // tm,),
        in_specs=[row(k), pl.BlockSpec((k, d), lambda i: (0, 0)), row(d), pl.BlockSpec((1, d), lambda i: (0, 0))],
        out_specs=(row(d), row(d)),
        compiler_params=_cparams(("arbitrary",)),
        name="out_norm",
    )(y, w_bf16, x2d, g)


def _mm_kernel(a_ref, w_ref, *rest, act, has_res, w_rows_are_outputs):
    if has_res:
        r_ref, o_ref = rest
    else:
        (o_ref,) = rest
    contract = (((1,), (1,)), ((), ())) if w_rows_are_outputs else (((1,), (0,)), ((), ()))
    acc = lax.dot_general(a_ref[...], w_ref[...].astype(BF16), contract, preferred_element_type=F32)
    if act == "sigmoid":
        acc = jax.nn.sigmoid(acc)
    elif act == "relu2":
        r = jnp.maximum(acc, 0.0)
        acc = r * r
    if has_res:
        acc = acc + r_ref[...]
    o_ref[...] = acc.astype(o_ref.dtype)


def _matmul(a, w, *, out_dtype, tm, tn, act=None, residual=None, w_rows_are_outputs=False,
            w_row_ranges=None, name="matmul"):
    m, k = a.shape
    if w_row_ranges is not None:
        assert w_rows_are_outputs
        n = sum(r for _, r in w_row_ranges)
    else:
        n = w.shape[0] if w_rows_are_outputs else w.shape[1]
    tm = _tile(m, tm)
    tn = _tile(n, tn)
    if w_row_ranges is not None:
        assert all(r % tn == 0 and f % 8 == 0 for f, r in w_row_ranges)

        def w_rows(i, j):
            start, first_blk = jnp.int32(0), 0
            for f, r in w_row_ranges:
                start = jnp.where(j >= first_blk, f + (j - first_blk) * tn, start)
                first_blk += r // tn
            return pl.multiple_of(start, 8), 0

        w_spec = pl.BlockSpec((pl.Element(tn), pl.Element(k)), w_rows)
    elif w_rows_are_outputs:
        w_spec = pl.BlockSpec((tn, k), lambda i, j: (j, 0))
    else:
        w_spec = pl.BlockSpec((k, tn), lambda i, j: (0, j))
    in_specs = [pl.BlockSpec((tm, k), lambda i, j: (i, 0)), w_spec]
    args = [a, w]
    if residual is not None:
        in_specs.append(pl.BlockSpec((tm, tn), lambda i, j: (i, j)))
        args.append(residual)
    return pl.pallas_call(
        functools.partial(_mm_kernel, act=act, has_res=residual is not None,
                          w_rows_are_outputs=w_rows_are_outputs),
        out_shape=jax.ShapeDtypeStruct((m, n), out_dtype),
        grid=(m // tm, n // tn),
        in_specs=in_specs,
        out_specs=pl.BlockSpec((tm, tn), lambda i, j: (i, j)),
        compiler_params=_cparams(("arbitrary", "arbitrary")),
        name=name,
    )(*args)


def _gdn_kernel(q_ref, k_ref, v_ref, z_ref, sm_ref, cwq_ref, cwk_ref, cwv_ref, gn_ref, o_ref,
                state_ref, tail_ref, ext_ref, qs, ks, vs, *, ts, nc, hb, nb):
    hg = pl.program_id(0)
    si = pl.program_id(1)
    c_sz = GDN_CHUNK

    @pl.when(si == 0)
    def _():
        state_ref[...] = jnp.zeros_like(state_ref)
        tail_ref[...] = jnp.zeros_like(tail_ref)

    def conv_silu(x_ref, w_ref, bi, idx, c):
        r0 = c * c_sz
        x = x_ref[bi, r0:r0 + c_sz, :]
        ext_ref[idx, 0:8, :] = tail_ref[bi, idx] if c == 0 else x_ref[bi, r0 - 8:r0, :]
        ext_ref[idx, 8:8 + c_sz, :] = x
        w = w_ref[...]
        y = (ext_ref[idx, 5:5 + c_sz, :] * w[0:1, :] + ext_ref[idx, 6:6 + c_sz, :] * w[1:2, :]
             + ext_ref[idx, 7:7 + c_sz, :] * w[2:3, :] + x * w[3:4, :])
        if c == nc - 1:
            tail_ref[bi, idx] = x[c_sz - 8:c_sz, :]
        return y * jax.nn.sigmoid(y)

    def prepare(c):
        rows = slice(c * c_sz, (c + 1) * c_sz)
        for bi in range(nb):
            q = conv_silu(q_ref, cwq_ref, bi, 0, c)
            k = conv_silu(k_ref, cwk_ref, bi, 1, c)
            vs[bi, rows, :] = conv_silu(v_ref, cwv_ref, bi, 2, c)
            for hh in range(hb):
                cs = slice(hh * LANES, (hh + 1) * LANES)
                qh = q[:, cs]
                kh = k[:, cs]
                qs[bi, rows, cs] = qh * (lax.rsqrt(jnp.sum(qh * qh, axis=-1, keepdims=True) + EPS)
                                         * (GDN_DK ** -0.5))
                ks[bi, rows, cs] = kh * lax.rsqrt(jnp.sum(kh * kh, axis=-1, keepdims=True) + EPS)

    ri = lax.broadcasted_iota(jnp.int32, (c_sz, c_sz), 0)
    ci = lax.broadcasted_iota(jnp.int32, (c_sz, c_sz), 1)
    strict = ri > ci
    incl = ri >= ci
    eye = jnp.where(ri == ci, 1.0, 0.0)
    n_lvl = c_sz.bit_length() - 1
    lvl_masks = []
    for l in range(n_lvl):
        same = (ri >> (l + 1)) == (ci >> (l + 1))
        lvl_masks.append(jnp.where(
            same, jnp.where(((ri >> l) & 1) == 1, jnp.where(((ci >> l) & 1) == 0, 1.0, 0.0), 0.0), 0.0))

    def advance(c):
        rows = slice(c * c_sz, (c + 1) * c_sz)
        chains = [(bi, hh) for bi in range(nb) for hh in range(hb)]
        mlows, rhss, lhs2s, qds, cds = [], [], [], [], []
        for bi, hh in chains:
            cs = slice(hh * LANES, (hh + 1) * LANES)
            hd = hg * hb + hh
            smc = sm_ref[bi, rows, :]
            beta = _lane_col(smc, hd)
            gam = _lane_col(smc, hd + GDN_HEADS)
            kc = ks[bi, rows, cs]
            qc = qs[bi, rows, cs]
            egam = jnp.exp(gam)
            g_last = gam[c_sz - 1:c_sz, :]
            gcol = jnp.broadcast_to(gam, (c_sz, c_sz))
            diff = gcol - gcol.T
            e = jnp.exp(jnp.where(incl, diff, 0.0))
            mlows.append(beta * _dot_nt(kc, kc) * jnp.where(strict, e, 0.0))
            qk = _dot_nt(qc, kc) * jnp.where(incl, e, 0.0)
            kd_t = (kc * jnp.exp(g_last - gam)).T
            lhs2s.append(jnp.concatenate([qk, kd_t], axis=0).astype(BF16))
            rhss.append(jnp.concatenate([kc * (beta * egam), vs[bi, rows, cs] * beta], axis=1).astype(BF16))
            qds.append(qc * egam)
            cds.append(jnp.exp(g_last))
        xs = [eye - lvl_masks[0] * m for m in mlows]
        mlows_bf = [m.astype(BF16) for m in mlows]
        for l in range(1, n_lvl):
            mask_bf = lvl_masks[l].astype(BF16)
            tl = [_dot(mask_bf * m, x) for m, x in zip(mlows_bf, xs)]
            xs = [x - _dot(x, t) for x, t in zip(xs, tl)]
        rs = [eye - x - _dot3(m, x) for m, x in zip(mlows, xs)]
        xs = [x + _dot(x, r) for x, r in zip(xs, rs)]
        wus = [_dot2(x, r) for x, r in zip(xs, rhss)]
        for n, (bi, hh) in enumerate(chains):
            cs = slice(hh * LANES, (hh + 1) * LANES)
            state = state_ref[n]
            wu = wus[n]
            a = _dot(jnp.concatenate([wu[:, :GDN_DK], qds[n]], axis=0), state)
            u = wu[:, GDN_DK:] - a[:c_sz]
            b2 = jnp.dot(lhs2s[n], u.astype(BF16), preferred_element_type=F32)
            o = a[c_sz:] + b2[:c_sz]
            state_ref[n] = state * cds[n] + b2[c_sz:]
            z = z_ref[bi, rows, cs]
            on = o * lax.rsqrt(jnp.mean(o * o, axis=-1, keepdims=True) + EPS) * gn_ref[...]
            o_ref[bi, rows, cs] = (on * (z * jax.nn.sigmoid(z))).astype(o_ref.dtype)

    for c in range(nc):
        prepare(c)
        advance(c)


def _gdn(qkvz, sm, conv_w, gn):
    b, s, _ = qkvz.shape
    nh = GDN_HEADS
    hb = GDN_HEADS_PER_STEP
    ng = nh // hb
    wb = hb * LANES
    ts = _tile(s, GDN_ROWS_PER_STEP)
    nc = ts // GDN_CHUNK
    blk = lambda off: pl.BlockSpec((b, ts, wb), lambda h, j: (0, j, off + h))
    cw = lambda off: pl.BlockSpec((GDN_CONV, wb), lambda h, j: (0, off + h))
    return pl.pallas_call(
        functools.partial(_gdn_kernel, ts=ts, nc=nc, hb=hb, nb=b),
        out_shape=jax.ShapeDtypeStruct((b, s, nh * GDN_DV), BF16),
        grid=(ng, s // ts),
        in_specs=[
            blk(0), blk(ng), blk(2 * ng), blk(3 * ng),
            pl.BlockSpec((b, ts, LANES), lambda h, j: (0, j, 0)),
            cw(0), cw(ng), cw(2 * ng),
            pl.BlockSpec((1, GDN_DV), lambda h, j: (0, 0)),
        ],
        out_specs=pl.BlockSpec((b, ts, wb), lambda h, j: (0, j, h)),
        scratch_shapes=[
            pltpu.VMEM((b * hb, GDN_DK, GDN_DV), F32),
            pltpu.VMEM((b, 3, 8, wb), F32),
            pltpu.VMEM((3, GDN_CHUNK + 8, wb), F32),
            pltpu.VMEM((b, ts, wb), F32),
            pltpu.VMEM((b, ts, wb), F32),
            pltpu.VMEM((b, ts, wb), F32),
        ],
        compiler_params=_cparams(("arbitrary", "arbitrary")),
        name="gdn",
    )(qkvz, qkvz, qkvz, qkvz, sm, conv_w, conv_w, conv_w, gn)


def _fox_kernel(q_ref, k_ref, v_ref, smq_ref, smk_ref, gq_ref, gk_ref, o_ref,
                kaug_ref, vaug_ref, m_s, acc_s, sa_ref, sb_ref, *, tq, tk, nk):
    hd = pl.program_id(1)
    qi = pl.program_id(2)
    cidx = hd + 2 * GDN_HEADS
    n_sub = tq // tk
    n_lt = tk // LANES

    @pl.when(qi == 0)
    def _():
        lane = lax.broadcasted_iota(jnp.int32, (tk, LANES), 1)

        def build(j, carry):
            rows = pl.ds(pl.multiple_of(j * tk, tk), tk)
            kk = k_ref[0, rows, :].astype(F32)
            kn = kk * lax.rsqrt(jnp.mean(kk * kk, axis=-1, keepdims=True) + EPS) * gk_ref[...]
            hi, mid, lo = _split3(_lane_col(smk_ref[0, rows, :], cidx) * LOG2E)
            aug = jnp.where(lane < 3, 1.0, jnp.where(lane == 3, -hi, jnp.where(lane == 4, -mid, jnp.where(lane == 5, -lo, 0.0))))
            kaug_ref[rows, 0:FOX_DH] = kn.astype(BF16)
            kaug_ref[rows, FOX_DH:2 * FOX_DH] = aug.astype(BF16)
            vaug_ref[rows, 0:FOX_DH] = v_ref[0, rows, :]
            vaug_ref[rows, FOX_DH:2 * FOX_DH] = jnp.ones((tk, FOX_DH), BF16)
            return carry
        lax.fori_loop(0, nk, build, 0)

    lane = lax.broadcasted_iota(jnp.int32, (tq, LANES), 1)
    qq = q_ref[0].astype(F32)
    qn = qq * lax.rsqrt(jnp.mean(qq * qq, axis=-1, keepdims=True) + EPS) * gq_ref[...] * (FOX_DH ** -0.5 * LOG2E)
    hi, mid, lo = _split3(_lane_col(smq_ref[0], cidx) * LOG2E)
    aug = jnp.where(lane == 0, hi, jnp.where(lane == 1, mid, jnp.where(lane == 2, lo, jnp.where(lane < 6, 1.0, 0.0))))
    q_aug = jnp.concatenate([qn.astype(BF16), aug.astype(BF16)], axis=1)

    m_s[...] = jnp.full_like(m_s, NEG_BIG)
    acc_s[...] = jnp.zeros_like(acc_s)

    def scores(s_ref, j):
        rows = pl.ds(pl.multiple_of(j * tk, tk), tk)
        s_ref[...] = lax.dot_general(q_aug, kaug_ref[rows, :], (((1,), (1,)), ((), ())),
                                     preferred_element_type=F32)

    def accumulate(s_ref, j, diag):
        rows = pl.ds(pl.multiple_of(j * tk, tk), tk)
        v_blk = vaug_ref[rows, :]
        groups = ([(0, tq, False)] if diag is None
                  else [(r * tk, (r + 1) * tk, r == diag) for r in range(diag, n_sub)])
        for lo_row, hi_row, masked in groups:
            rs = slice(lo_row, hi_row)
            tiles = [s_ref[rs, c * LANES:(c + 1) * LANES] for c in range(n_lt)]
            if masked:
                ri = lax.broadcasted_iota(jnp.int32, (tk, LANES), 0)
                ci = lax.broadcasted_iota(jnp.int32, (tk, LANES), 1)
                tiles = [jnp.where(ri >= ci + c * LANES, t, NEG_BIG) for c, t in enumerate(tiles)]
            mx = tiles[0]
            for t in tiles[1:]:
                mx = jnp.maximum(mx, t)
            m_prev = m_s[rs, :]
            m_next = jnp.maximum(m_prev, jnp.max(mx, axis=1, keepdims=True))
            p = jnp.concatenate([jnp.exp2(t - m_next) for t in tiles], axis=1).astype(BF16)
            alpha = jnp.exp2(m_prev - m_next)
            acc_s[rs, :] = (jnp.concatenate([alpha, alpha], axis=1) * acc_s[rs, :]
                            + jnp.dot(p, v_blk, preferred_element_type=F32))
            m_s[rs, :] = m_next

    scores(sa_ref, 0)

    def body(i, carry):
        j = 2 * i
        scores(sb_ref, j + 1)
        accumulate(sa_ref, j, None)
        scores(sa_ref, j + 2)
        accumulate(sb_ref, j + 1, None)
        return carry

    n_below = n_sub * qi
    lax.fori_loop(0, n_below // 2, body, 0)
    bufs = (sa_ref, sb_ref)
    for e in range(n_sub):
        if e + 1 < n_sub:
            scores(bufs[(e + 1) % 2], n_below + e + 1)
        accumulate(bufs[e % 2], n_below + e, e)

    acc = acc_s[...]
    o_ref[0] = (acc[:, :FOX_DH] / acc[:, FOX_DH:]).astype(o_ref.dtype)


def _fox(qkv, sm, gq, gk):
    b, s, _ = qkv.shape
    nh = FOX_HEADS
    tq = _tile(s, FOX_TQ)
    tk = _tile(tq, FOX_TK)
    assert (tq // tk) % 2 == 0, "the score pipeline consumes key blocks in pairs"
    nk = s // tk
    return pl.pallas_call(
        functools.partial(_fox_kernel, tq=tq, tk=tk, nk=nk),
        out_shape=jax.ShapeDtypeStruct((b, s, nh * FOX_DH), BF16),
        grid=(b, nh, s // tq),
        in_specs=[
            pl.BlockSpec((1, tq, FOX_DH), lambda i, h, j: (i, j, h)),
            pl.BlockSpec((1, s, FOX_DH), lambda i, h, j: (i, 0, nh + h)),
            pl.BlockSpec((1, s, FOX_DH), lambda i, h, j: (i, 0, 2 * nh + h)),
            pl.BlockSpec((1, tq, LANES), lambda i, h, j: (i, j, 0)),
            pl.BlockSpec((1, s, LANES), lambda i, h, j: (i, 0, 0)),
            pl.BlockSpec((1, FOX_DH), lambda i, h, j: (0, 0)),
            pl.BlockSpec((1, FOX_DH), lambda i, h, j: (0, 0)),
        ],
        out_specs=pl.BlockSpec((1, tq, FOX_DH), lambda i, h, j: (i, j, h)),
        scratch_shapes=[
            pltpu.VMEM((s, 2 * FOX_DH), BF16),
            pltpu.VMEM((s, 2 * FOX_DH), BF16),
            pltpu.VMEM((tq, LANES), F32),
            pltpu.VMEM((tq, 2 * FOX_DH), F32),
            pltpu.VMEM((tq, tk), F32),
            pltpu.VMEM((tq, tk), F32),
        ],
        compiler_params=_cparams(("arbitrary", "arbitrary", "arbitrary")),
        name="fox",
    )(qkv, qkv, qkv, sm, sm, gq, gk)


def _memkv_kernel(mem_ref, g_ref, w_ref, gk_ref, o_ref, *, n_k_tiles):
    j = pl.program_id(1)
    m = mem_ref[0]
    hn = (m * lax.rsqrt(jnp.mean(m * m, axis=-1, keepdims=True) + EPS) * g_ref[...]).astype(BF16)
    r = jnp.dot(hn, w_ref[...].astype(BF16), preferred_element_type=F32)
    parts = []
    for t in range(r.shape[1] // MEM_DH):
        rt = r[:, t * MEM_DH:(t + 1) * MEM_DH]
        parts.append(rt * lax.rsqrt(jnp.mean(rt * rt, axis=-1, keepdims=True) + EPS) * gk_ref[...])
    normed = jnp.concatenate(parts, axis=1)
    is_k = jnp.where(j < n_k_tiles, 1.0, 0.0)
    o_ref[0] = (is_k * normed + (1.0 - is_k) * r).astype(o_ref.dtype)


def _memkv(mem, g, w, gk):
    b, ml, d = mem.shape
    n = w.shape[1]
    tn = 2 * MEM_DH
    return pl.pallas_call(
        functools.partial(_memkv_kernel, n_k_tiles=(n // 2) // tn),
        out_shape=jax.ShapeDtypeStruct((b, ml, n), BF16),
        grid=(b, n // tn),
        in_specs=[
            pl.BlockSpec((1, ml, d), lambda i, j: (i, 0, 0)),
            pl.BlockSpec((1, d), lambda i, j: (0, 0)),
            pl.BlockSpec((d, tn), lambda i, j: (0, j)),
            pl.BlockSpec((1, MEM_DH), lambda i, j: (0, 0)),
        ],
        out_specs=pl.BlockSpec((1, ml, tn), lambda i, j: (i, 0, j)),
        compiler_params=_cparams(("arbitrary", "arbitrary")),
        name="memkv",
    )(mem, g, w, gk)


def _mem_kernel(q_ref, k_ref, v_ref, gq_ref, o_ref):
    for hd in range(MEM_HEADS):
        cs = slice(hd * MEM_DH, (hd + 1) * MEM_DH)
        qq = q_ref[0, :, cs].astype(F32)
        qn = qq * lax.rsqrt(jnp.mean(qq * qq, axis=-1, keepdims=True) + EPS) * gq_ref[...] * (MEM_DH ** -0.5)
        s = _dot_nt(qn, k_ref[0, :, cs])
        p = jnp.exp(s - jnp.max(s, axis=1, keepdims=True))
        p = p / jnp.sum(p, axis=1, keepdims=True)
        o_ref[0, :, cs] = jnp.dot(p.astype(BF16), v_ref[0, :, cs], preferred_element_type=F32).astype(o_ref.dtype)


def _mem_attn(qsrc, q_col_block, kv, gq):
    b, s, _ = qsrc.shape
    ml = kv.shape[1]
    wq = MEM_HEADS * MEM_DH
    tq = _tile(s, 512)
    return pl.pallas_call(
        _mem_kernel,
        out_shape=jax.ShapeDtypeStruct((b, s, wq), BF16),
        grid=(b, s // tq),
        in_specs=[
            pl.BlockSpec((1, tq, wq), lambda i, j: (i, j, q_col_block)),
            pl.BlockSpec((1, ml, wq), lambda i, j: (i, 0, 0)),
            pl.BlockSpec((1, ml, wq), lambda i, j: (i, 0, 1)),
            pl.BlockSpec((1, MEM_DH), lambda i, j: (0, 0)),
        ],
        out_specs=pl.BlockSpec((1, tq, wq), lambda i, j: (i, j, 0)),
        compiler_params=_cparams(("arbitrary", "arbitrary")),
        name="mem_attn",
    )(qsrc, kv, kv, gq)


def _merge_kernel(oa_ref, ob_ref, om_ref, wa_ref, wb_ref, wm_ref, ga_ref, gb_ref, gm_ref, y_ref):
    d = lambda o_ref, w_ref: jnp.dot(o_ref[...], w_ref[...].astype(BF16), preferred_element_type=F32)
    y = (ga_ref[...].astype(F32) * d(oa_ref, wa_ref)
         + gb_ref[...].astype(F32) * d(ob_ref, wb_ref)
         + gm_ref[...].astype(F32) * d(om_ref, wm_ref))
    y_ref[...] = y.astype(y_ref.dtype)


def _merge(oa, ob, om, wa, wb, wm, gates):
    t, ka = oa.shape
    dm = wa.shape[1]
    tm = _tile(t, 2048)
    tn = _tile(dm, 256)
    nb = dm // tn
    a_spec = lambda kk: pl.BlockSpec((tm, kk), lambda i, j: (i, 0))
    w_spec = lambda kk: pl.BlockSpec((kk, tn), lambda i, j: (0, j))
    g_spec = lambda off: pl.BlockSpec((tm, tn), lambda i, j: (i, off * nb + j))
    return pl.pallas_call(
        _merge_kernel,
        out_shape=jax.ShapeDtypeStruct((t, dm), BF16),
        grid=(t // tm, nb),
        in_specs=[a_spec(ka), a_spec(ob.shape[1]), a_spec(om.shape[1]),
                  w_spec(ka), w_spec(ob.shape[1]), w_spec(om.shape[1]),
                  g_spec(0), g_spec(1), g_spec(2)],
        out_specs=pl.BlockSpec((tm, tn), lambda i, j: (i, j)),
        compiler_params=_cparams(("arbitrary", "arbitrary")),
        name="merge",
    )(oa, ob, om, wa, wb, wm, gates, gates, gates)


def _layer(x, mem, g_mix, w_in, conv_w, a_log, dt_bias, gdn_norm_g, fox_b_f, fox_q_norm, fox_k_norm,
           g_mem, w_mem_kv, mem_q_norm, mem_k_norm, w_up_gdn, w_up_fox, w_up_mem, w_out, g_mlp, w_ff1, w_ff2):
    b, s, d = x.shape
    t = b * s
    gdn_qk = GDN_HEADS * GDN_DK
    gdn_v = GDN_HEADS * GDN_DV
    fox_w = FOX_HEADS * FOX_DH
    mem_w = MEM_HEADS * MEM_DH
    o_z = 2 * gdn_qk + gdn_v
    o_beta = o_z + gdn_v
    o_dec = o_beta + GDN_HEADS
    o_fq = o_dec + GDN_HEADS
    o_ff = o_fq + 3 * fox_w
    o_mq = o_ff + FOX_HEADS
    o_gate = o_mq + mem_w

    wt = w_in.T
    n_small = 2 * GDN_HEADS + FOX_HEADS
    w_small = jnp.concatenate([wt[o_beta:o_fq], wt[o_ff:o_mq],
                               jnp.zeros((LANES - n_small, d), F32)], axis=0)
    zpad = jnp.zeros((LANES - n_small,), F32)
    bias = jnp.concatenate([jnp.zeros((GDN_HEADS,), F32), dt_bias.astype(F32), fox_b_f.astype(F32), zpad])[None, :]
    alog = jnp.concatenate([jnp.zeros((GDN_HEADS,), F32), a_log.astype(F32), jnp.zeros((FOX_HEADS,), F32), zpad])[None, :]

    h, sm = _norm_small(x, g_mix[None, :], w_small, bias, alog)
    h2d = h.reshape(t, d)

    proj = functools.partial(_matmul, h2d, wt, tm=2048, tn=512, w_rows_are_outputs=True)
    qkvz = proj(out_dtype=F32, w_row_ranges=[(0, o_beta)], name="proj_gdn").reshape(b, s, -1)
    att = proj(out_dtype=BF16, w_row_ranges=[(o_fq, 3 * fox_w), (o_mq, mem_w)],
               name="proj_att").reshape(b, s, -1)
    gates = proj(out_dtype=BF16, act="sigmoid", w_row_ranges=[(o_gate, N_BRANCH * d)], name="proj_gate")

    o_a = _gdn(qkvz, sm, conv_w, gdn_norm_g[None, :])
    o_b = _fox(att, sm, fox_q_norm[None, :], fox_k_norm[None, :])
    kv_m = _memkv(mem, g_mem[None, :], w_mem_kv, mem_k_norm[None, :])
    o_m = _mem_attn(att, (3 * fox_w) // mem_w, kv_m, mem_q_norm[None, :])

    y = _merge(o_a.reshape(t, -1), o_b.reshape(t, -1), o_m.reshape(t, -1), w_up_gdn, w_up_fox, w_up_mem, gates)
    x1, h2 = _out_norm(y, w_out.astype(BF16), x.reshape(t, d), g_mlp[None, :])
    u = _matmul(h2, w_ff1, out_dtype=BF16, tm=2048, tn=512, act="relu2", name="ff1")
    out = _matmul(u, w_ff2.astype(BF16), out_dtype=F32, tm=1024, tn=256, residual=x1, name="ff2")
    return out.reshape(b, s, d)


def kernel(x, mem, g_mix, w_in, conv_w, a_log, dt_bias, gdn_norm_g, fox_b_f, fox_q_norm, fox_k_norm, g_mem, w_mem_kv, mem_q_norm, mem_k_norm, w_up_gdn, w_up_fox, w_up_mem, w_out, g_mlp, w_ff1, w_ff2):
    depth = w_in.shape[0]
    for l in range(depth):
        x = _layer(x, mem, g_mix[l], w_in[l], conv_w[l], a_log[l], dt_bias[l], gdn_norm_g[l], fox_b_f[l],
                   fox_q_norm[l], fox_k_norm[l], g_mem[l], w_mem_kv[l], mem_q_norm[l], mem_k_norm[l],
                   w_up_gdn[l], w_up_fox[l], w_up_mem[l], w_out[l], g_mlp[l], w_ff1[l], w_ff2[l])
    return x
```

```python
import functools
import math

import jax
import jax.numpy as jnp
from jax import lax
from jax.experimental import pallas as pl
from jax.experimental.pallas import tpu as pltpu

F32 = jnp.float32
BF16 = jnp.bfloat16
EPS = 1e-6

GDN_HEADS = 8
GDN_DK = 128
GDN_DV = 128
GDN_CONV = 4
GDN_CHUNK = 128
GDN_HEADS_PER_STEP = 8
GDN_ROWS_PER_STEP = 256
FOX_HEADS = 8
FOX_DH = 128
FOX_TQ = 1024
FOX_TK = 512
MEM_HEADS = 4
MEM_DH = 256
N_BRANCH = 3
LANES = 128
NEG_BIG = -1e30
LOG2E = math.log2(math.e)

VMEM_LIMIT = 56 * 1024 * 1024


def _tile(n, pref):
    return pref if n % pref == 0 else n


def _cparams(sem):
    return pltpu.CompilerParams(dimension_semantics=sem, vmem_limit_bytes=VMEM_LIMIT)


def _dot(a, b):
    return jnp.dot(a.astype(BF16), b.astype(BF16), preferred_element_type=F32)


def _dot_nt(a, b):
    return lax.dot_general(a.astype(BF16), b.astype(BF16), (((1,), (1,)), ((), ())),
                           preferred_element_type=F32)


def _split2(a):
    hi = a.astype(BF16)
    lo = (a - hi.astype(F32)).astype(BF16)
    return hi, lo


def _split3(a):
    hi = a.astype(BF16).astype(F32)
    r = a - hi
    mid = r.astype(BF16).astype(F32)
    lo = (r - mid).astype(BF16).astype(F32)
    return hi, mid, lo


def _dot3_nt(a, b):
    a_hi, a_lo = _split2(a)
    b_hi, b_lo = _split2(b)
    n = b.shape[0]
    d = lambda p, q: lax.dot_general(p, q, (((1,), (1,)), ((), ())), preferred_element_type=F32)
    both = d(a_hi, jnp.concatenate([b_hi, b_lo], axis=0))
    return both[:, :n] + both[:, n:] + d(a_lo, b_hi)


def _dot_exact_lhs(l_bf16, v):
    hi, mid, lo = _split3(v)
    n = v.shape[1]
    d = functools.partial(jnp.dot, preferred_element_type=F32)
    both = d(l_bf16, jnp.concatenate([hi, mid], axis=1).astype(BF16))
    return both[:, :n] + both[:, n:] + d(l_bf16, lo.astype(BF16))


def _lane_col(a, idx):
    lane = lax.broadcasted_iota(jnp.int32, a.shape, 1)
    return jnp.sum(jnp.where(lane == idx, a, 0.0), axis=1, keepdims=True)


def _softplus(x):
    return jnp.maximum(x, 0.0) + jnp.log1p(jnp.exp(-jnp.abs(x)))


def _norm_small_kernel(x_ref, g_ref, ws_ref, bias_ref, alog_ref, h_ref, sm_ref, carry_ref, *, tm):
    s = pl.program_id(1)

    @pl.when(s == 0)
    def _():
        carry_ref[...] = jnp.zeros_like(carry_ref)

    x = x_ref[0]
    h = x * lax.rsqrt(jnp.mean(x * x, axis=-1, keepdims=True) + EPS) * g_ref[...]
    h_ref[0] = h.astype(BF16)

    pre = _dot3_nt(h, ws_ref[...]) + bias_ref[...]
    lane = lax.broadcasted_iota(jnp.int32, pre.shape, 1)
    nh = GDN_HEADS
    beta = jax.nn.sigmoid(pre)
    gdec = -jnp.exp(alog_ref[...]) * _softplus(pre)
    logf = -_softplus(-pre)
    vals = jnp.where(lane < nh, beta, jnp.where(lane < 2 * nh, gdec, jnp.where(lane < 3 * nh, logf, 0.0)))

    row = lax.broadcasted_iota(jnp.int32, (tm, tm), 0)
    col = lax.broadcasted_iota(jnp.int32, (tm, tm), 1)
    low = col <= row
    l_full = jnp.where(low, 1.0, 0.0).astype(BF16)
    sh = GDN_CHUNK.bit_length() - 1
    same_chunk = (row >> sh) == (col >> sh)
    l_blk = jnp.where(low, jnp.where(same_chunk, 1.0, 0.0), 0.0).astype(BF16)
    cs_blk = _dot_exact_lhs(l_blk, vals)
    cs_full = _dot_exact_lhs(l_full, vals) + carry_ref[0:1, :]
    sm_ref[0] = jnp.where(lane < nh, vals, jnp.where(lane < 2 * nh, cs_blk, jnp.where(lane < 3 * nh, cs_full, 0.0)))
    carry_ref[...] = jnp.broadcast_to(cs_full[tm - 1:tm, :], carry_ref.shape)


def _norm_small(x, g, w_small, bias, alog):
    b, s, d = x.shape
    tm = _tile(s, 512)
    return pl.pallas_call(
        functools.partial(_norm_small_kernel, tm=tm),
        out_shape=(jax.ShapeDtypeStruct((b, s, d), BF16), jax.ShapeDtypeStruct((b, s, LANES), F32)),
        grid=(b, s // tm),
        in_specs=[
            pl.BlockSpec((1, tm, d), lambda i, j: (i, j, 0)),
            pl.BlockSpec((1, d), lambda i, j: (0, 0)),
            pl.BlockSpec((LANES, d), lambda i, j: (0, 0)),
            pl.BlockSpec((1, LANES), lambda i, j: (0, 0)),
            pl.BlockSpec((1, LANES), lambda i, j: (0, 0)),
        ],
        out_specs=(
            pl.BlockSpec((1, tm, d), lambda i, j: (i, j, 0)),
            pl.BlockSpec((1, tm, LANES), lambda i, j: (i, j, 0)),
        ),
        scratch_shapes=[pltpu.VMEM((8, LANES), F32)],
        compiler_params=_cparams(("arbitrary", "arbitrary")),
        name="norm_small",
    )(x, g, w_small, bias, alog)


def _out_norm_kernel(y_ref, w_ref, x_ref, g_ref, x1_ref, h_ref):
    x1 = x_ref[...] + jnp.dot(y_ref[...], w_ref[...], preferred_element_type=F32)
    x1_ref[...] = x1
    h_ref[...] = (x1 * lax.rsqrt(jnp.mean(x1 * x1, axis=-1, keepdims=True) + EPS) * g_ref[...]).astype(h_ref.dtype)


def _out_norm(y, w_bf16, x2d, g):
    t, d = x2d.shape
    k = y.shape[1]
    tm = _tile(t, 512)
    row = lambda width: pl.BlockSpec((tm, width), lambda i: (i, 0))
    return pl.pallas_call(
        _out_norm_kernel,
        out_shape=(jax.ShapeDtypeStruct((t, d), F32), jax.ShapeDtypeStruct((t, d), BF16)),
        grid=(t // tm,),
        in_specs=[row(k), pl.BlockSpec((k, d), lambda i: (0, 0)), row(d), pl.BlockSpec((1, d), lambda i: (0, 0))],
        out_specs=(row(d), row(d)),
        compiler_params=_cparams(("arbitrary",)),
        name="out_norm",
    )(y, w_bf16, x2d, g)


def _mm_kernel(a_ref, w_ref, *rest, act, has_res, w_rows_are_outputs):
    if has_res:
        r_ref, o_ref = rest
    else:
        (o_ref,) = rest
    contract = (((1,), (1,)), ((), ())) if w_rows_are_outputs else (((1,), (0,)), ((), ()))
    acc = lax.dot_general(a_ref[...], w_ref[...].astype(BF16), contract, preferred_element_type=F32)
    if act == "sigmoid":
        acc = 0.5 * jnp.tanh(0.5 * acc) + 0.5
    elif act == "relu2":
        r = jnp.maximum(acc, 0.0)
        acc = r * r
    if has_res:
        acc = acc + r_ref[...]
    o_ref[...] = acc.astype(o_ref.dtype)


def _matmul(a, w, *, out_dtype, tm, tn, act=None, residual=None, w_rows_are_outputs=False,
            w_row_ranges=None, name="matmul"):
    m, k = a.shape
    if w_row_ranges is not None:
        assert w_rows_are_outputs
        n = sum(r for _, r in w_row_ranges)
    else:
        n = w.shape[0] if w_rows_are_outputs else w.shape[1]
    tm = _tile(m, tm)
    tn = _tile(n, tn)
    if w_row_ranges is not None:
        assert all(r % tn == 0 and f % 8 == 0 for f, r in w_row_ranges)

        def w_rows(i, j):
            start, first_blk = jnp.int32(0), 0
            for f, r in w_row_ranges:
                start = jnp.where(j >= first_blk, f + (j - first_blk) * tn, start)
                first_blk += r // tn
            return pl.multiple_of(start, 8), 0

        w_spec = pl.BlockSpec((pl.Element(tn), pl.Element(k)), w_rows)
    elif w_rows_are_outputs:
        w_spec = pl.BlockSpec((tn, k), lambda i, j: (j, 0))
    else:
        w_spec = pl.BlockSpec((k, tn), lambda i, j: (0, j))
    in_specs = [pl.BlockSpec((tm, k), lambda i, j: (i, 0)), w_spec]
    args = [a, w]
    if residual is not None:
        in_specs.append(pl.BlockSpec((tm, tn), lambda i, j: (i, j)))
        args.append(residual)
    return pl.pallas_call(
        functools.partial(_mm_kernel, act=act, has_res=residual is not None,
                          w_rows_are_outputs=w_rows_are_outputs),
        out_shape=jax.ShapeDtypeStruct((m, n), out_dtype),
        grid=(m // tm, n // tn),
        in_specs=in_specs,
        out_specs=pl.BlockSpec((tm, tn), lambda i, j: (i, j)),
        compiler_params=_cparams(("arbitrary", "arbitrary")),
        name=name,
    )(*args)


def _gdn_kernel(q_ref, k_ref, v_ref, z_ref, sm_ref, cwq_ref, cwk_ref, cwv_ref, gn_ref, o_ref,
                state_ref, tail_ref, ext_ref, qs, ks, vs, *, ts, nc, hb, nb):
    hg = pl.program_id(0)
    si = pl.program_id(1)
    c_sz = GDN_CHUNK

    @pl.when(si == 0)
    def _():
        state_ref[...] = jnp.zeros_like(state_ref)
        tail_ref[...] = jnp.zeros_like(tail_ref)

    def conv_silu(x_ref, w_ref, bi, idx, c):
        r0 = c * c_sz
        x = x_ref[bi, r0:r0 + c_sz, :]
        w = w_ref[...]
        y = x * w[0:1, :]
        for i in range(1, GDN_CONV):
            y = pltpu.roll(y, 1, 0) + x * w[i:i + 1, :]
        ext_ref[idx, 0:8, :] = tail_ref[bi, idx] if c == 0 else x_ref[bi, r0 - 8:r0, :]
        ext_ref[idx, 8:16, :] = x[0:8, :]
        first = 8 - (GDN_CONV - 1)
        head = ext_ref[idx, first:first + 8, :] * w[0:1, :]
        for i in range(1, GDN_CONV):
            head = head + ext_ref[idx, first + i:first + i + 8, :] * w[i:i + 1, :]
        y = jnp.concatenate([head, y[8:, :]], axis=0)
        if c == nc - 1:
            tail_ref[bi, idx] = x[c_sz - 8:c_sz, :]
        return y * jax.nn.sigmoid(y)

    def prepare(c):
        rows = slice(c * c_sz, (c + 1) * c_sz)
        for bi in range(nb):
            q = conv_silu(q_ref, cwq_ref, bi, 0, c)
            k = conv_silu(k_ref, cwk_ref, bi, 1, c)
            vs[bi, rows, :] = conv_silu(v_ref, cwv_ref, bi, 2, c)
            for hh in range(hb):
                cs = slice(hh * LANES, (hh + 1) * LANES)
                qh = q[:, cs]
                kh = k[:, cs]
                qs[bi, rows, cs] = qh * (lax.rsqrt(jnp.sum(qh * qh, axis=-1, keepdims=True) + EPS)
                                         * (GDN_DK ** -0.5))
                ks[bi, rows, cs] = kh * lax.rsqrt(jnp.sum(kh * kh, axis=-1, keepdims=True) + EPS)

    ri = lax.broadcasted_iota(jnp.int32, (c_sz, c_sz), 0)
    ci = lax.broadcasted_iota(jnp.int32, (c_sz, c_sz), 1)
    strict = ri > ci
    incl = ri >= ci
    eye = jnp.where(ri == ci, 1.0, 0.0)
    n_lvl = c_sz.bit_length() - 1
    lvl_masks = []
    for l in range(n_lvl):
        same = (ri >> (l + 1)) == (ci >> (l + 1))
        lvl_masks.append(jnp.where(
            same, jnp.where(((ri >> l) & 1) == 1, jnp.where(((ci >> l) & 1) == 0, 1.0, 0.0), 0.0), 0.0))

    def advance(c):
        rows = slice(c * c_sz, (c + 1) * c_sz)
        chains = [(bi, hh) for bi in range(nb) for hh in range(hb)]
        mlows, rhss, lhs2s, qds, cds = [], [], [], [], []
        for bi, hh in chains:
            cs = slice(hh * LANES, (hh + 1) * LANES)
            hd = hg * hb + hh
            smc = sm_ref[bi, rows, :]
            beta = _lane_col(smc, hd)
            gam = _lane_col(smc, hd + GDN_HEADS)
            kc = ks[bi, rows, cs]
            qc = qs[bi, rows, cs]
            egam = jnp.exp(gam)
            g_last = gam[c_sz - 1:c_sz, :]
            gcol = jnp.broadcast_to(gam, (c_sz, c_sz))
            diff = gcol - gcol.T
            e = jnp.exp(jnp.where(incl, diff, 0.0))
            mlows.append(beta * _dot_nt(kc, kc) * jnp.where(strict, e, 0.0))
            qk = _dot_nt(qc, kc) * jnp.where(incl, e, 0.0)
            kd_t = (kc * jnp.exp(g_last - gam)).T
            lhs2s.append(jnp.concatenate([qk, kd_t], axis=0).astype(BF16))
            rhss.append(jnp.concatenate([kc * (beta * egam), vs[bi, rows, cs] * beta], axis=1).astype(BF16))
            qds.append(qc * egam)
            cds.append(jnp.exp(g_last))
        xs = [eye - lvl_masks[0] * m for m in mlows]
        mlows_bf = [m.astype(BF16) for m in mlows]
        for l in range(1, n_lvl):
            mask_bf = lvl_masks[l].astype(BF16)
            tl = [_dot(mask_bf * m, x) for m, x in zip(mlows_bf, xs)]
            xs = [x - _dot(x, t) for x, t in zip(xs, tl)]
        wus = [_dot(x, r) for x, r in zip(xs, rhss)]
        for n, (bi, hh) in enumerate(chains):
            cs = slice(hh * LANES, (hh + 1) * LANES)
            state = state_ref[n]
            wu = wus[n]
            a = _dot(jnp.concatenate([wu[:, :GDN_DK], qds[n]], axis=0), state)
            u = wu[:, GDN_DK:] - a[:c_sz]
            b2 = jnp.dot(lhs2s[n], u.astype(BF16), preferred_element_type=F32)
            o = a[c_sz:] + b2[:c_sz]
            state_ref[n] = state * cds[n] + b2[c_sz:]
            z = z_ref[bi, rows, cs]
            on = o * lax.rsqrt(jnp.mean(o * o, axis=-1, keepdims=True) + EPS) * gn_ref[...]
            o_ref[bi, rows, cs] = (on * (z * jax.nn.sigmoid(z))).astype(o_ref.dtype)

    for c in range(nc):
        prepare(c)
        advance(c)


def _gdn(qkvz, sm, conv_w, gn):
    b, s, _ = qkvz.shape
    nh = GDN_HEADS
    hb = GDN_HEADS_PER_STEP
    ng = nh // hb
    wb = hb * LANES
    ts = _tile(s, GDN_ROWS_PER_STEP)
    nc = ts // GDN_CHUNK
    blk = lambda off: pl.BlockSpec((b, ts, wb), lambda h, j: (0, j, off + h))
    cw = lambda off: pl.BlockSpec((GDN_CONV, wb), lambda h, j: (0, off + h))
    return pl.pallas_call(
        functools.partial(_gdn_kernel, ts=ts, nc=nc, hb=hb, nb=b),
        out_shape=jax.ShapeDtypeStruct((b, s, nh * GDN_DV), BF16),
        grid=(ng, s // ts),
        in_specs=[
            blk(0), blk(ng), blk(2 * ng), blk(3 * ng),
            pl.BlockSpec((b, ts, LANES), lambda h, j: (0, j, 0)),
            cw(0), cw(ng), cw(2 * ng),
            pl.BlockSpec((1, GDN_DV), lambda h, j: (0, 0)),
        ],
        out_specs=pl.BlockSpec((b, ts, wb), lambda h, j: (0, j, h)),
        scratch_shapes=[
            pltpu.VMEM((b * hb, GDN_DK, GDN_DV), F32),
            pltpu.VMEM((b, 3, 8, wb), F32),
            pltpu.VMEM((3, 16, wb), F32),
            pltpu.VMEM((b, ts, wb), F32),
            pltpu.VMEM((b, ts, wb), F32),
            pltpu.VMEM((b, ts, wb), F32),
        ],
        compiler_params=_cparams(("arbitrary", "arbitrary")),
        name="gdn",
    )(qkvz, qkvz, qkvz, qkvz, sm, conv_w, conv_w, conv_w, gn)


def _fox_kernel(q_ref, k_ref, v_ref, smq_ref, smk_ref, gq_ref, gk_ref, o_ref,
                kaug_ref, vaug_ref, m_s, acc_s, sa_ref, sb_ref, *, tq, tk, nk):
    hd = pl.program_id(1)
    qi = pl.program_id(2)
    cidx = hd + 2 * GDN_HEADS
    n_sub = tq // tk
    n_lt = tk // LANES

    @pl.when(qi == 0)
    def _():
        lane = lax.broadcasted_iota(jnp.int32, (tk, LANES), 1)

        def build(j, carry):
            rows = pl.ds(pl.multiple_of(j * tk, tk), tk)
            kk = k_ref[0, rows, :].astype(F32)
            kn = kk * lax.rsqrt(jnp.mean(kk * kk, axis=-1, keepdims=True) + EPS) * gk_ref[...]
            hi, mid, lo = _split3(_lane_col(smk_ref[0, rows, :], cidx) * LOG2E)
            aug = jnp.where(lane < 3, 1.0, jnp.where(lane == 3, -hi, jnp.where(lane == 4, -mid, jnp.where(lane == 5, -lo, 0.0))))
            kaug_ref[rows, 0:FOX_DH] = kn.astype(BF16)
            kaug_ref[rows, FOX_DH:2 * FOX_DH] = aug.astype(BF16)
            vaug_ref[rows, 0:FOX_DH] = v_ref[0, rows, :]
            vaug_ref[rows, FOX_DH:2 * FOX_DH] = jnp.ones((tk, FOX_DH), BF16)
            return carry
        lax.fori_loop(0, nk, build, 0)

    lane = lax.broadcasted_iota(jnp.int32, (tq, LANES), 1)
    qq = q_ref[0].astype(F32)
    qn = qq * lax.rsqrt(jnp.mean(qq * qq, axis=-1, keepdims=True) + EPS) * gq_ref[...] * (FOX_DH ** -0.5 * LOG2E)
    hi, mid, lo = _split3(_lane_col(smq_ref[0], cidx) * LOG2E)
    aug = jnp.where(lane == 0, hi, jnp.where(lane == 1, mid, jnp.where(lane == 2, lo, jnp.where(lane < 6, 1.0, 0.0))))
    q_aug = jnp.concatenate([qn.astype(BF16), aug.astype(BF16)], axis=1)

    m_s[...] = jnp.full_like(m_s, NEG_BIG)
    acc_s[...] = jnp.zeros_like(acc_s)

    def scores(s_ref, j):
        rows = pl.ds(pl.multiple_of(j * tk, tk), tk)
        s_ref[...] = lax.dot_general(q_aug, kaug_ref[rows, :], (((1,), (1,)), ((), ())),
                                     preferred_element_type=F32)

    def accumulate(s_ref, j, diag):
        rows = pl.ds(pl.multiple_of(j * tk, tk), tk)
        v_blk = vaug_ref[rows, :]
        groups = ([(0, tq, False)] if diag is None
                  else [(r * tk, (r + 1) * tk, r == diag) for r in range(diag, n_sub)])
        for lo_row, hi_row, masked in groups:
            rs = slice(lo_row, hi_row)
            tiles = [s_ref[rs, c * LANES:(c + 1) * LANES] for c in range(n_lt)]
            if masked:
                ri = lax.broadcasted_iota(jnp.int32, (tk, LANES), 0)
                ci = lax.broadcasted_iota(jnp.int32, (tk, LANES), 1)
                tiles = [jnp.where(ri >= ci + c * LANES, t, NEG_BIG) for c, t in enumerate(tiles)]
            mx = tiles[0]
            for t in tiles[1:]:
                mx = jnp.maximum(mx, t)
            m_prev = m_s[rs, :]
            m_next = jnp.maximum(m_prev, jnp.max(mx, axis=1, keepdims=True))
            p = jnp.concatenate([jnp.exp2(t - m_next) for t in tiles], axis=1).astype(BF16)
            alpha = jnp.exp2(m_prev - m_next)
            acc_s[rs, :] = (jnp.concatenate([alpha, alpha], axis=1) * acc_s[rs, :]
                            + jnp.dot(p, v_blk, preferred_element_type=F32))
            m_s[rs, :] = m_next

    scores(sa_ref, 0)

    def body(i, carry):
        j = 2 * i
        scores(sb_ref, j + 1)
        accumulate(sa_ref, j, None)
        scores(sa_ref, j + 2)
        accumulate(sb_ref, j + 1, None)
        return carry

    n_below = n_sub * qi
    lax.fori_loop(0, n_below // 2, body, 0)
    bufs = (sa_ref, sb_ref)
    for e in range(n_sub):
        if e + 1 < n_sub:
            scores(bufs[(e + 1) % 2], n_below + e + 1)
        accumulate(bufs[e % 2], n_below + e, e)

    acc = acc_s[...]
    o_ref[0] = (acc[:, :FOX_DH] / acc[:, FOX_DH:]).astype(o_ref.dtype)


def _fox(qkv, sm, gq, gk):
    b, s, _ = qkv.shape
    nh = FOX_HEADS
    tq = _tile(s, FOX_TQ)
    tk = _tile(tq, FOX_TK)
    assert (tq // tk) % 2 == 0, "the score pipeline consumes key blocks in pairs"
    nk = s // tk
    return pl.pallas_call(
        functools.partial(_fox_kernel, tq=tq, tk=tk, nk=nk),
        out_shape=jax.ShapeDtypeStruct((b, s, nh * FOX_DH), BF16),
        grid=(b, nh, s // tq),
        in_specs=[
            pl.BlockSpec((1, tq, FOX_DH), lambda i, h, j: (i, j, h)),
            pl.BlockSpec((1, s, FOX_DH), lambda i, h, j: (i, 0, nh + h)),
            pl.BlockSpec((1, s, FOX_DH), lambda i, h, j: (i, 0, 2 * nh + h)),
            pl.BlockSpec((1, tq, LANES), lambda i, h, j: (i, j, 0)),
            pl.BlockSpec((1, s, LANES), lambda i, h, j: (i, 0, 0)),
            pl.BlockSpec((1, FOX_DH), lambda i, h, j: (0, 0)),
            pl.BlockSpec((1, FOX_DH), lambda i, h, j: (0, 0)),
        ],
        out_specs=pl.BlockSpec((1, tq, FOX_DH), lambda i, h, j: (i, j, h)),
        scratch_shapes=[
            pltpu.VMEM((s, 2 * FOX_DH), BF16),
            pltpu.VMEM((s, 2 * FOX_DH), BF16),
            pltpu.VMEM((tq, LANES), F32),
            pltpu.VMEM((tq, 2 * FOX_DH), F32),
            pltpu.VMEM((tq, tk), F32),
            pltpu.VMEM((tq, tk), F32),
        ],
        compiler_params=_cparams(("arbitrary", "arbitrary", "arbitrary")),
        name="fox",
    )(qkv, qkv, qkv, sm, sm, gq, gk)


def _memkv_kernel(mem_ref, g_ref, w_ref, gk_ref, o_ref, *, n_k_tiles):
    j = pl.program_id(1)
    m = mem_ref[0]
    hn = (m * lax.rsqrt(jnp.mean(m * m, axis=-1, keepdims=True) + EPS) * g_ref[...]).astype(BF16)
    r = jnp.dot(hn, w_ref[...].astype(BF16), preferred_element_type=F32)
    parts = []
    for t in range(r.shape[1] // MEM_DH):
        rt = r[:, t * MEM_DH:(t + 1) * MEM_DH]
        parts.append(rt * lax.rsqrt(jnp.mean(rt * rt, axis=-1, keepdims=True) + EPS) * gk_ref[...])
    normed = jnp.concatenate(parts, axis=1)
    is_k = jnp.where(j < n_k_tiles, 1.0, 0.0)
    o_ref[0] = (is_k * normed + (1.0 - is_k) * r).astype(o_ref.dtype)


def _memkv(mem, g, w, gk):
    b, ml, d = mem.shape
    n = w.shape[1]
    tn = 2 * MEM_DH
    return pl.pallas_call(
        functools.partial(_memkv_kernel, n_k_tiles=(n // 2) // tn),
        out_shape=jax.ShapeDtypeStruct((b, ml, n), BF16),
        grid=(b, n // tn),
        in_specs=[
            pl.BlockSpec((1, ml, d), lambda i, j: (i, 0, 0)),
            pl.BlockSpec((1, d), lambda i, j: (0, 0)),
            pl.BlockSpec((d, tn), lambda i, j: (0, j)),
            pl.BlockSpec((1, MEM_DH), lambda i, j: (0, 0)),
        ],
        out_specs=pl.BlockSpec((1, ml, tn), lambda i, j: (i, 0, j)),
        compiler_params=_cparams(("arbitrary", "arbitrary")),
        name="memkv",
    )(mem, g, w, gk)


def _mem_kernel(q_ref, k_ref, v_ref, gq_ref, o_ref):
    for hd in range(MEM_HEADS):
        cs = slice(hd * MEM_DH, (hd + 1) * MEM_DH)
        qq = q_ref[0, :, cs].astype(F32)
        qn = qq * lax.rsqrt(jnp.mean(qq * qq, axis=-1, keepdims=True) + EPS) * gq_ref[...] * (MEM_DH ** -0.5)
        s = _dot_nt(qn, k_ref[0, :, cs])
        p = jnp.exp(s - jnp.max(s, axis=1, keepdims=True))
        p = p / jnp.sum(p, axis=1, keepdims=True)
        o_ref[0, :, cs] = jnp.dot(p.astype(BF16), v_ref[0, :, cs], preferred_element_type=F32).astype(o_ref.dtype)


def _mem_attn(qsrc, q_col_block, kv, gq):
    b, s, _ = qsrc.shape
    ml = kv.shape[1]
    wq = MEM_HEADS * MEM_DH
    tq = _tile(s, 512)
    return pl.pallas_call(
        _mem_kernel,
        out_shape=jax.ShapeDtypeStruct((b, s, wq), BF16),
        grid=(b, s // tq),
        in_specs=[
            pl.BlockSpec((1, tq, wq), lambda i, j: (i, j, q_col_block)),
            pl.BlockSpec((1, ml, wq), lambda i, j: (i, 0, 0)),
            pl.BlockSpec((1, ml, wq), lambda i, j: (i, 0, 1)),
            pl.BlockSpec((1, MEM_DH), lambda i, j: (0, 0)),
        ],
        out_specs=pl.BlockSpec((1, tq, wq), lambda i, j: (i, j, 0)),
        compiler_params=_cparams(("arbitrary", "arbitrary")),
        name="mem_attn",
    )(qsrc, kv, kv, gq)


def _merge_kernel(oa_ref, ob_ref, om_ref, wa_ref, wb_ref, wm_ref, ga_ref, gb_ref, gm_ref, y_ref):
    d = lambda o_ref, w_ref: jnp.dot(o_ref[...], w_ref[...].astype(BF16), preferred_element_type=F32)
    y = (ga_ref[...].astype(F32) * d(oa_ref, wa_ref)
         + gb_ref[...].astype(F32) * d(ob_ref, wb_ref)
         + gm_ref[...].astype(F32) * d(om_ref, wm_ref))
    y_ref[...] = y.astype(y_ref.dtype)


def _merge(oa, ob, om, wa, wb, wm, gates):
    t, ka = oa.shape
    dm = wa.shape[1]
    tm = _tile(t, 2048)
    tn = _tile(dm, 256)
    nb = dm // tn
    a_spec = lambda kk: pl.BlockSpec((tm, kk), lambda i, j: (i, 0))
    w_spec = lambda kk: pl.BlockSpec((kk, tn), lambda i, j: (0, j))
    g_spec = lambda off: pl.BlockSpec((tm, tn), lambda i, j: (i, off * nb + j))
    return pl.pallas_call(
        _merge_kernel,
        out_shape=jax.ShapeDtypeStruct((t, dm), BF16),
        grid=(t // tm, nb),
        in_specs=[a_spec(ka), a_spec(ob.shape[1]), a_spec(om.shape[1]),
                  w_spec(ka), w_spec(ob.shape[1]), w_spec(om.shape[1]),
                  g_spec(0), g_spec(1), g_spec(2)],
        out_specs=pl.BlockSpec((tm, tn), lambda i, j: (i, j)),
        compiler_params=_cparams(("arbitrary", "arbitrary")),
        name="merge",
    )(oa, ob, om, wa, wb, wm, gates, gates, gates)


def _layer(x, mem, g_mix, w_in, conv_w, a_log, dt_bias, gdn_norm_g, fox_b_f, fox_q_norm, fox_k_norm,
           g_mem, w_mem_kv, mem_q_norm, mem_k_norm, w_up_gdn, w_up_fox, w_up_mem, w_out, g_mlp, w_ff1, w_ff2):
    b, s, d = x.shape
    t = b * s
    gdn_qk = GDN_HEADS * GDN_DK
    gdn_v = GDN_HEADS * GDN_DV
    fox_w = FOX_HEADS * FOX_DH
    mem_w = MEM_HEADS * MEM_DH
    o_z = 2 * gdn_qk + gdn_v
    o_beta = o_z + gdn_v
    o_dec = o_beta + GDN_HEADS
    o_fq = o_dec + GDN_HEADS
    o_ff = o_fq + 3 * fox_w
    o_mq = o_ff + FOX_HEADS
    o_gate = o_mq + mem_w

    wt = w_in.T
    n_small = 2 * GDN_HEADS + FOX_HEADS
    w_small = jnp.concatenate([wt[o_beta:o_fq], wt[o_ff:o_mq],
                               jnp.zeros((LANES - n_small, d), F32)], axis=0)
    zpad = jnp.zeros((LANES - n_small,), F32)
    bias = jnp.concatenate([jnp.zeros((GDN_HEADS,), F32), dt_bias.astype(F32), fox_b_f.astype(F32), zpad])[None, :]
    alog = jnp.concatenate([jnp.zeros((GDN_HEADS,), F32), a_log.astype(F32), jnp.zeros((FOX_HEADS,), F32), zpad])[None, :]

    h, sm = _norm_small(x, g_mix[None, :], w_small, bias, alog)
    h2d = h.reshape(t, d)

    proj = functools.partial(_matmul, h2d, wt, tm=2048, tn=512, w_rows_are_outputs=True)
    qkvz = proj(out_dtype=F32, w_row_ranges=[(0, o_beta)], name="proj_gdn").reshape(b, s, -1)
    att = proj(out_dtype=BF16, w_row_ranges=[(o_fq, 3 * fox_w), (o_mq, mem_w)],
               name="proj_att").reshape(b, s, -1)
    gates = proj(out_dtype=BF16, act="sigmoid", w_row_ranges=[(o_gate, N_BRANCH * d)], name="proj_gate")

    o_a = _gdn(qkvz, sm, conv_w, gdn_norm_g[None, :])
    o_b = _fox(att, sm, fox_q_norm[None, :], fox_k_norm[None, :])
    kv_m = _memkv(mem, g_mem[None, :], w_mem_kv, mem_k_norm[None, :])
    o_m = _mem_attn(att, (3 * fox_w) // mem_w, kv_m, mem_q_norm[None, :])

    y = _merge(o_a.reshape(t, -1), o_b.reshape(t, -1), o_m.reshape(t, -1), w_up_gdn, w_up_fox, w_up_mem, gates)
    x1, h2 = _out_norm(y, w_out.astype(BF16), x.reshape(t, d), g_mlp[None, :])
    u = _matmul(h2, w_ff1, out_dtype=BF16, tm=2048, tn=512, act="relu2", name="ff1")
    out = _matmul(u, w_ff2.astype(BF16), out_dtype=F32, tm=1024, tn=256, residual=x1, name="ff2")
    return out.reshape(b, s, d)


def kernel(x, mem, g_mix, w_in, conv_w, a_log, dt_bias, gdn_norm_g, fox_b_f, fox_q_norm, fox_k_norm, g_mem, w_mem_kv, mem_q_norm, mem_k_norm, w_up_gdn, w_up_fox, w_up_mem, w_out, g_mlp, w_ff1, w_ff2):
    depth = w_in.shape[0]
    for l in range(depth):
        x = _layer(x, mem, g_mix[l], w_in[l], conv_w[l], a_log[l], dt_bias[l], gdn_norm_g[l], fox_b_f[l],
                   fox_q_norm[l], fox_k_norm[l], g_mem[l], w_mem_kv[l], mem_q_norm[l], mem_k_norm[l],
                   w_up_gdn[l], w_up_fox[l], w_up_mem[l], w_out[l], g_mlp[l], w_ff1[l], w_ff2[l])
    return x
```

```python
import functools
import math

import jax
import jax.numpy as jnp
from jax import lax
from jax.experimental import pallas as pl
from jax.experimental.pallas import tpu as pltpu

F32 = jnp.float32
BF16 = jnp.bfloat16
EPS = 1e-6

GDN_HEADS = 8
GDN_DK = 128
GDN_DV = 128
GDN_CONV = 4
GDN_CHUNK = 128
GDN_HEADS_PER_STEP = 8
GDN_ROWS_PER_STEP = 256
FOX_HEADS = 8
FOX_DH = 128
FOX_TQ = 1024
FOX_TK = 512
MEM_HEADS = 4
MEM_DH = 256
N_BRANCH = 3
LANES = 128
NEG_BIG = -1e30
LOG2E = math.log2(math.e)

VMEM_LIMIT = 56 * 1024 * 1024


def _tile(n, pref):
    return pref if n % pref == 0 else n


def _cparams(sem):
    return pltpu.CompilerParams(dimension_semantics=sem, vmem_limit_bytes=VMEM_LIMIT)


def _dot(a, b):
    return jnp.dot(a.astype(BF16), b.astype(BF16), preferred_element_type=F32)


def _dot_nt(a, b):
    return lax.dot_general(a.astype(BF16), b.astype(BF16), (((1,), (1,)), ((), ())),
                           preferred_element_type=F32)


def _split2(a):
    hi = a.astype(BF16)
    lo = (a - hi.astype(F32)).astype(BF16)
    return hi, lo


def _split3(a):
    hi = a.astype(BF16).astype(F32)
    r = a - hi
    mid = r.astype(BF16).astype(F32)
    lo = (r - mid).astype(BF16).astype(F32)
    return hi, mid, lo


def _dot3_nt(a, b):
    a_hi, a_lo = _split2(a)
    b_hi, b_lo = _split2(b)
    n = b.shape[0]
    d = lambda p, q: lax.dot_general(p, q, (((1,), (1,)), ((), ())), preferred_element_type=F32)
    both = d(a_hi, jnp.concatenate([b_hi, b_lo], axis=0))
    return both[:, :n] + both[:, n:] + d(a_lo, b_hi)


def _dot_exact_lhs(l_bf16, v):
    hi, mid, lo = _split3(v)
    n = v.shape[1]
    d = functools.partial(jnp.dot, preferred_element_type=F32)
    both = d(l_bf16, jnp.concatenate([hi, mid], axis=1).astype(BF16))
    return both[:, :n] + both[:, n:] + d(l_bf16, lo.astype(BF16))


def _lane_col(a, idx):
    lane = lax.broadcasted_iota(jnp.int32, a.shape, 1)
    return jnp.sum(jnp.where(lane == idx, a, 0.0), axis=1, keepdims=True)


def _softplus(x):
    return jnp.maximum(x, 0.0) + jnp.log1p(jnp.exp(-jnp.abs(x)))


def _norm_small_kernel(x_ref, g_ref, ws_ref, bias_ref, alog_ref, h_ref, sm_ref, carry_ref, *, tm):
    s = pl.program_id(1)

    @pl.when(s == 0)
    def _():
        carry_ref[...] = jnp.zeros_like(carry_ref)

    x = x_ref[0]
    h = x * lax.rsqrt(jnp.mean(x * x, axis=-1, keepdims=True) + EPS) * g_ref[...]
    h_ref[0] = h.astype(BF16)

    pre = _dot3_nt(h, ws_ref[...]) + bias_ref[...]
    lane = lax.broadcasted_iota(jnp.int32, pre.shape, 1)
    nh = GDN_HEADS
    beta = jax.nn.sigmoid(pre)
    gdec = -jnp.exp(alog_ref[...]) * _softplus(pre)
    logf = -_softplus(-pre)
    vals = jnp.where(lane < nh, beta, jnp.where(lane < 2 * nh, gdec, jnp.where(lane < 3 * nh, logf, 0.0)))

    row = lax.broadcasted_iota(jnp.int32, (tm, tm), 0)
    col = lax.broadcasted_iota(jnp.int32, (tm, tm), 1)
    low = col <= row
    l_full = jnp.where(low, 1.0, 0.0).astype(BF16)
    sh = GDN_CHUNK.bit_length() - 1
    same_chunk = (row >> sh) == (col >> sh)
    l_blk = jnp.where(low, jnp.where(same_chunk, 1.0, 0.0), 0.0).astype(BF16)
    cs_blk = _dot_exact_lhs(l_blk, vals)
    cs_full = _dot_exact_lhs(l_full, vals) + carry_ref[0:1, :]
    hi, mid, lo = _split3(cs_full * LOG2E)
    pieces = jnp.where(lane < 3 * nh, hi, jnp.where(lane < 4 * nh, pltpu.roll(mid, nh, 1), pltpu.roll(lo, 2 * nh, 1)))
    sm_ref[0] = jnp.where(lane < nh, vals, jnp.where(lane < 2 * nh, cs_blk, jnp.where(lane < 5 * nh, pieces, 0.0)))
    carry_ref[...] = jnp.broadcast_to(cs_full[tm - 1:tm, :], carry_ref.shape)


def _norm_small(x, g, w_small, bias, alog):
    b, s, d = x.shape
    tm = _tile(s, 512)
    return pl.pallas_call(
        functools.partial(_norm_small_kernel, tm=tm),
        out_shape=(jax.ShapeDtypeStruct((b, s, d), BF16), jax.ShapeDtypeStruct((b, s, LANES), F32)),
        grid=(b, s // tm),
        in_specs=[
            pl.BlockSpec((1, tm, d), lambda i, j: (i, j, 0)),
            pl.BlockSpec((1, d), lambda i, j: (0, 0)),
            pl.BlockSpec((LANES, d), lambda i, j: (0, 0)),
            pl.BlockSpec((1, LANES), lambda i, j: (0, 0)),
            pl.BlockSpec((1, LANES), lambda i, j: (0, 0)),
        ],
        out_specs=(
            pl.BlockSpec((1, tm, d), lambda i, j: (i, j, 0)),
            pl.BlockSpec((1, tm, LANES), lambda i, j: (i, j, 0)),
        ),
        scratch_shapes=[pltpu.VMEM((8, LANES), F32)],
        compiler_params=_cparams(("arbitrary", "arbitrary")),
        name="norm_small",
    )(x, g, w_small, bias, alog)


def _out_norm_kernel(y_ref, w_ref, x_ref, g_ref, x1_ref, h_ref):
    x1 = x_ref[...] + jnp.dot(y_ref[...], w_ref[...], preferred_element_type=F32)
    x1_ref[...] = x1
    h_ref[...] = (x1 * lax.rsqrt(jnp.mean(x1 * x1, axis=-1, keepdims=True) + EPS) * g_ref[...]).astype(h_ref.dtype)


def _out_norm(y, w_bf16, x2d, g):
    t, d = x2d.shape
    k = y.shape[1]
    tm = _tile(t, 512)
    row = lambda width: pl.BlockSpec((tm, width), lambda i: (i, 0))
    return pl.pallas_call(
        _out_norm_kernel,
        out_shape=(jax.ShapeDtypeStruct((t, d), F32), jax.ShapeDtypeStruct((t, d), BF16)),
        grid=(t // tm,),
        in_specs=[row(k), pl.BlockSpec((k, d), lambda i: (0, 0)), row(d), pl.BlockSpec((1, d), lambda i: (0, 0))],
        out_specs=(row(d), row(d)),
        compiler_params=_cparams(("arbitrary",)),
        name="out_norm",
    )(y, w_bf16, x2d, g)


def _mm_kernel(a_ref, w_ref, *rest, act, has_res, w_rows_are_outputs):
    if has_res:
        r_ref, o_ref = rest
    else:
        (o_ref,) = rest
    contract = (((1,), (1,)), ((), ())) if w_rows_are_outputs else (((1,), (0,)), ((), ()))
    acc = lax.dot_general(a_ref[...], w_ref[...].astype(BF16), contract, preferred_element_type=F32)
    if act == "sigmoid":
        acc = 0.5 * jnp.tanh(0.5 * acc) + 0.5
    elif act == "relu2":
        r = jnp.maximum(acc, 0.0)
        acc = r * r
    if has_res:
        acc = acc + r_ref[...]
    o_ref[...] = acc.astype(o_ref.dtype)


def _matmul(a, w, *, out_dtype, tm, tn, act=None, residual=None, w_rows_are_outputs=False,
            w_row_ranges=None, name="matmul"):
    m, k = a.shape
    if w_row_ranges is not None:
        assert w_rows_are_outputs
        n = sum(r for _, r in w_row_ranges)
    else:
        n = w.shape[0] if w_rows_are_outputs else w.shape[1]
    tm = _tile(m, tm)
    tn = _tile(n, tn)
    if w_row_ranges is not None:
        assert all(r % tn == 0 and f % 8 == 0 for f, r in w_row_ranges)

        def w_rows(i, j):
            start, first_blk = jnp.int32(0), 0
            for f, r in w_row_ranges:
                start = jnp.where(j >= first_blk, f + (j - first_blk) * tn, start)
                first_blk += r // tn
            return pl.multiple_of(start, 8), 0

        w_spec = pl.BlockSpec((pl.Element(tn), pl.Element(k)), w_rows)
    elif w_rows_are_outputs:
        w_spec = pl.BlockSpec((tn, k), lambda i, j: (j, 0))
    else:
        w_spec = pl.BlockSpec((k, tn), lambda i, j: (0, j))
    in_specs = [pl.BlockSpec((tm, k), lambda i, j: (i, 0)), w_spec]
    args = [a, w]
    if residual is not None:
        in_specs.append(pl.BlockSpec((tm, tn), lambda i, j: (i, j)))
        args.append(residual)
    return pl.pallas_call(
        functools.partial(_mm_kernel, act=act, has_res=residual is not None,
                          w_rows_are_outputs=w_rows_are_outputs),
        out_shape=jax.ShapeDtypeStruct((m, n), out_dtype),
        grid=(m // tm, n // tn),
        in_specs=in_specs,
        out_specs=pl.BlockSpec((tm, tn), lambda i, j: (i, j)),
        compiler_params=_cparams(("arbitrary", "arbitrary")),
        name=name,
    )(*args)


def _gdn_kernel(q_ref, k_ref, v_ref, z_ref, sm_ref, cwq_ref, cwk_ref, cwv_ref, gn_ref, o_ref,
                state_ref, tail_ref, ext_ref, qs, ks, vs, *, ts, nc, hb, nb):
    hg = pl.program_id(0)
    si = pl.program_id(1)
    c_sz = GDN_CHUNK

    @pl.when(si == 0)
    def _():
        state_ref[...] = jnp.zeros_like(state_ref)
        tail_ref[...] = jnp.zeros_like(tail_ref)

    def conv_silu(x_ref, w_ref, bi, idx, c):
        r0 = c * c_sz
        x = x_ref[bi, r0:r0 + c_sz, :]
        w = w_ref[...]
        y = x * w[0:1, :]
        for i in range(1, GDN_CONV):
            y = pltpu.roll(y, 1, 0) + x * w[i:i + 1, :]
        ext_ref[idx, 0:8, :] = tail_ref[bi, idx] if c == 0 else x_ref[bi, r0 - 8:r0, :]
        ext_ref[idx, 8:16, :] = x[0:8, :]
        first = 8 - (GDN_CONV - 1)
        head = ext_ref[idx, first:first + 8, :] * w[0:1, :]
        for i in range(1, GDN_CONV):
            head = head + ext_ref[idx, first + i:first + i + 8, :] * w[i:i + 1, :]
        y = jnp.concatenate([head, y[8:, :]], axis=0)
        if c == nc - 1:
            tail_ref[bi, idx] = x[c_sz - 8:c_sz, :]
        return y * jax.nn.sigmoid(y)

    def prepare(c):
        rows = slice(c * c_sz, (c + 1) * c_sz)
        for bi in range(nb):
            q = conv_silu(q_ref, cwq_ref, bi, 0, c)
            k = conv_silu(k_ref, cwk_ref, bi, 1, c)
            vs[bi, rows, :] = conv_silu(v_ref, cwv_ref, bi, 2, c)
            for hh in range(hb):
                cs = slice(hh * LANES, (hh + 1) * LANES)
                qh = q[:, cs]
                kh = k[:, cs]
                qs[bi, rows, cs] = qh * (lax.rsqrt(jnp.sum(qh * qh, axis=-1, keepdims=True) + EPS)
                                         * (GDN_DK ** -0.5))
                ks[bi, rows, cs] = kh * lax.rsqrt(jnp.sum(kh * kh, axis=-1, keepdims=True) + EPS)

    ri = lax.broadcasted_iota(jnp.int32, (c_sz, c_sz), 0)
    ci = lax.broadcasted_iota(jnp.int32, (c_sz, c_sz), 1)
    strict = ri > ci
    incl = ri >= ci
    eye = jnp.where(ri == ci, 1.0, 0.0)
    n_lvl = c_sz.bit_length() - 1
    lvl_masks = []
    for l in range(n_lvl):
        same = (ri >> (l + 1)) == (ci >> (l + 1))
        lvl_masks.append(jnp.where(
            same, jnp.where(((ri >> l) & 1) == 1, jnp.where(((ci >> l) & 1) == 0, 1.0, 0.0), 0.0), 0.0))

    def advance(c):
        rows = slice(c * c_sz, (c + 1) * c_sz)
        chains = [(bi, hh) for bi in range(nb) for hh in range(hb)]
        mlows, rhss, lhs2s, qds, cds = [], [], [], [], []
        for bi, hh in chains:
            cs = slice(hh * LANES, (hh + 1) * LANES)
            hd = hg * hb + hh
            smc = sm_ref[bi, rows, :]
            beta = _lane_col(smc, hd)
            gam = _lane_col(smc, hd + GDN_HEADS)
            kc = ks[bi, rows, cs]
            qc = qs[bi, rows, cs]
            egam = jnp.exp(gam)
            g_last = gam[c_sz - 1:c_sz, :]
            gcol = jnp.broadcast_to(gam, (c_sz, c_sz))
            diff = gcol - gcol.T
            e = jnp.exp(jnp.where(incl, diff, 0.0))
            mlows.append(beta * _dot_nt(kc, kc) * jnp.where(strict, e, 0.0))
            qk = _dot_nt(qc, kc) * jnp.where(incl, e, 0.0)
            kd_t = (kc * jnp.exp(g_last - gam)).T
            lhs2s.append(jnp.concatenate([qk, kd_t], axis=0).astype(BF16))
            rhss.append(jnp.concatenate([kc * (beta * egam), vs[bi, rows, cs] * beta], axis=1).astype(BF16))
            qds.append(qc * egam)
            cds.append(jnp.exp(g_last))
        xs = [eye - lvl_masks[0] * m for m in mlows]
        mlows_bf = [m.astype(BF16) for m in mlows]
        for l in range(1, n_lvl):
            mask_bf = lvl_masks[l].astype(BF16)
            tl = [_dot(mask_bf * m, x) for m, x in zip(mlows_bf, xs)]
            xs = [x - _dot(x, t) for x, t in zip(xs, tl)]
        wus = [_dot(x, r) for x, r in zip(xs, rhss)]
        for n, (bi, hh) in enumerate(chains):
            cs = slice(hh * LANES, (hh + 1) * LANES)
            state = state_ref[n]
            wu = wus[n]
            a = _dot(jnp.concatenate([wu[:, :GDN_DK], qds[n]], axis=0), state)
            u = wu[:, GDN_DK:] - a[:c_sz]
            b2 = jnp.dot(lhs2s[n], u.astype(BF16), preferred_element_type=F32)
            o = a[c_sz:] + b2[:c_sz]
            state_ref[n] = state * cds[n] + b2[c_sz:]
            z = z_ref[bi, rows, cs]
            on = o * lax.rsqrt(jnp.mean(o * o, axis=-1, keepdims=True) + EPS) * gn_ref[...]
            o_ref[bi, rows, cs] = (on * (z * jax.nn.sigmoid(z))).astype(o_ref.dtype)

    for c in range(nc):
        prepare(c)
        advance(c)


def _gdn(qkvz, sm, conv_w, gn):
    b, s, _ = qkvz.shape
    nh = GDN_HEADS
    hb = GDN_HEADS_PER_STEP
    ng = nh // hb
    wb = hb * LANES
    ts = _tile(s, GDN_ROWS_PER_STEP)
    nc = ts // GDN_CHUNK
    blk = lambda off: pl.BlockSpec((b, ts, wb), lambda h, j: (0, j, off + h))
    cw = lambda off: pl.BlockSpec((GDN_CONV, wb), lambda h, j: (0, off + h))
    return pl.pallas_call(
        functools.partial(_gdn_kernel, ts=ts, nc=nc, hb=hb, nb=b),
        out_shape=jax.ShapeDtypeStruct((b, s, nh * GDN_DV), BF16),
        grid=(ng, s // ts),
        in_specs=[
            blk(0), blk(ng), blk(2 * ng), blk(3 * ng),
            pl.BlockSpec((b, ts, LANES), lambda h, j: (0, j, 0)),
            cw(0), cw(ng), cw(2 * ng),
            pl.BlockSpec((1, GDN_DV), lambda h, j: (0, 0)),
        ],
        out_specs=pl.BlockSpec((b, ts, wb), lambda h, j: (0, j, h)),
        scratch_shapes=[
            pltpu.VMEM((b * hb, GDN_DK, GDN_DV), F32),
            pltpu.VMEM((b, 3, 8, wb), F32),
            pltpu.VMEM((3, 16, wb), F32),
            pltpu.VMEM((b, ts, wb), F32),
            pltpu.VMEM((b, ts, wb), F32),
            pltpu.VMEM((b, ts, wb), F32),
        ],
        compiler_params=_cparams(("arbitrary", "arbitrary")),
        name="gdn",
    )(qkvz, qkvz, qkvz, qkvz, sm, conv_w, conv_w, conv_w, gn)


def _fox_kernel(q_ref, k_ref, v_ref, smq_ref, smk_ref, gq_ref, gk_ref, o_ref,
                kaug_ref, vaug_ref, m_s, acc_s, sa_ref, sb_ref, *, tq, tk, nk):
    hd = pl.program_id(1)
    qi = pl.program_id(2)
    n_sub = tq // tk
    n_lt = tk // LANES

    src = lax.broadcasted_iota(jnp.int32, (LANES, LANES), 0)
    dst = lax.broadcasted_iota(jnp.int32, (LANES, LANES), 1)
    first_piece = 2 * GDN_HEADS + hd

    def bias_cols(sm_block, dst0, sign, ones0):
        pick = jnp.where((dst >= dst0) & (dst < dst0 + 3) & (src == first_piece + FOX_HEADS * (dst - dst0)),
                         sign, 0.0).astype(BF16)
        lane = lax.broadcasted_iota(jnp.int32, sm_block.shape, 1)
        ones = jnp.where((lane >= ones0) & (lane < ones0 + 3), 1.0, 0.0)
        return (jnp.dot(sm_block.astype(BF16), pick, preferred_element_type=F32) + ones).astype(BF16)

    @pl.when(qi == 0)
    def _():
        def build(j, carry):
            rows = pl.ds(pl.multiple_of(j * tk, tk), tk)
            kk = k_ref[0, rows, :].astype(F32)
            kn = kk * lax.rsqrt(jnp.mean(kk * kk, axis=-1, keepdims=True) + EPS) * gk_ref[...]
            kaug_ref[rows, 0:FOX_DH] = kn.astype(BF16)
            kaug_ref[rows, FOX_DH:2 * FOX_DH] = bias_cols(smk_ref[0, rows, :], 3, -1.0, 0)
            vaug_ref[rows, 0:FOX_DH] = v_ref[0, rows, :]
            vaug_ref[rows, FOX_DH:2 * FOX_DH] = jnp.ones((tk, FOX_DH), BF16)
            return carry
        lax.fori_loop(0, nk, build, 0)

    qq = q_ref[0].astype(F32)
    qn = qq * (lax.rsqrt(jnp.mean(qq * qq, axis=-1, keepdims=True) + EPS) * (FOX_DH ** -0.5 * LOG2E)) * gq_ref[...]
    q_aug = jnp.concatenate([qn.astype(BF16), bias_cols(smq_ref[0], 0, 1.0, 3)], axis=1)

    m_s[...] = jnp.full_like(m_s, NEG_BIG)
    acc_s[...] = jnp.zeros_like(acc_s)

    def scores(s_ref, j):
        rows = pl.ds(pl.multiple_of(j * tk, tk), tk)
        s_ref[...] = lax.dot_general(q_aug, kaug_ref[rows, :], (((1,), (1,)), ((), ())),
                                     preferred_element_type=F32)

    def accumulate(s_ref, j, diag):
        rows = pl.ds(pl.multiple_of(j * tk, tk), tk)
        v_blk = vaug_ref[rows, :]
        groups = ([(0, tq, False)] if diag is None
                  else [(r * tk, (r + 1) * tk, r == diag) for r in range(diag, n_sub)])
        for lo_row, hi_row, masked in groups:
            rs = slice(lo_row, hi_row)
            tiles = [s_ref[rs, c * LANES:(c + 1) * LANES] for c in range(n_lt)]
            if masked:
                ri = lax.broadcasted_iota(jnp.int32, (tk, LANES), 0)
                ci = lax.broadcasted_iota(jnp.int32, (tk, LANES), 1)
                tiles = [jnp.where(ri >= ci + c * LANES, t, NEG_BIG) for c, t in enumerate(tiles)]
            mx = tiles[0]
            for t in tiles[1:]:
                mx = jnp.maximum(mx, t)
            m_prev = m_s[rs, :]
            m_next = jnp.maximum(m_prev, jnp.max(mx, axis=1, keepdims=True))
            p = jnp.concatenate([jnp.exp2(t - m_next) for t in tiles], axis=1).astype(BF16)
            alpha = jnp.exp2(m_prev - m_next)
            acc_s[rs, :] = (jnp.concatenate([alpha, alpha], axis=1) * acc_s[rs, :]
                            + jnp.dot(p, v_blk, preferred_element_type=F32))
            m_s[rs, :] = m_next

    scores(sa_ref, 0)

    def body(i, carry):
        j = 2 * i
        scores(sb_ref, j + 1)
        accumulate(sa_ref, j, None)
        scores(sa_ref, j + 2)
        accumulate(sb_ref, j + 1, None)
        return carry

    n_below = n_sub * qi
    lax.fori_loop(0, n_below // 2, body, 0)
    bufs = (sa_ref, sb_ref)
    for e in range(n_sub):
        if e + 1 < n_sub:
            scores(bufs[(e + 1) % 2], n_below + e + 1)
        accumulate(bufs[e % 2], n_below + e, e)

    acc = acc_s[...]
    o_ref[0] = (acc[:, :FOX_DH] / acc[:, FOX_DH:]).astype(o_ref.dtype)


def _fox(qkv, sm, gq, gk):
    b, s, _ = qkv.shape
    nh = FOX_HEADS
    tq = _tile(s, FOX_TQ)
    tk = _tile(tq, FOX_TK)
    assert (tq // tk) % 2 == 0, "the score pipeline consumes key blocks in pairs"
    nk = s // tk
    return pl.pallas_call(
        functools.partial(_fox_kernel, tq=tq, tk=tk, nk=nk),
        out_shape=jax.ShapeDtypeStruct((b, s, nh * FOX_DH), BF16),
        grid=(b, nh, s // tq),
        in_specs=[
            pl.BlockSpec((1, tq, FOX_DH), lambda i, h, j: (i, j, h)),
            pl.BlockSpec((1, s, FOX_DH), lambda i, h, j: (i, 0, nh + h)),
            pl.BlockSpec((1, s, FOX_DH), lambda i, h, j: (i, 0, 2 * nh + h)),
            pl.BlockSpec((1, tq, LANES), lambda i, h, j: (i, j, 0)),
            pl.BlockSpec((1, s, LANES), lambda i, h, j: (i, 0, 0)),
            pl.BlockSpec((1, FOX_DH), lambda i, h, j: (0, 0)),
            pl.BlockSpec((1, FOX_DH), lambda i, h, j: (0, 0)),
        ],
        out_specs=pl.BlockSpec((1, tq, FOX_DH), lambda i, h, j: (i, j, h)),
        scratch_shapes=[
            pltpu.VMEM((s, 2 * FOX_DH), BF16),
            pltpu.VMEM((s, 2 * FOX_DH), BF16),
            pltpu.VMEM((tq, LANES), F32),
            pltpu.VMEM((tq, 2 * FOX_DH), F32),
            pltpu.VMEM((tq, tk), F32),
            pltpu.VMEM((tq, tk), F32),
        ],
        compiler_params=_cparams(("arbitrary", "arbitrary", "arbitrary")),
        name="fox",
    )(qkv, qkv, qkv, sm, sm, gq, gk)


def _memkv_kernel(mem_ref, g_ref, w_ref, gk_ref, o_ref, *, n_k_tiles):
    j = pl.program_id(1)
    m = mem_ref[0]
    hn = (m * lax.rsqrt(jnp.mean(m * m, axis=-1, keepdims=True) + EPS) * g_ref[...]).astype(BF16)
    r = jnp.dot(hn, w_ref[...].astype(BF16), preferred_element_type=F32)
    parts = []
    for t in range(r.shape[1] // MEM_DH):
        rt = r[:, t * MEM_DH:(t + 1) * MEM_DH]
        parts.append(rt * lax.rsqrt(jnp.mean(rt * rt, axis=-1, keepdims=True) + EPS) * gk_ref[...])
    normed = jnp.concatenate(parts, axis=1)
    is_k = jnp.where(j < n_k_tiles, 1.0, 0.0)
    o_ref[0] = (is_k * normed + (1.0 - is_k) * r).astype(o_ref.dtype)


def _memkv(mem, g, w, gk):
    b, ml, d = mem.shape
    n = w.shape[1]
    tn = 2 * MEM_DH
    return pl.pallas_call(
        functools.partial(_memkv_kernel, n_k_tiles=(n // 2) // tn),
        out_shape=jax.ShapeDtypeStruct((b, ml, n), BF16),
        grid=(b, n // tn),
        in_specs=[
            pl.BlockSpec((1, ml, d), lambda i, j: (i, 0, 0)),
            pl.BlockSpec((1, d), lambda i, j: (0, 0)),
            pl.BlockSpec((d, tn), lambda i, j: (0, j)),
            pl.BlockSpec((1, MEM_DH), lambda i, j: (0, 0)),
        ],
        out_specs=pl.BlockSpec((1, ml, tn), lambda i, j: (i, 0, j)),
        compiler_params=_cparams(("arbitrary", "arbitrary")),
        name="memkv",
    )(mem, g, w, gk)


def _mem_kernel(q_ref, k_ref, v_ref, gq_ref, o_ref):
    for hd in range(MEM_HEADS):
        cs = slice(hd * MEM_DH, (hd + 1) * MEM_DH)
        qq = q_ref[0, :, cs].astype(F32)
        qn = qq * lax.rsqrt(jnp.mean(qq * qq, axis=-1, keepdims=True) + EPS) * gq_ref[...] * (MEM_DH ** -0.5)
        s = _dot_nt(qn, k_ref[0, :, cs])
        p = jnp.exp(s - jnp.max(s, axis=1, keepdims=True))
        p = p / jnp.sum(p, axis=1, keepdims=True)
        o_ref[0, :, cs] = jnp.dot(p.astype(BF16), v_ref[0, :, cs], preferred_element_type=F32).astype(o_ref.dtype)


def _mem_attn(qsrc, q_col_block, kv, gq):
    b, s, _ = qsrc.shape
    ml = kv.shape[1]
    wq = MEM_HEADS * MEM_DH
    tq = _tile(s, 512)
    return pl.pallas_call(
        _mem_kernel,
        out_shape=jax.ShapeDtypeStruct((b, s, wq), BF16),
        grid=(b, s // tq),
        in_specs=[
            pl.BlockSpec((1, tq, wq), lambda i, j: (i, j, q_col_block)),
            pl.BlockSpec((1, ml, wq), lambda i, j: (i, 0, 0)),
            pl.BlockSpec((1, ml, wq), lambda i, j: (i, 0, 1)),
            pl.BlockSpec((1, MEM_DH), lambda i, j: (0, 0)),
        ],
        out_specs=pl.BlockSpec((1, tq, wq), lambda i, j: (i, j, 0)),
        compiler_params=_cparams(("arbitrary", "arbitrary")),
        name="mem_attn",
    )(qsrc, kv, kv, gq)


def _merge_kernel(oa_ref, ob_ref, om_ref, wa_ref, wb_ref, wm_ref, ga_ref, gb_ref, gm_ref, y_ref):
    d = lambda o_ref, w_ref: jnp.dot(o_ref[...], w_ref[...].astype(BF16), preferred_element_type=F32)
    y = (ga_ref[...].astype(F32) * d(oa_ref, wa_ref)
         + gb_ref[...].astype(F32) * d(ob_ref, wb_ref)
         + gm_ref[...].astype(F32) * d(om_ref, wm_ref))
    y_ref[...] = y.astype(y_ref.dtype)


def _merge(oa, ob, om, wa, wb, wm, gates):
    t, ka = oa.shape
    dm = wa.shape[1]
    tm = _tile(t, 2048)
    tn = _tile(dm, 256)
    nb = dm // tn
    a_spec = lambda kk: pl.BlockSpec((tm, kk), lambda i, j: (i, 0))
    w_spec = lambda kk: pl.BlockSpec((kk, tn), lambda i, j: (0, j))
    g_spec = lambda off: pl.BlockSpec((tm, tn), lambda i, j: (i, off * nb + j))
    return pl.pallas_call(
        _merge_kernel,
        out_shape=jax.ShapeDtypeStruct((t, dm), BF16),
        grid=(t // tm, nb),
        in_specs=[a_spec(ka), a_spec(ob.shape[1]), a_spec(om.shape[1]),
                  w_spec(ka), w_spec(ob.shape[1]), w_spec(om.shape[1]),
                  g_spec(0), g_spec(1), g_spec(2)],
        out_specs=pl.BlockSpec((tm, tn), lambda i, j: (i, j)),
        compiler_params=_cparams(("arbitrary", "arbitrary")),
        name="merge",
    )(oa, ob, om, wa, wb, wm, gates, gates, gates)


def _layer(x, mem, g_mix, w_in, conv_w, a_log, dt_bias, gdn_norm_g, fox_b_f, fox_q_norm, fox_k_norm,
           g_mem, w_mem_kv, mem_q_norm, mem_k_norm, w_up_gdn, w_up_fox, w_up_mem, w_out, g_mlp, w_ff1, w_ff2):
    b, s, d = x.shape
    t = b * s
    gdn_qk = GDN_HEADS * GDN_DK
    gdn_v = GDN_HEADS * GDN_DV
    fox_w = FOX_HEADS * FOX_DH
    mem_w = MEM_HEADS * MEM_DH
    o_z = 2 * gdn_qk + gdn_v
    o_beta = o_z + gdn_v
    o_dec = o_beta + GDN_HEADS
    o_fq = o_dec + GDN_HEADS
    o_ff = o_fq + 3 * fox_w
    o_mq = o_ff + FOX_HEADS
    o_gate = o_mq + mem_w

    wt = w_in.T
    n_small = 2 * GDN_HEADS + FOX_HEADS
    w_small = jnp.concatenate([wt[o_beta:o_fq], wt[o_ff:o_mq],
                               jnp.zeros((LANES - n_small, d), F32)], axis=0)
    zpad = jnp.zeros((LANES - n_small,), F32)
    bias = jnp.concatenate([jnp.zeros((GDN_HEADS,), F32), dt_bias.astype(F32), fox_b_f.astype(F32), zpad])[None, :]
    alog = jnp.concatenate([jnp.zeros((GDN_HEADS,), F32), a_log.astype(F32), jnp.zeros((FOX_HEADS,), F32), zpad])[None, :]

    h, sm = _norm_small(x, g_mix[None, :], w_small, bias, alog)
    h2d = h.reshape(t, d)

    proj = functools.partial(_matmul, h2d, wt, tm=2048, tn=512, w_rows_are_outputs=True)
    qkvz = proj(out_dtype=F32, w_row_ranges=[(0, o_beta)], name="proj_gdn").reshape(b, s, -1)
    att = proj(out_dtype=BF16, w_row_ranges=[(o_fq, 3 * fox_w), (o_mq, mem_w)],
               name="proj_att").reshape(b, s, -1)
    gates = proj(out_dtype=BF16, act="sigmoid", w_row_ranges=[(o_gate, N_BRANCH * d)], name="proj_gate")

    o_a = _gdn(qkvz, sm, conv_w, gdn_norm_g[None, :])
    o_b = _fox(att, sm, fox_q_norm[None, :], fox_k_norm[None, :])
    kv_m = _memkv(mem, g_mem[None, :], w_mem_kv, mem_k_norm[None, :])
    o_m = _mem_attn(att, (3 * fox_w) // mem_w, kv_m, mem_q_norm[None, :])

    y = _merge(o_a.reshape(t, -1), o_b.reshape(t, -1), o_m.reshape(t, -1), w_up_gdn, w_up_fox, w_up_mem, gates)
    x1, h2 = _out_norm(y, w_out.astype(BF16), x.reshape(t, d), g_mlp[None, :])
    u = _matmul(h2, w_ff1, out_dtype=BF16, tm=2048, tn=512, act="relu2", name="ff1")
    out = _matmul(u, w_ff2.astype(BF16), out_dtype=F32, tm=1024, tn=256, residual=x1, name="ff2")
    return out.reshape(b, s, d)


def kernel(x, mem, g_mix, w_in, conv_w, a_log, dt_bias, gdn_norm_g, fox_b_f, fox_q_norm, fox_k_norm, g_mem, w_mem_kv, mem_q_norm, mem_k_norm, w_up_gdn, w_up_fox, w_up_mem, w_out, g_mlp, w_ff1, w_ff2):
    depth = w_in.shape[0]
    for l in range(depth):
        x = _layer(x, mem, g_mix[l], w_in[l], conv_w[l], a_log[l], dt_bias[l], gdn_norm_g[l], fox_b_f[l],
                   fox_q_norm[l], fox_k_norm[l], g_mem[l], w_mem_kv[l], mem_q_norm[l], mem_k_norm[l],
                   w_up_gdn[l], w_up_fox[l], w_up_mem[l], w_out[l], g_mlp[l], w_ff1[l], w_ff2[l])
    return x
```

```python
import functools
import math

import jax
import jax.numpy as jnp
from jax import lax
from jax.experimental import pallas as pl
from jax.experimental.pallas import tpu as pltpu

F32 = jnp.float32
BF16 = jnp.bfloat16
EPS = 1e-6

GDN_HEADS = 8
GDN_DK = 128
GDN_DV = 128
GDN_CONV = 4
GDN_CHUNK = 128
GDN_HEADS_PER_STEP = 8
GDN_ROWS_PER_STEP = 256
FOX_HEADS = 8
FOX_DH = 128
FOX_TQ = 1024
FOX_TK = 512
MEM_HEADS = 4
MEM_DH = 256
N_BRANCH = 3
LANES = 128
SUBLANES = 8
NEG_BIG = -1e30
LOG2E = math.log2(math.e)

V7X_VMEM_BYTES = 64 * 1024 * 1024
VMEM_LIMIT = V7X_VMEM_BYTES - 8 * 1024 * 1024

PROJ_TILE = (2048, 512)
FF1_TILE = (2048, 512)
FF2_TILE = (1024, 256)
MERGE_TILE = (2048, 256)
ROW_TILE = 512


def _tile(n, pref):
    return pref if n % pref == 0 else n


def _cparams(sem):
    return pltpu.CompilerParams(dimension_semantics=sem, vmem_limit_bytes=VMEM_LIMIT)


def _dot(a, b):
    return jnp.dot(a.astype(BF16), b.astype(BF16), preferred_element_type=F32)


def _dot_nt(a, b):
    return lax.dot_general(a.astype(BF16), b.astype(BF16), (((1,), (1,)), ((), ())),
                           preferred_element_type=F32)


def _split2(a):
    hi = a.astype(BF16)
    lo = (a - hi.astype(F32)).astype(BF16)
    return hi, lo


def _split3(a):
    hi = a.astype(BF16).astype(F32)
    r = a - hi
    mid = r.astype(BF16).astype(F32)
    lo = (r - mid).astype(BF16).astype(F32)
    return hi, mid, lo


def _dot_exact_lhs(l_bf16, v):
    hi, mid, lo = _split3(v)
    n = v.shape[1]
    d = functools.partial(jnp.dot, preferred_element_type=F32)
    both = d(l_bf16, jnp.concatenate([hi, mid], axis=1).astype(BF16))
    return both[:, :n] + both[:, n:] + d(l_bf16, lo.astype(BF16))


def _lane_col(a, idx):
    lane = lax.broadcasted_iota(jnp.int32, a.shape, 1)
    return jnp.sum(jnp.where(lane == idx, a, 0.0), axis=1, keepdims=True)


def _softplus(x):
    return jnp.maximum(x, 0.0) + jnp.log1p(jnp.exp(-jnp.abs(x)))


def _norm_small_kernel(x_ref, g_ref, ws_ref, bias_ref, alog_ref, h_ref, sm_ref,
                       carry_ref, wsplit_ref, lfull_ref, lblk_ref, *, tm):
    s = pl.program_id(1)

    @pl.when(s == 0)
    def _():
        carry_ref[...] = jnp.zeros_like(carry_ref)

    @pl.when((pl.program_id(0) == 0) & (s == 0))
    def _():
        w_hi, w_lo = _split2(ws_ref[...])
        wsplit_ref[0:LANES, :] = w_hi
        wsplit_ref[LANES:2 * LANES, :] = w_lo
        row = lax.broadcasted_iota(jnp.int32, (tm, tm), 0)
        col = lax.broadcasted_iota(jnp.int32, (tm, tm), 1)
        low = col <= row
        sh = GDN_CHUNK.bit_length() - 1
        lfull_ref[...] = jnp.where(low, 1.0, 0.0).astype(BF16)
        lblk_ref[...] = jnp.where(low, jnp.where((row >> sh) == (col >> sh), 1.0, 0.0), 0.0).astype(BF16)

    x = x_ref[0]
    h = x * lax.rsqrt(jnp.mean(x * x, axis=-1, keepdims=True) + EPS) * g_ref[...]
    h_hi, h_lo = _split2(h)
    h_ref[0] = h_hi

    nt = lambda p, q: lax.dot_general(p, q, (((1,), (1,)), ((), ())), preferred_element_type=F32)
    both = nt(h_hi, wsplit_ref[...])
    pre = both[:, :LANES] + both[:, LANES:] + nt(h_lo, wsplit_ref[0:LANES, :]) + bias_ref[...]
    lane = lax.broadcasted_iota(jnp.int32, pre.shape, 1)
    nh = GDN_HEADS
    beta = jax.nn.sigmoid(pre)
    gdec = -jnp.exp(alog_ref[...]) * _softplus(pre)
    logf = -_softplus(-pre)
    vals = jnp.where(lane < nh, beta, jnp.where(lane < 2 * nh, gdec, jnp.where(lane < 3 * nh, logf, 0.0)))

    cs_blk = _dot_exact_lhs(lblk_ref[...], vals)
    cs_full = _dot_exact_lhs(lfull_ref[...], vals) + carry_ref[0:1, :]
    hi, mid, lo = _split3(cs_full * LOG2E)
    pieces = jnp.where(lane < 3 * nh, hi, jnp.where(lane < 4 * nh, pltpu.roll(mid, nh, 1), pltpu.roll(lo, 2 * nh, 1)))
    sm_ref[0] = jnp.where(lane < nh, vals, jnp.where(lane < 2 * nh, cs_blk, jnp.where(lane < 5 * nh, pieces, 0.0)))
    carry_ref[...] = jnp.broadcast_to(cs_full[tm - 1:tm, :], carry_ref.shape)


def _norm_small(x, g, w_small, bias, alog):
    b, s, d = x.shape
    tm = _tile(s, ROW_TILE)
    return pl.pallas_call(
        functools.partial(_norm_small_kernel, tm=tm),
        out_shape=(jax.ShapeDtypeStruct((b, s, d), BF16), jax.ShapeDtypeStruct((b, s, LANES), F32)),
        grid=(b, s // tm),
        in_specs=[
            pl.BlockSpec((1, tm, d), lambda i, j: (i, j, 0)),
            pl.BlockSpec((1, d), lambda i, j: (0, 0)),
            pl.BlockSpec((LANES, d), lambda i, j: (0, 0)),
            pl.BlockSpec((1, LANES), lambda i, j: (0, 0)),
            pl.BlockSpec((1, LANES), lambda i, j: (0, 0)),
        ],
        out_specs=(
            pl.BlockSpec((1, tm, d), lambda i, j: (i, j, 0)),
            pl.BlockSpec((1, tm, LANES), lambda i, j: (i, j, 0)),
        ),
        scratch_shapes=[
            pltpu.VMEM((SUBLANES, LANES), F32),
            pltpu.VMEM((2 * LANES, d), BF16),
            pltpu.VMEM((tm, tm), BF16),
            pltpu.VMEM((tm, tm), BF16),
        ],
        compiler_params=_cparams(("arbitrary", "arbitrary")),
        name="norm_small",
    )(x, g, w_small, bias, alog)


def _out_norm_kernel(y_ref, w_ref, x_ref, g_ref, x1_ref, h_ref):
    x1 = x_ref[...] + jnp.dot(y_ref[...], w_ref[...], preferred_element_type=F32)
    x1_ref[...] = x1
    h_ref[...] = (x1 * lax.rsqrt(jnp.mean(x1 * x1, axis=-1, keepdims=True) + EPS) * g_ref[...]).astype(h_ref.dtype)


def _out_norm(y, w_bf16, x2d, g):
    t, d = x2d.shape
    k = y.shape[1]
    tm = _tile(t, ROW_TILE)
    row = lambda width: pl.BlockSpec((tm, width), lambda i: (i, 0))
    return pl.pallas_call(
        _out_norm_kernel,
        out_shape=(jax.ShapeDtypeStruct((t, d), F32), jax.ShapeDtypeStruct((t, d), BF16)),
        grid=(t // tm,),
        in_specs=[row(k), pl.BlockSpec((k, d), lambda i: (0, 0)), row(d), pl.BlockSpec((1, d), lambda i: (0, 0))],
        out_specs=(row(d), row(d)),
        compiler_params=_cparams(("arbitrary",)),
        name="out_norm",
    )(y, w_bf16, x2d, g)


def _mm_kernel(a_ref, w_ref, *rest, act, has_res, w_rows_are_outputs):
    if has_res:
        r_ref, o_ref = rest
    else:
        (o_ref,) = rest
    contract = (((1,), (1,)), ((), ())) if w_rows_are_outputs else (((1,), (0,)), ((), ()))
    acc = lax.dot_general(a_ref[...], w_ref[...].astype(BF16), contract, preferred_element_type=F32)
    if act == "sigmoid":
        acc = 0.5 * jnp.tanh(0.5 * acc) + 0.5
    elif act == "relu2":
        r = jnp.maximum(acc, 0.0)
        acc = r * r
    if has_res:
        acc = acc + r_ref[...]
    o_ref[...] = acc.astype(o_ref.dtype)


def _matmul(a, w, *, out_dtype, tile, act=None, residual=None, w_rows_are_outputs=False,
            w_row_ranges=None, name="matmul"):
    m, k = a.shape
    if w_row_ranges is not None:
        assert w_rows_are_outputs
        n = sum(r for _, r in w_row_ranges)
    else:
        n = w.shape[0] if w_rows_are_outputs else w.shape[1]
    tm = _tile(m, tile[0])
    tn = _tile(n, tile[1])
    if w_row_ranges is not None:
        assert all(r % tn == 0 and f % 8 == 0 for f, r in w_row_ranges)

        def w_rows(i, j):
            start, first_blk = jnp.int32(0), 0
            for f, r in w_row_ranges:
                start = jnp.where(j >= first_blk, f + (j - first_blk) * tn, start)
                first_blk += r // tn
            return pl.multiple_of(start, 8), 0

        w_spec = pl.BlockSpec((pl.Element(tn), pl.Element(k)), w_rows)
    elif w_rows_are_outputs:
        w_spec = pl.BlockSpec((tn, k), lambda i, j: (j, 0))
    else:
        w_spec = pl.BlockSpec((k, tn), lambda i, j: (0, j))
    in_specs = [pl.BlockSpec((tm, k), lambda i, j: (i, 0)), w_spec]
    args = [a, w]
    if residual is not None:
        in_specs.append(pl.BlockSpec((tm, tn), lambda i, j: (i, j)))
        args.append(residual)
    return pl.pallas_call(
        functools.partial(_mm_kernel, act=act, has_res=residual is not None,
                          w_rows_are_outputs=w_rows_are_outputs),
        out_shape=jax.ShapeDtypeStruct((m, n), out_dtype),
        grid=(m // tm, n // tn),
        in_specs=in_specs,
        out_specs=pl.BlockSpec((tm, tn), lambda i, j: (i, j)),
        compiler_params=_cparams(("arbitrary", "arbitrary")),
        name=name,
    )(*args)


def _gdn_kernel(q_ref, k_ref, v_ref, z_ref, sm_ref, cwq_ref, cwk_ref, cwv_ref, gn_ref, o_ref,
                state_ref, tail_ref, ext_ref, qs, ks, vs, *, ts, nc, hb, nb):
    hg = pl.program_id(0)
    si = pl.program_id(1)
    c_sz = GDN_CHUNK

    @pl.when(si == 0)
    def _():
        state_ref[...] = jnp.zeros_like(state_ref)
        tail_ref[...] = jnp.zeros_like(tail_ref)

    def conv_silu(x_ref, w_ref, bi, idx, c):
        r0 = c * c_sz
        x = x_ref[bi, r0:r0 + c_sz, :]
        w = w_ref[...]
        y = x * w[0:1, :]
        for i in range(1, GDN_CONV):
            y = pltpu.roll(y, 1, 0) + x * w[i:i + 1, :]
        sl = SUBLANES
        ext_ref[idx, 0:sl, :] = tail_ref[bi, idx] if c == 0 else x_ref[bi, r0 - sl:r0, :]
        ext_ref[idx, sl:2 * sl, :] = x[0:sl, :]
        first = sl - (GDN_CONV - 1)
        head = ext_ref[idx, first:first + sl, :] * w[0:1, :]
        for i in range(1, GDN_CONV):
            head = head + ext_ref[idx, first + i:first + i + sl, :] * w[i:i + 1, :]
        y = jnp.concatenate([head, y[sl:, :]], axis=0)
        if c == nc - 1:
            tail_ref[bi, idx] = x[c_sz - sl:c_sz, :]
        return y * jax.nn.sigmoid(y)

    def prepare(c):
        rows = slice(c * c_sz, (c + 1) * c_sz)
        for bi in range(nb):
            q = conv_silu(q_ref, cwq_ref, bi, 0, c)
            k = conv_silu(k_ref, cwk_ref, bi, 1, c)
            vs[bi, rows, :] = conv_silu(v_ref, cwv_ref, bi, 2, c)
            for hh in range(hb):
                cs = slice(hh * LANES, (hh + 1) * LANES)
                qh = q[:, cs]
                kh = k[:, cs]
                qs[bi, rows, cs] = qh * (lax.rsqrt(jnp.sum(qh * qh, axis=-1, keepdims=True) + EPS)
                                         * (GDN_DK ** -0.5))
                ks[bi, rows, cs] = kh * lax.rsqrt(jnp.sum(kh * kh, axis=-1, keepdims=True) + EPS)

    ri = lax.broadcasted_iota(jnp.int32, (c_sz, c_sz), 0)
    ci = lax.broadcasted_iota(jnp.int32, (c_sz, c_sz), 1)
    strict = ri > ci
    incl = ri >= ci
    eye = jnp.where(ri == ci, 1.0, 0.0)
    n_lvl = c_sz.bit_length() - 1
    lvl_masks = []
    for l in range(n_lvl):
        same = (ri >> (l + 1)) == (ci >> (l + 1))
        lvl_masks.append(jnp.where(
            same, jnp.where(((ri >> l) & 1) == 1, jnp.where(((ci >> l) & 1) == 0, 1.0, 0.0), 0.0), 0.0))

    def advance(c):
        rows = slice(c * c_sz, (c + 1) * c_sz)
        chains = [(bi, hh) for bi in range(nb) for hh in range(hb)]
        mlows, rhss, lhs2s, qds, cds = [], [], [], [], []
        for bi, hh in chains:
            cs = slice(hh * LANES, (hh + 1) * LANES)
            hd = hg * hb + hh
            smc = sm_ref[bi, rows, :]
            beta = _lane_col(smc, hd)
            gam = _lane_col(smc, hd + GDN_HEADS)
            kc = ks[bi, rows, cs]
            qc = qs[bi, rows, cs]
            egam = jnp.exp(gam)
            g_last = gam[c_sz - 1:c_sz, :]
            gcol = jnp.broadcast_to(gam, (c_sz, c_sz))
            diff = gcol - gcol.T
            e = jnp.exp(jnp.where(incl, diff, 0.0))
            mlows.append(beta * _dot_nt(kc, kc) * jnp.where(strict, e, 0.0))
            qk = _dot_nt(qc, kc) * jnp.where(incl, e, 0.0)
            kd_t = (kc * jnp.exp(g_last - gam)).T
            lhs2s.append(jnp.concatenate([qk, kd_t], axis=0).astype(BF16))
            rhss.append(jnp.concatenate([kc * (beta * egam), vs[bi, rows, cs] * beta], axis=1).astype(BF16))
            qds.append(qc * egam)
            cds.append(jnp.exp(g_last))
        xs = [eye - lvl_masks[0] * m for m in mlows]
        mlows_bf = [m.astype(BF16) for m in mlows]
        for l in range(1, n_lvl):
            mask_bf = lvl_masks[l].astype(BF16)
            tl = [_dot(mask_bf * m, x) for m, x in zip(mlows_bf, xs)]
            xs = [x - _dot(x, t) for x, t in zip(xs, tl)]
        wus = [_dot(x, r) for x, r in zip(xs, rhss)]
        for n, (bi, hh) in enumerate(chains):
            cs = slice(hh * LANES, (hh + 1) * LANES)
            state = state_ref[n]
            wu = wus[n]
            a = _dot(jnp.concatenate([wu[:, :GDN_DK], qds[n]], axis=0), state)
            u = wu[:, GDN_DK:] - a[:c_sz]
            b2 = jnp.dot(lhs2s[n], u.astype(BF16), preferred_element_type=F32)
            o = a[c_sz:] + b2[:c_sz]
            state_ref[n] = state * cds[n] + b2[c_sz:]
            z = z_ref[bi, rows, cs]
            on = o * lax.rsqrt(jnp.mean(o * o, axis=-1, keepdims=True) + EPS) * gn_ref[...]
            o_ref[bi, rows, cs] = (on * (z * jax.nn.sigmoid(z))).astype(o_ref.dtype)

    for c in range(nc):
        prepare(c)
        advance(c)


def _gdn(qkvz, sm, conv_w, gn):
    b, s, _ = qkvz.shape
    nh = GDN_HEADS
    hb = GDN_HEADS_PER_STEP
    ng = nh // hb
    wb = hb * LANES
    ts = _tile(s, GDN_ROWS_PER_STEP)
    nc = ts // GDN_CHUNK
    blk = lambda off: pl.BlockSpec((b, ts, wb), lambda h, j: (0, j, off + h))
    cw = lambda off: pl.BlockSpec((GDN_CONV, wb), lambda h, j: (0, off + h))
    return pl.pallas_call(
        functools.partial(_gdn_kernel, ts=ts, nc=nc, hb=hb, nb=b),
        out_shape=jax.ShapeDtypeStruct((b, s, nh * GDN_DV), BF16),
        grid=(ng, s // ts),
        in_specs=[
            blk(0), blk(ng), blk(2 * ng), blk(3 * ng),
            pl.BlockSpec((b, ts, LANES), lambda h, j: (0, j, 0)),
            cw(0), cw(ng), cw(2 * ng),
            pl.BlockSpec((1, GDN_DV), lambda h, j: (0, 0)),
        ],
        out_specs=pl.BlockSpec((b, ts, wb), lambda h, j: (0, j, h)),
        scratch_shapes=[
            pltpu.VMEM((b * hb, GDN_DK, GDN_DV), F32),
            pltpu.VMEM((b, 3, SUBLANES, wb), F32),
            pltpu.VMEM((3, 2 * SUBLANES, wb), F32),
            pltpu.VMEM((b, ts, wb), F32),
            pltpu.VMEM((b, ts, wb), F32),
            pltpu.VMEM((b, ts, wb), F32),
        ],
        compiler_params=_cparams(("arbitrary", "arbitrary")),
        name="gdn",
    )(qkvz, qkvz, qkvz, qkvz, sm, conv_w, conv_w, conv_w, gn)


def _fox_kernel(q_ref, k_ref, v_ref, smq_ref, smk_ref, gq_ref, gk_ref, o_ref,
                kaug_ref, vaug_ref, m_s, acc_s, sa_ref, sb_ref, *, tq, tk, nk):
    hd = pl.program_id(1)
    qi = pl.program_id(2)
    n_sub = tq // tk
    n_lt = tk // LANES

    src = lax.broadcasted_iota(jnp.int32, (LANES, LANES), 0)
    dst = lax.broadcasted_iota(jnp.int32, (LANES, LANES), 1)
    first_piece = 2 * GDN_HEADS + hd

    def bias_cols(sm_block, dst0, sign, ones0):
        pick = jnp.where((dst >= dst0) & (dst < dst0 + 3) & (src == first_piece + FOX_HEADS * (dst - dst0)),
                         sign, 0.0).astype(BF16)
        lane = lax.broadcasted_iota(jnp.int32, sm_block.shape, 1)
        ones = jnp.where((lane >= ones0) & (lane < ones0 + 3), 1.0, 0.0)
        return (jnp.dot(sm_block.astype(BF16), pick, preferred_element_type=F32) + ones).astype(BF16)

    @pl.when(qi == 0)
    def _():
        def build(j, carry):
            rows = pl.ds(pl.multiple_of(j * tk, tk), tk)
            kk = k_ref[0, rows, :].astype(F32)
            kn = kk * lax.rsqrt(jnp.mean(kk * kk, axis=-1, keepdims=True) + EPS) * gk_ref[...]
            kaug_ref[rows, 0:FOX_DH] = kn.astype(BF16)
            kaug_ref[rows, FOX_DH:2 * FOX_DH] = bias_cols(smk_ref[0, rows, :], 3, -1.0, 0)
            vaug_ref[rows, 0:FOX_DH] = v_ref[0, rows, :]
            vaug_ref[rows, FOX_DH:2 * FOX_DH] = jnp.ones((tk, FOX_DH), BF16)
            return carry
        lax.fori_loop(0, nk, build, 0)

    qq = q_ref[0].astype(F32)
    qn = qq * (lax.rsqrt(jnp.mean(qq * qq, axis=-1, keepdims=True) + EPS) * (FOX_DH ** -0.5 * LOG2E)) * gq_ref[...]
    q_aug = jnp.concatenate([qn.astype(BF16), bias_cols(smq_ref[0], 0, 1.0, 3)], axis=1)

    m_s[...] = jnp.full_like(m_s, NEG_BIG)
    acc_s[...] = jnp.zeros_like(acc_s)

    def scores(s_ref, j):
        rows = pl.ds(pl.multiple_of(j * tk, tk), tk)
        s_ref[...] = lax.dot_general(q_aug, kaug_ref[rows, :], (((1,), (1,)), ((), ())),
                                     preferred_element_type=F32)

    def accumulate(s_ref, j, diag):
        rows = pl.ds(pl.multiple_of(j * tk, tk), tk)
        v_blk = vaug_ref[rows, :]
        groups = ([(0, tq, False)] if diag is None
                  else [(r * tk, (r + 1) * tk, r == diag) for r in range(diag, n_sub)])
        for lo_row, hi_row, masked in groups:
            rs = slice(lo_row, hi_row)
            tiles = [s_ref[rs, c * LANES:(c + 1) * LANES] for c in range(n_lt)]
            if masked:
                ri = lax.broadcasted_iota(jnp.int32, (tk, LANES), 0)
                ci = lax.broadcasted_iota(jnp.int32, (tk, LANES), 1)
                tiles = [jnp.where(ri >= ci + c * LANES, t, NEG_BIG) for c, t in enumerate(tiles)]
            mx = tiles[0]
            for t in tiles[1:]:
                mx = jnp.maximum(mx, t)
            m_prev = m_s[rs, :]
            m_next = jnp.maximum(m_prev, jnp.max(mx, axis=1, keepdims=True))
            p = jnp.concatenate([jnp.exp2(t - m_next) for t in tiles], axis=1).astype(BF16)
            alpha = jnp.exp2(m_prev - m_next)
            acc_s[rs, :] = (jnp.concatenate([alpha, alpha], axis=1) * acc_s[rs, :]
                            + jnp.dot(p, v_blk, preferred_element_type=F32))
            m_s[rs, :] = m_next

    scores(sa_ref, 0)

    def body(i, carry):
        j = 2 * i
        scores(sb_ref, j + 1)
        accumulate(sa_ref, j, None)
        scores(sa_ref, j + 2)
        accumulate(sb_ref, j + 1, None)
        return carry

    n_below = n_sub * qi
    lax.fori_loop(0, n_below // 2, body, 0)
    bufs = (sa_ref, sb_ref)
    for e in range(n_sub):
        if e + 1 < n_sub:
            scores(bufs[(e + 1) % 2], n_below + e + 1)
        accumulate(bufs[e % 2], n_below + e, e)

    acc = acc_s[...]
    o_ref[0] = (acc[:, :FOX_DH] / acc[:, FOX_DH:]).astype(o_ref.dtype)


def _fox(qkv, sm, gq, gk):
    b, s, _ = qkv.shape
    nh = FOX_HEADS
    tq = _tile(s, FOX_TQ)
    tk = _tile(tq, FOX_TK)
    assert (tq // tk) % 2 == 0, "the score pipeline consumes key blocks in pairs"
    nk = s // tk
    return pl.pallas_call(
        functools.partial(_fox_kernel, tq=tq, tk=tk, nk=nk),
        out_shape=jax.ShapeDtypeStruct((b, s, nh * FOX_DH), BF16),
        grid=(b, nh, s // tq),
        in_specs=[
            pl.BlockSpec((1, tq, FOX_DH), lambda i, h, j: (i, j, h)),
            pl.BlockSpec((1, s, FOX_DH), lambda i, h, j: (i, 0, nh + h)),
            pl.BlockSpec((1, s, FOX_DH), lambda i, h, j: (i, 0, 2 * nh + h)),
            pl.BlockSpec((1, tq, LANES), lambda i, h, j: (i, j, 0)),
            pl.BlockSpec((1, s, LANES), lambda i, h, j: (i, 0, 0)),
            pl.BlockSpec((1, FOX_DH), lambda i, h, j: (0, 0)),
            pl.BlockSpec((1, FOX_DH), lambda i, h, j: (0, 0)),
        ],
        out_specs=pl.BlockSpec((1, tq, FOX_DH), lambda i, h, j: (i, j, h)),
        scratch_shapes=[
            pltpu.VMEM((s, 2 * FOX_DH), BF16),
            pltpu.VMEM((s, 2 * FOX_DH), BF16),
            pltpu.VMEM((tq, LANES), F32),
            pltpu.VMEM((tq, 2 * FOX_DH), F32),
            pltpu.VMEM((tq, tk), F32),
            pltpu.VMEM((tq, tk), F32),
        ],
        compiler_params=_cparams(("arbitrary", "arbitrary", "arbitrary")),
        name="fox",
    )(qkv, qkv, qkv, sm, sm, gq, gk)


def _memkv_kernel(mem_ref, g_ref, w_ref, gk_ref, o_ref, *, n_k_tiles):
    j = pl.program_id(1)
    m = mem_ref[0]
    hn = (m * lax.rsqrt(jnp.mean(m * m, axis=-1, keepdims=True) + EPS) * g_ref[...]).astype(BF16)
    r = jnp.dot(hn, w_ref[...].astype(BF16), preferred_element_type=F32)
    parts = []
    for t in range(r.shape[1] // MEM_DH):
        rt = r[:, t * MEM_DH:(t + 1) * MEM_DH]
        parts.append(rt * lax.rsqrt(jnp.mean(rt * rt, axis=-1, keepdims=True) + EPS) * gk_ref[...])
    normed = jnp.concatenate(parts, axis=1)
    is_k = jnp.where(j < n_k_tiles, 1.0, 0.0)
    o_ref[0] = (is_k * normed + (1.0 - is_k) * r).astype(o_ref.dtype)


def _memkv(mem, g, w, gk):
    b, ml, d = mem.shape
    n = w.shape[1]
    tn = 2 * MEM_DH
    return pl.pallas_call(
        functools.partial(_memkv_kernel, n_k_tiles=(n // 2) // tn),
        out_shape=jax.ShapeDtypeStruct((b, ml, n), BF16),
        grid=(b, n // tn),
        in_specs=[
            pl.BlockSpec((1, ml, d), lambda i, j: (i, 0, 0)),
            pl.BlockSpec((1, d), lambda i, j: (0, 0)),
            pl.BlockSpec((d, tn), lambda i, j: (0, j)),
            pl.BlockSpec((1, MEM_DH), lambda i, j: (0, 0)),
        ],
        out_specs=pl.BlockSpec((1, ml, tn), lambda i, j: (i, 0, j)),
        compiler_params=_cparams(("arbitrary", "arbitrary")),
        name="memkv",
    )(mem, g, w, gk)


def _mem_kernel(q_ref, k_ref, v_ref, gq_ref, o_ref):
    for hd in range(MEM_HEADS):
        cs = slice(hd * MEM_DH, (hd + 1) * MEM_DH)
        qq = q_ref[0, :, cs].astype(F32)
        qn = qq * lax.rsqrt(jnp.mean(qq * qq, axis=-1, keepdims=True) + EPS) * gq_ref[...] * (MEM_DH ** -0.5)
        s = _dot_nt(qn, k_ref[0, :, cs])
        p = jnp.exp(s - jnp.max(s, axis=1, keepdims=True))
        p = p / jnp.sum(p, axis=1, keepdims=True)
        o_ref[0, :, cs] = jnp.dot(p.astype(BF16), v_ref[0, :, cs], preferred_element_type=F32).astype(o_ref.dtype)


def _mem_attn(qsrc, q_col_block, kv, gq):
    b, s, _ = qsrc.shape
    ml = kv.shape[1]
    wq = MEM_HEADS * MEM_DH
    tq = _tile(s, ROW_TILE)
    return pl.pallas_call(
        _mem_kernel,
        out_shape=jax.ShapeDtypeStruct((b, s, wq), BF16),
        grid=(b, s // tq),
        in_specs=[
            pl.BlockSpec((1, tq, wq), lambda i, j: (i, j, q_col_block)),
            pl.BlockSpec((1, ml, wq), lambda i, j: (i, 0, 0)),
            pl.BlockSpec((1, ml, wq), lambda i, j: (i, 0, 1)),
            pl.BlockSpec((1, MEM_DH), lambda i, j: (0, 0)),
        ],
        out_specs=pl.BlockSpec((1, tq, wq), lambda i, j: (i, j, 0)),
        compiler_params=_cparams(("arbitrary", "arbitrary")),
        name="mem_attn",
    )(qsrc, kv, kv, gq)


def _merge_kernel(oa_ref, ob_ref, om_ref, wa_ref, wb_ref, wm_ref, ga_ref, gb_ref, gm_ref, y_ref):
    d = lambda o_ref, w_ref: jnp.dot(o_ref[...], w_ref[...].astype(BF16), preferred_element_type=F32)
    y = (ga_ref[...].astype(F32) * d(oa_ref, wa_ref)
         + gb_ref[...].astype(F32) * d(ob_ref, wb_ref)
         + gm_ref[...].astype(F32) * d(om_ref, wm_ref))
    y_ref[...] = y.astype(y_ref.dtype)


def _merge(oa, ob, om, wa, wb, wm, gates):
    t, ka = oa.shape
    dm = wa.shape[1]
    tm = _tile(t, MERGE_TILE[0])
    tn = _tile(dm, MERGE_TILE[1])
    nb = dm // tn
    a_spec = lambda kk: pl.BlockSpec((tm, kk), lambda i, j: (i, 0))
    w_spec = lambda kk: pl.BlockSpec((kk, tn), lambda i, j: (0, j))
    g_spec = lambda off: pl.BlockSpec((tm, tn), lambda i, j: (i, off * nb + j))
    return pl.pallas_call(
        _merge_kernel,
        out_shape=jax.ShapeDtypeStruct((t, dm), BF16),
        grid=(t // tm, nb),
        in_specs=[a_spec(ka), a_spec(ob.shape[1]), a_spec(om.shape[1]),
                  w_spec(ka), w_spec(ob.shape[1]), w_spec(om.shape[1]),
                  g_spec(0), g_spec(1), g_spec(2)],
        out_specs=pl.BlockSpec((tm, tn), lambda i, j: (i, j)),
        compiler_params=_cparams(("arbitrary", "arbitrary")),
        name="merge",
    )(oa, ob, om, wa, wb, wm, gates, gates, gates)


def _layer(x, mem, g_mix, w_in, conv_w, a_log, dt_bias, gdn_norm_g, fox_b_f, fox_q_norm, fox_k_norm,
           g_mem, w_mem_kv, mem_q_norm, mem_k_norm, w_up_gdn, w_up_fox, w_up_mem, w_out, g_mlp, w_ff1, w_ff2):
    b, s, d = x.shape
    t = b * s
    gdn_qk = GDN_HEADS * GDN_DK
    gdn_v = GDN_HEADS * GDN_DV
    fox_w = FOX_HEADS * FOX_DH
    mem_w = MEM_HEADS * MEM_DH
    o_z = 2 * gdn_qk + gdn_v
    o_beta = o_z + gdn_v
    o_dec = o_beta + GDN_HEADS
    o_fq = o_dec + GDN_HEADS
    o_ff = o_fq + 3 * fox_w
    o_mq = o_ff + FOX_HEADS
    o_gate = o_mq + mem_w

    wt = w_in.T
    n_small = 2 * GDN_HEADS + FOX_HEADS
    w_small = jnp.concatenate([wt[o_beta:o_fq], wt[o_ff:o_mq],
                               jnp.zeros((LANES - n_small, d), F32)], axis=0)
    zpad = jnp.zeros((LANES - n_small,), F32)
    bias = jnp.concatenate([jnp.zeros((GDN_HEADS,), F32), dt_bias.astype(F32), fox_b_f.astype(F32), zpad])[None, :]
    alog = jnp.concatenate([jnp.zeros((GDN_HEADS,), F32), a_log.astype(F32), jnp.zeros((FOX_HEADS,), F32), zpad])[None, :]

    h, sm = _norm_small(x, g_mix[None, :], w_small, bias, alog)
    h2d = h.reshape(t, d)

    proj = functools.partial(_matmul, h2d, wt, tile=PROJ_TILE, w_rows_are_outputs=True)
    qkvz = proj(out_dtype=F32, w_row_ranges=[(0, o_beta)], name="proj_gdn").reshape(b, s, -1)
    att = proj(out_dtype=BF16, w_row_ranges=[(o_fq, 3 * fox_w), (o_mq, mem_w)],
               name="proj_att").reshape(b, s, -1)
    gates = proj(out_dtype=BF16, act="sigmoid", w_row_ranges=[(o_gate, N_BRANCH * d)], name="proj_gate")

    o_a = _gdn(qkvz, sm, conv_w, gdn_norm_g[None, :])
    o_b = _fox(att, sm, fox_q_norm[None, :], fox_k_norm[None, :])
    kv_m = _memkv(mem, g_mem[None, :], w_mem_kv, mem_k_norm[None, :])
    o_m = _mem_attn(att, (3 * fox_w) // mem_w, kv_m, mem_q_norm[None, :])

    y = _merge(o_a.reshape(t, -1), o_b.reshape(t, -1), o_m.reshape(t, -1), w_up_gdn, w_up_fox, w_up_mem, gates)
    x1, h2 = _out_norm(y, w_out.astype(BF16), x.reshape(t, d), g_mlp[None, :])
    u = _matmul(h2, w_ff1, out_dtype=BF16, tile=FF1_TILE, act="relu2", name="ff1")
    out = _matmul(u, w_ff2.astype(BF16), out_dtype=F32, tile=FF2_TILE, residual=x1, name="ff2")
    return out.reshape(b, s, d)


def kernel(x, mem, g_mix, w_in, conv_w, a_log, dt_bias, gdn_norm_g, fox_b_f, fox_q_norm, fox_k_norm, g_mem, w_mem_kv, mem_q_norm, mem_k_norm, w_up_gdn, w_up_fox, w_up_mem, w_out, g_mlp, w_ff1, w_ff2):
    depth = w_in.shape[0]
    for l in range(depth):
        x = _layer(x, mem, g_mix[l], w_in[l], conv_w[l], a_log[l], dt_bias[l], gdn_norm_g[l], fox_b_f[l],
                   fox_q_norm[l], fox_k_norm[l], g_mem[l], w_mem_kv[l], mem_q_norm[l], mem_k_norm[l],
                   w_up_gdn[l], w_up_fox[l], w_up_mem[l], w_out[l], g_mlp[l], w_ff1[l], w_ff2[l])
    return x
```

```python
import functools
import math

import jax
import jax.numpy as jnp
from jax import lax
from jax.experimental import pallas as pl
from jax.experimental.pallas import tpu as pltpu

F32 = jnp.float32
BF16 = jnp.bfloat16
EPS = 1e-6

GDN_HEADS = 8
GDN_DK = 128
GDN_DV = 128
GDN_CONV = 4
GDN_CHUNK = 128
GDN_HEADS_PER_STEP = 8
GDN_ROWS_PER_STEP = 256
FOX_HEADS = 8
FOX_DH = 128
FOX_TQ = 1024
FOX_TK = 512
MEM_HEADS = 4
MEM_DH = 256
N_BRANCH = 3
LANES = 128
SUBLANES = 8
NEG_BIG = -1e30
LOG2E = math.log2(math.e)

V7X_VMEM_BYTES = 64 * 1024 * 1024
VMEM_LIMIT = V7X_VMEM_BYTES - 8 * 1024 * 1024

PROJ_TILE = (2048, 512)
FF1_TILE = (2048, 512)
FF2_TILE = (1024, 256)
MERGE_TILE = (2048, 256)
ROW_TILE = 512


def _tile(n, pref):
    return pref if n % pref == 0 else n


def _cparams(sem):
    return pltpu.CompilerParams(dimension_semantics=sem, vmem_limit_bytes=VMEM_LIMIT)


def _dot(a, b):
    return jnp.dot(a.astype(BF16), b.astype(BF16), preferred_element_type=F32)


def _dot_nt(a, b):
    return lax.dot_general(a.astype(BF16), b.astype(BF16), (((1,), (1,)), ((), ())),
                           preferred_element_type=F32)


def _split2(a):
    hi = a.astype(BF16)
    lo = (a - hi.astype(F32)).astype(BF16)
    return hi, lo


def _split3(a):
    hi = a.astype(BF16).astype(F32)
    r = a - hi
    mid = r.astype(BF16).astype(F32)
    lo = (r - mid).astype(BF16).astype(F32)
    return hi, mid, lo


def _dot_exact_lhs(l_bf16, v):
    hi, mid, lo = _split3(v)
    n = v.shape[1]
    d = functools.partial(jnp.dot, preferred_element_type=F32)
    both = d(l_bf16, jnp.concatenate([hi, mid], axis=1).astype(BF16))
    return both[:, :n] + both[:, n:] + d(l_bf16, lo.astype(BF16))


def _lane_col(a, idx):
    lane = lax.broadcasted_iota(jnp.int32, a.shape, 1)
    return jnp.sum(jnp.where(lane == idx, a, 0.0), axis=1, keepdims=True)


def _softplus(x):
    return jnp.maximum(x, 0.0) + jnp.log1p(jnp.exp(-jnp.abs(x)))


def _norm_small_kernel(x_ref, g_ref, ws_ref, bias_ref, alog_ref, h_ref, sm_ref,
                       carry_ref, wsplit_ref, lfull_ref, lblk_ref, *, tm):
    s = pl.program_id(1)

    @pl.when(s == 0)
    def _():
        carry_ref[...] = jnp.zeros_like(carry_ref)

    @pl.when((pl.program_id(0) == 0) & (s == 0))
    def _():
        w_hi, w_lo = _split2(ws_ref[...])
        wsplit_ref[0:LANES, :] = w_hi
        wsplit_ref[LANES:2 * LANES, :] = w_lo
        row = lax.broadcasted_iota(jnp.int32, (tm, tm), 0)
        col = lax.broadcasted_iota(jnp.int32, (tm, tm), 1)
        low = col <= row
        sh = GDN_CHUNK.bit_length() - 1
        lfull_ref[...] = jnp.where(low, 1.0, 0.0).astype(BF16)
        lblk_ref[...] = jnp.where(low, jnp.where((row >> sh) == (col >> sh), 1.0, 0.0), 0.0).astype(BF16)

    x = x_ref[0]
    h = x * lax.rsqrt(jnp.mean(x * x, axis=-1, keepdims=True) + EPS) * g_ref[...]
    h_hi, h_lo = _split2(h)
    h_ref[0] = h_hi

    nt = lambda p, q: lax.dot_general(p, q, (((1,), (1,)), ((), ())), preferred_element_type=F32)
    both = nt(h_hi, wsplit_ref[...])
    pre = both[:, :LANES] + both[:, LANES:] + nt(h_lo, wsplit_ref[0:LANES, :]) + bias_ref[...]
    lane = lax.broadcasted_iota(jnp.int32, pre.shape, 1)
    nh = GDN_HEADS
    beta = jax.nn.sigmoid(pre)
    gdec = -jnp.exp(alog_ref[...]) * _softplus(pre)
    logf = -_softplus(-pre)
    vals = jnp.where(lane < nh, beta, jnp.where(lane < 2 * nh, gdec, jnp.where(lane < 3 * nh, logf, 0.0)))

    cs_blk = _dot_exact_lhs(lblk_ref[...], vals)
    cs_full = _dot_exact_lhs(lfull_ref[...], vals) + carry_ref[0:1, :]
    hi, mid, lo = _split3(cs_full * LOG2E)
    pieces = jnp.where(lane < 3 * nh, hi, jnp.where(lane < 4 * nh, pltpu.roll(mid, nh, 1), pltpu.roll(lo, 2 * nh, 1)))
    sm_ref[0] = jnp.where(lane < nh, vals, jnp.where(lane < 2 * nh, cs_blk, jnp.where(lane < 5 * nh, pieces, 0.0)))
    carry_ref[...] = jnp.broadcast_to(cs_full[tm - 1:tm, :], carry_ref.shape)


def _norm_small(x, g, w_small, bias, alog):
    b, s, d = x.shape
    tm = _tile(s, ROW_TILE)
    return pl.pallas_call(
        functools.partial(_norm_small_kernel, tm=tm),
        out_shape=(jax.ShapeDtypeStruct((b, s, d), BF16), jax.ShapeDtypeStruct((b, s, LANES), F32)),
        grid=(b, s // tm),
        in_specs=[
            pl.BlockSpec((1, tm, d), lambda i, j: (i, j, 0)),
            pl.BlockSpec((1, d), lambda i, j: (0, 0)),
            pl.BlockSpec((LANES, d), lambda i, j: (0, 0)),
            pl.BlockSpec((1, LANES), lambda i, j: (0, 0)),
            pl.BlockSpec((1, LANES), lambda i, j: (0, 0)),
        ],
        out_specs=(
            pl.BlockSpec((1, tm, d), lambda i, j: (i, j, 0)),
            pl.BlockSpec((1, tm, LANES), lambda i, j: (i, j, 0)),
        ),
        scratch_shapes=[
            pltpu.VMEM((SUBLANES, LANES), F32),
            pltpu.VMEM((2 * LANES, d), BF16),
            pltpu.VMEM((tm, tm), BF16),
            pltpu.VMEM((tm, tm), BF16),
        ],
        compiler_params=_cparams(("arbitrary", "arbitrary")),
        name="norm_small",
    )(x, g, w_small, bias, alog)


def _out_norm_kernel(y_ref, w_ref, x_ref, g_ref, x1_ref, h_ref):
    x1 = x_ref[...] + jnp.dot(y_ref[...], w_ref[...], preferred_element_type=F32)
    x1_ref[...] = x1
    h_ref[...] = (x1 * lax.rsqrt(jnp.mean(x1 * x1, axis=-1, keepdims=True) + EPS) * g_ref[...]).astype(h_ref.dtype)


def _out_norm(y, w_bf16, x2d, g):
    t, d = x2d.shape
    k = y.shape[1]
    tm = _tile(t, ROW_TILE)
    row = lambda width: pl.BlockSpec((tm, width), lambda i: (i, 0))
    return pl.pallas_call(
        _out_norm_kernel,
        out_shape=(jax.ShapeDtypeStruct((t, d), F32), jax.ShapeDtypeStruct((t, d), BF16)),
        grid=(t // tm,),
        in_specs=[row(k), pl.BlockSpec((k, d), lambda i: (0, 0)), row(d), pl.BlockSpec((1, d), lambda i: (0, 0))],
        out_specs=(row(d), row(d)),
        compiler_params=_cparams(("arbitrary",)),
        name="out_norm",
    )(y, w_bf16, x2d, g)


def _mm_kernel(a_ref, w_ref, *rest, act, has_res, w_rows_are_outputs):
    if has_res:
        r_ref, o_ref = rest
    else:
        (o_ref,) = rest
    contract = (((1,), (1,)), ((), ())) if w_rows_are_outputs else (((1,), (0,)), ((), ()))
    acc = lax.dot_general(a_ref[...], w_ref[...].astype(BF16), contract, preferred_element_type=F32)
    if act == "sigmoid":
        acc = 0.5 * jnp.tanh(0.5 * acc) + 0.5
    elif act == "relu2":
        r = jnp.maximum(acc, 0.0)
        acc = r * r
    if has_res:
        acc = acc + r_ref[...]
    o_ref[...] = acc.astype(o_ref.dtype)


def _matmul(a, w, *, out_dtype, tile, act=None, residual=None, w_rows_are_outputs=False,
            w_row_ranges=None, name="matmul"):
    m, k = a.shape
    if w_row_ranges is not None:
        assert w_rows_are_outputs
        n = sum(r for _, r in w_row_ranges)
    else:
        n = w.shape[0] if w_rows_are_outputs else w.shape[1]
    tm = _tile(m, tile[0])
    tn = _tile(n, tile[1])
    if w_row_ranges is not None:
        assert all(r % tn == 0 and f % 8 == 0 for f, r in w_row_ranges)

        def w_rows(i, j):
            start, first_blk = jnp.int32(0), 0
            for f, r in w_row_ranges:
                start = jnp.where(j >= first_blk, f + (j - first_blk) * tn, start)
                first_blk += r // tn
            return pl.multiple_of(start, 8), 0

        w_spec = pl.BlockSpec((pl.Element(tn), pl.Element(k)), w_rows)
    elif w_rows_are_outputs:
        w_spec = pl.BlockSpec((tn, k), lambda i, j: (j, 0))
    else:
        w_spec = pl.BlockSpec((k, tn), lambda i, j: (0, j))
    in_specs = [pl.BlockSpec((tm, k), lambda i, j: (i, 0)), w_spec]
    args = [a, w]
    if residual is not None:
        in_specs.append(pl.BlockSpec((tm, tn), lambda i, j: (i, j)))
        args.append(residual)
    return pl.pallas_call(
        functools.partial(_mm_kernel, act=act, has_res=residual is not None,
                          w_rows_are_outputs=w_rows_are_outputs),
        out_shape=jax.ShapeDtypeStruct((m, n), out_dtype),
        grid=(m // tm, n // tn),
        in_specs=in_specs,
        out_specs=pl.BlockSpec((tm, tn), lambda i, j: (i, j)),
        compiler_params=_cparams(("arbitrary", "arbitrary")),
        name=name,
    )(*args)


def _mm_streamed_kernel(a_hbm, w_ref, r_ref, o_ref, a_buf, sem, *, tm, n_row_tiles, n_parts):
    i = pl.program_id(0)
    j = pl.program_id(1)
    slot = i % 2
    slab = tm // n_parts

    def slab_copy(row_tile, part, to_slot):
        rows = pl.ds(row_tile * tm + part * slab, slab)
        return pltpu.make_async_copy(a_hbm.at[rows, :], a_buf.at[to_slot, pl.ds(part * slab, slab), :],
                                     sem.at[to_slot])

    @pl.when((i == 0) & (j == 0))
    def _():
        for part in range(n_parts):
            slab_copy(0, part, 0).start()

    @pl.when(j == 0)
    def _():
        for part in range(n_parts):
            slab_copy(i, part, slot).wait()

    @pl.when(i + 1 < n_row_tiles)
    def _():
        slab_copy(i + 1, j, 1 - slot).start()

    acc = jnp.dot(a_buf[slot], w_ref[...], preferred_element_type=F32)
    o_ref[...] = (acc + r_ref[...]).astype(o_ref.dtype)


def _matmul_streamed(a, w_bf16, residual, *, tile, name):
    m, k = a.shape
    n = w_bf16.shape[1]
    tm = _tile(m, tile[0])
    tn = _tile(n, tile[1])
    n_parts = n // tn
    assert tm % n_parts == 0 and (tm // n_parts) % (2 * SUBLANES) == 0
    return pl.pallas_call(
        functools.partial(_mm_streamed_kernel, tm=tm, n_row_tiles=m // tm, n_parts=n_parts),
        out_shape=jax.ShapeDtypeStruct((m, n), residual.dtype),
        grid=(m // tm, n_parts),
        in_specs=[
            pl.BlockSpec(memory_space=pl.ANY),
            pl.BlockSpec((k, tn), lambda i, j: (0, j)),
            pl.BlockSpec((tm, tn), lambda i, j: (i, j)),
        ],
        out_specs=pl.BlockSpec((tm, tn), lambda i, j: (i, j)),
        scratch_shapes=[pltpu.VMEM((2, tm, k), a.dtype), pltpu.SemaphoreType.DMA((2,))],
        compiler_params=_cparams(("arbitrary", "arbitrary")),
        name=name,
    )(a, w_bf16, residual)


def _gdn_kernel(q_ref, k_ref, v_ref, z_ref, sm_ref, cwq_ref, cwk_ref, cwv_ref, gn_ref, o_ref,
                state_ref, tail_ref, ext_ref, qs, ks, vs, *, ts, nc, hb, nb):
    hg = pl.program_id(0)
    si = pl.program_id(1)
    c_sz = GDN_CHUNK

    @pl.when(si == 0)
    def _():
        state_ref[...] = jnp.zeros_like(state_ref)
        tail_ref[...] = jnp.zeros_like(tail_ref)

    def conv_silu(x_ref, w_ref, bi, idx, c):
        r0 = c * c_sz
        x = x_ref[bi, r0:r0 + c_sz, :]
        w = w_ref[...]
        y = x * w[0:1, :]
        for i in range(1, GDN_CONV):
            y = pltpu.roll(y, 1, 0) + x * w[i:i + 1, :]
        sl = SUBLANES
        ext_ref[idx, 0:sl, :] = tail_ref[bi, idx] if c == 0 else x_ref[bi, r0 - sl:r0, :]
        ext_ref[idx, sl:2 * sl, :] = x[0:sl, :]
        first = sl - (GDN_CONV - 1)
        head = ext_ref[idx, first:first + sl, :] * w[0:1, :]
        for i in range(1, GDN_CONV):
            head = head + ext_ref[idx, first + i:first + i + sl, :] * w[i:i + 1, :]
        y = jnp.concatenate([head, y[sl:, :]], axis=0)
        if c == nc - 1:
            tail_ref[bi, idx] = x[c_sz - sl:c_sz, :]
        return y * jax.nn.sigmoid(y)

    def prepare(c):
        rows = slice(c * c_sz, (c + 1) * c_sz)
        for bi in range(nb):
            q = conv_silu(q_ref, cwq_ref, bi, 0, c)
            k = conv_silu(k_ref, cwk_ref, bi, 1, c)
            vs[bi, rows, :] = conv_silu(v_ref, cwv_ref, bi, 2, c)
            for hh in range(hb):
                cs = slice(hh * LANES, (hh + 1) * LANES)
                qh = q[:, cs]
                kh = k[:, cs]
                qs[bi, rows, cs] = qh * (lax.rsqrt(jnp.sum(qh * qh, axis=-1, keepdims=True) + EPS)
                                         * (GDN_DK ** -0.5))
                ks[bi, rows, cs] = kh * lax.rsqrt(jnp.sum(kh * kh, axis=-1, keepdims=True) + EPS)

    ri = lax.broadcasted_iota(jnp.int32, (c_sz, c_sz), 0)
    ci = lax.broadcasted_iota(jnp.int32, (c_sz, c_sz), 1)
    strict = ri > ci
    incl = ri >= ci
    eye = jnp.where(ri == ci, 1.0, 0.0)
    n_lvl = c_sz.bit_length() - 1
    lvl_masks = []
    for l in range(n_lvl):
        same = (ri >> (l + 1)) == (ci >> (l + 1))
        lvl_masks.append(jnp.where(
            same, jnp.where(((ri >> l) & 1) == 1, jnp.where(((ci >> l) & 1) == 0, 1.0, 0.0), 0.0), 0.0))

    def advance(c):
        rows = slice(c * c_sz, (c + 1) * c_sz)
        chains = [(bi, hh) for bi in range(nb) for hh in range(hb)]
        mlows, rhss, lhs2s, qds, cds = [], [], [], [], []
        for bi, hh in chains:
            cs = slice(hh * LANES, (hh + 1) * LANES)
            hd = hg * hb + hh
            smc = sm_ref[bi, rows, :]
            beta = _lane_col(smc, hd)
            gam = _lane_col(smc, hd + GDN_HEADS)
            kc = ks[bi, rows, cs]
            qc = qs[bi, rows, cs]
            egam = jnp.exp(gam)
            g_last = gam[c_sz - 1:c_sz, :]
            gcol = jnp.broadcast_to(gam, (c_sz, c_sz))
            diff = gcol - gcol.T
            e = jnp.exp(jnp.where(incl, diff, 0.0))
            mlows.append(beta * _dot_nt(kc, kc) * jnp.where(strict, e, 0.0))
            qk = _dot_nt(qc, kc) * jnp.where(incl, e, 0.0)
            kd_t = (kc * jnp.exp(g_last - gam)).T
            lhs2s.append(jnp.concatenate([qk, kd_t], axis=0).astype(BF16))
            rhss.append(jnp.concatenate([kc * (beta * egam), vs[bi, rows, cs] * beta], axis=1).astype(BF16))
            qds.append(qc * egam)
            cds.append(jnp.exp(g_last))
        xs = [eye - lvl_masks[0] * m for m in mlows]
        mlows_bf = [m.astype(BF16) for m in mlows]
        for l in range(1, n_lvl):
            mask_bf = lvl_masks[l].astype(BF16)
            tl = [_dot(mask_bf * m, x) for m, x in zip(mlows_bf, xs)]
            xs = [x - _dot(x, t) for x, t in zip(xs, tl)]
        wus = [_dot(x, r) for x, r in zip(xs, rhss)]
        for n, (bi, hh) in enumerate(chains):
            cs = slice(hh * LANES, (hh + 1) * LANES)
            state = state_ref[n]
            wu = wus[n]
            a = _dot(jnp.concatenate([wu[:, :GDN_DK], qds[n]], axis=0), state)
            u = wu[:, GDN_DK:] - a[:c_sz]
            b2 = jnp.dot(lhs2s[n], u.astype(BF16), preferred_element_type=F32)
            o = a[c_sz:] + b2[:c_sz]
            state_ref[n] = state * cds[n] + b2[c_sz:]
            z = z_ref[bi, rows, cs]
            on = o * lax.rsqrt(jnp.mean(o * o, axis=-1, keepdims=True) + EPS) * gn_ref[...]
            o_ref[bi, rows, cs] = (on * (z * jax.nn.sigmoid(z))).astype(o_ref.dtype)

    for c in range(nc):
        prepare(c)
        advance(c)


def _gdn(qkvz, sm, conv_w, gn):
    b, s, _ = qkvz.shape
    nh = GDN_HEADS
    hb = GDN_HEADS_PER_STEP
    ng = nh // hb
    wb = hb * LANES
    ts = _tile(s, GDN_ROWS_PER_STEP)
    nc = ts // GDN_CHUNK
    blk = lambda off: pl.BlockSpec((b, ts, wb), lambda h, j: (0, j, off + h))
    cw = lambda off: pl.BlockSpec((GDN_CONV, wb), lambda h, j: (0, off + h))
    return pl.pallas_call(
        functools.partial(_gdn_kernel, ts=ts, nc=nc, hb=hb, nb=b),
        out_shape=jax.ShapeDtypeStruct((b, s, nh * GDN_DV), BF16),
        grid=(ng, s // ts),
        in_specs=[
            blk(0), blk(ng), blk(2 * ng), blk(3 * ng),
            pl.BlockSpec((b, ts, LANES), lambda h, j: (0, j, 0)),
            cw(0), cw(ng), cw(2 * ng),
            pl.BlockSpec((1, GDN_DV), lambda h, j: (0, 0)),
        ],
        out_specs=pl.BlockSpec((b, ts, wb), lambda h, j: (0, j, h)),
        scratch_shapes=[
            pltpu.VMEM((b * hb, GDN_DK, GDN_DV), F32),
            pltpu.VMEM((b, 3, SUBLANES, wb), F32),
            pltpu.VMEM((3, 2 * SUBLANES, wb), F32),
            pltpu.VMEM((b, ts, wb), F32),
            pltpu.VMEM((b, ts, wb), F32),
            pltpu.VMEM((b, ts, wb), F32),
        ],
        compiler_params=_cparams(("arbitrary", "arbitrary")),
        name="gdn",
    )(qkvz, qkvz, qkvz, qkvz, sm, conv_w, conv_w, conv_w, gn)


def _fox_kernel(q_ref, k_ref, v_ref, smq_ref, smk_ref, gq_ref, gk_ref, o_ref,
                kaug_ref, vaug_ref, m_s, acc_s, sa_ref, sb_ref, *, tq, tk, nk):
    hd = pl.program_id(1)
    qi = pl.program_id(2)
    n_sub = tq // tk
    n_lt = tk // LANES

    src = lax.broadcasted_iota(jnp.int32, (LANES, LANES), 0)
    dst = lax.broadcasted_iota(jnp.int32, (LANES, LANES), 1)
    first_piece = 2 * GDN_HEADS + hd

    def bias_cols(sm_block, dst0, sign, ones0):
        pick = jnp.where((dst >= dst0) & (dst < dst0 + 3) & (src == first_piece + FOX_HEADS * (dst - dst0)),
                         sign, 0.0).astype(BF16)
        lane = lax.broadcasted_iota(jnp.int32, sm_block.shape, 1)
        ones = jnp.where((lane >= ones0) & (lane < ones0 + 3), 1.0, 0.0)
        return (jnp.dot(sm_block.astype(BF16), pick, preferred_element_type=F32) + ones).astype(BF16)

    @pl.when(qi == 0)
    def _():
        def build(j, carry):
            rows = pl.ds(pl.multiple_of(j * tk, tk), tk)
            kk = k_ref[0, rows, :].astype(F32)
            kn = kk * lax.rsqrt(jnp.mean(kk * kk, axis=-1, keepdims=True) + EPS) * gk_ref[...]
            kaug_ref[rows, 0:FOX_DH] = kn.astype(BF16)
            kaug_ref[rows, FOX_DH:2 * FOX_DH] = bias_cols(smk_ref[0, rows, :], 3, -1.0, 0)
            vaug_ref[rows, 0:FOX_DH] = v_ref[0, rows, :]
            vaug_ref[rows, FOX_DH:2 * FOX_DH] = jnp.ones((tk, FOX_DH), BF16)
            return carry
        lax.fori_loop(0, nk, build, 0)

    qq = q_ref[0].astype(F32)
    qn = qq * (lax.rsqrt(jnp.mean(qq * qq, axis=-1, keepdims=True) + EPS) * (FOX_DH ** -0.5 * LOG2E)) * gq_ref[...]
    q_aug = jnp.concatenate([qn.astype(BF16), bias_cols(smq_ref[0], 0, 1.0, 3)], axis=1)

    m_s[...] = jnp.full_like(m_s, NEG_BIG)
    acc_s[...] = jnp.zeros_like(acc_s)

    def scores(s_ref, j):
        rows = pl.ds(pl.multiple_of(j * tk, tk), tk)
        s_ref[...] = lax.dot_general(q_aug, kaug_ref[rows, :], (((1,), (1,)), ((), ())),
                                     preferred_element_type=F32)

    def accumulate(s_ref, j, diag):
        rows = pl.ds(pl.multiple_of(j * tk, tk), tk)
        v_blk = vaug_ref[rows, :]
        groups = ([(0, tq, False)] if diag is None
                  else [(r * tk, (r + 1) * tk, r == diag) for r in range(diag, n_sub)])
        for lo_row, hi_row, masked in groups:
            rs = slice(lo_row, hi_row)
            tiles = [s_ref[rs, c * LANES:(c + 1) * LANES] for c in range(n_lt)]
            if masked:
                ri = lax.broadcasted_iota(jnp.int32, (tk, LANES), 0)
                ci = lax.broadcasted_iota(jnp.int32, (tk, LANES), 1)
                tiles = [jnp.where(ri >= ci + c * LANES, t, NEG_BIG) for c, t in enumerate(tiles)]
            mx = tiles[0]
            for t in tiles[1:]:
                mx = jnp.maximum(mx, t)
            m_prev = m_s[rs, :]
            m_next = jnp.maximum(m_prev, jnp.max(mx, axis=1, keepdims=True))
            p = jnp.concatenate([jnp.exp2(t - m_next) for t in tiles], axis=1).astype(BF16)
            alpha = jnp.exp2(m_prev - m_next)
            acc_s[rs, :] = (jnp.concatenate([alpha, alpha], axis=1) * acc_s[rs, :]
                            + jnp.dot(p, v_blk, preferred_element_type=F32))
            m_s[rs, :] = m_next

    scores(sa_ref, 0)

    def body(i, carry):
        j = 2 * i
        scores(sb_ref, j + 1)
        accumulate(sa_ref, j, None)
        scores(sa_ref, j + 2)
        accumulate(sb_ref, j + 1, None)
        return carry

    n_below = n_sub * qi
    lax.fori_loop(0, n_below // 2, body, 0)
    bufs = (sa_ref, sb_ref)
    for e in range(n_sub):
        if e + 1 < n_sub:
            scores(bufs[(e + 1) % 2], n_below + e + 1)
        accumulate(bufs[e % 2], n_below + e, e)

    acc = acc_s[...]
    o_ref[0] = (acc[:, :FOX_DH] / acc[:, FOX_DH:]).astype(o_ref.dtype)


def _fox(qkv, sm, gq, gk):
    b, s, _ = qkv.shape
    nh = FOX_HEADS
    tq = _tile(s, FOX_TQ)
    tk = _tile(tq, FOX_TK)
    assert (tq // tk) % 2 == 0, "the score pipeline consumes key blocks in pairs"
    nk = s // tk
    return pl.pallas_call(
        functools.partial(_fox_kernel, tq=tq, tk=tk, nk=nk),
        out_shape=jax.ShapeDtypeStruct((b, s, nh * FOX_DH), BF16),
        grid=(b, nh, s // tq),
        in_specs=[
            pl.BlockSpec((1, tq, FOX_DH), lambda i, h, j: (i, j, h)),
            pl.BlockSpec((1, s, FOX_DH), lambda i, h, j: (i, 0, nh + h)),
            pl.BlockSpec((1, s, FOX_DH), lambda i, h, j: (i, 0, 2 * nh + h)),
            pl.BlockSpec((1, tq, LANES), lambda i, h, j: (i, j, 0)),
            pl.BlockSpec((1, s, LANES), lambda i, h, j: (i, 0, 0)),
            pl.BlockSpec((1, FOX_DH), lambda i, h, j: (0, 0)),
            pl.BlockSpec((1, FOX_DH), lambda i, h, j: (0, 0)),
        ],
        out_specs=pl.BlockSpec((1, tq, FOX_DH), lambda i, h, j: (i, j, h)),
        scratch_shapes=[
            pltpu.VMEM((s, 2 * FOX_DH), BF16),
            pltpu.VMEM((s, 2 * FOX_DH), BF16),
            pltpu.VMEM((tq, LANES), F32),
            pltpu.VMEM((tq, 2 * FOX_DH), F32),
            pltpu.VMEM((tq, tk), F32),
            pltpu.VMEM((tq, tk), F32),
        ],
        compiler_params=_cparams(("arbitrary", "arbitrary", "arbitrary")),
        name="fox",
    )(qkv, qkv, qkv, sm, sm, gq, gk)


def _memkv_kernel(mem_ref, g_ref, w_ref, gk_ref, o_ref, *, n_k_tiles):
    j = pl.program_id(1)
    m = mem_ref[0]
    hn = (m * lax.rsqrt(jnp.mean(m * m, axis=-1, keepdims=True) + EPS) * g_ref[...]).astype(BF16)
    r = jnp.dot(hn, w_ref[...].astype(BF16), preferred_element_type=F32)
    parts = []
    for t in range(r.shape[1] // MEM_DH):
        rt = r[:, t * MEM_DH:(t + 1) * MEM_DH]
        parts.append(rt * lax.rsqrt(jnp.mean(rt * rt, axis=-1, keepdims=True) + EPS) * gk_ref[...])
    normed = jnp.concatenate(parts, axis=1)
    is_k = jnp.where(j < n_k_tiles, 1.0, 0.0)
    o_ref[0] = (is_k * normed + (1.0 - is_k) * r).astype(o_ref.dtype)


def _memkv(mem, g, w, gk):
    b, ml, d = mem.shape
    n = w.shape[1]
    tn = 2 * MEM_DH
    return pl.pallas_call(
        functools.partial(_memkv_kernel, n_k_tiles=(n // 2) // tn),
        out_shape=jax.ShapeDtypeStruct((b, ml, n), BF16),
        grid=(b, n // tn),
        in_specs=[
            pl.BlockSpec((1, ml, d), lambda i, j: (i, 0, 0)),
            pl.BlockSpec((1, d), lambda i, j: (0, 0)),
            pl.BlockSpec((d, tn), lambda i, j: (0, j)),
            pl.BlockSpec((1, MEM_DH), lambda i, j: (0, 0)),
        ],
        out_specs=pl.BlockSpec((1, ml, tn), lambda i, j: (i, 0, j)),
        compiler_params=_cparams(("arbitrary", "arbitrary")),
        name="memkv",
    )(mem, g, w, gk)


def _mem_kernel(q_ref, k_ref, v_ref, gq_ref, o_ref):
    for hd in range(MEM_HEADS):
        cs = slice(hd * MEM_DH, (hd + 1) * MEM_DH)
        qq = q_ref[0, :, cs].astype(F32)
        qn = qq * lax.rsqrt(jnp.mean(qq * qq, axis=-1, keepdims=True) + EPS) * gq_ref[...] * (MEM_DH ** -0.5)
        s = _dot_nt(qn, k_ref[0, :, cs])
        p = jnp.exp(s - jnp.max(s, axis=1, keepdims=True))
        p = p / jnp.sum(p, axis=1, keepdims=True)
        o_ref[0, :, cs] = jnp.dot(p.astype(BF16), v_ref[0, :, cs], preferred_element_type=F32).astype(o_ref.dtype)


def _mem_attn(qsrc, q_col_block, kv, gq):
    b, s, _ = qsrc.shape
    ml = kv.shape[1]
    wq = MEM_HEADS * MEM_DH
    tq = _tile(s, ROW_TILE)
    return pl.pallas_call(
        _mem_kernel,
        out_shape=jax.ShapeDtypeStruct((b, s, wq), BF16),
        grid=(b, s // tq),
        in_specs=[
            pl.BlockSpec((1, tq, wq), lambda i, j: (i, j, q_col_block)),
            pl.BlockSpec((1, ml, wq), lambda i, j: (i, 0, 0)),
            pl.BlockSpec((1, ml, wq), lambda i, j: (i, 0, 1)),
            pl.BlockSpec((1, MEM_DH), lambda i, j: (0, 0)),
        ],
        out_specs=pl.BlockSpec((1, tq, wq), lambda i, j: (i, j, 0)),
        compiler_params=_cparams(("arbitrary", "arbitrary")),
        name="mem_attn",
    )(qsrc, kv, kv, gq)


def _merge_kernel(oa_ref, ob_ref, om_ref, wa_ref, wb_ref, wm_ref, ga_ref, gb_ref, gm_ref, y_ref):
    d = lambda o_ref, w_ref: jnp.dot(o_ref[...], w_ref[...].astype(BF16), preferred_element_type=F32)
    y = (ga_ref[...].astype(F32) * d(oa_ref, wa_ref)
         + gb_ref[...].astype(F32) * d(ob_ref, wb_ref)
         + gm_ref[...].astype(F32) * d(om_ref, wm_ref))
    y_ref[...] = y.astype(y_ref.dtype)


def _merge(oa, ob, om, wa, wb, wm, gates):
    t, ka = oa.shape
    dm = wa.shape[1]
    tm = _tile(t, MERGE_TILE[0])
    tn = _tile(dm, MERGE_TILE[1])
    nb = dm // tn
    a_spec = lambda kk: pl.BlockSpec((tm, kk), lambda i, j: (i, 0))
    w_spec = lambda kk: pl.BlockSpec((kk, tn), lambda i, j: (0, j))
    g_spec = lambda off: pl.BlockSpec((tm, tn), lambda i, j: (i, off * nb + j))
    return pl.pallas_call(
        _merge_kernel,
        out_shape=jax.ShapeDtypeStruct((t, dm), BF16),
        grid=(t // tm, nb),
        in_specs=[a_spec(ka), a_spec(ob.shape[1]), a_spec(om.shape[1]),
                  w_spec(ka), w_spec(ob.shape[1]), w_spec(om.shape[1]),
                  g_spec(0), g_spec(1), g_spec(2)],
        out_specs=pl.BlockSpec((tm, tn), lambda i, j: (i, j)),
        compiler_params=_cparams(("arbitrary", "arbitrary")),
        name="merge",
    )(oa, ob, om, wa, wb, wm, gates, gates, gates)


def _layer(x, mem, g_mix, w_in, conv_w, a_log, dt_bias, gdn_norm_g, fox_b_f, fox_q_norm, fox_k_norm,
           g_mem, w_mem_kv, mem_q_norm, mem_k_norm, w_up_gdn, w_up_fox, w_up_mem, w_out, g_mlp, w_ff1, w_ff2):
    b, s, d = x.shape
    t = b * s
    gdn_qk = GDN_HEADS * GDN_DK
    gdn_v = GDN_HEADS * GDN_DV
    fox_w = FOX_HEADS * FOX_DH
    mem_w = MEM_HEADS * MEM_DH
    o_z = 2 * gdn_qk + gdn_v
    o_beta = o_z + gdn_v
    o_dec = o_beta + GDN_HEADS
    o_fq = o_dec + GDN_HEADS
    o_ff = o_fq + 3 * fox_w
    o_mq = o_ff + FOX_HEADS
    o_gate = o_mq + mem_w

    wt = w_in.T
    n_small = 2 * GDN_HEADS + FOX_HEADS
    w_small = jnp.concatenate([wt[o_beta:o_fq], wt[o_ff:o_mq],
                               jnp.zeros((LANES - n_small, d), F32)], axis=0)
    zpad = jnp.zeros((LANES - n_small,), F32)
    bias = jnp.concatenate([jnp.zeros((GDN_HEADS,), F32), dt_bias.astype(F32), fox_b_f.astype(F32), zpad])[None, :]
    alog = jnp.concatenate([jnp.zeros((GDN_HEADS,), F32), a_log.astype(F32), jnp.zeros((FOX_HEADS,), F32), zpad])[None, :]

    h, sm = _norm_small(x, g_mix[None, :], w_small, bias, alog)
    h2d = h.reshape(t, d)

    proj = functools.partial(_matmul, h2d, wt, tile=PROJ_TILE, w_rows_are_outputs=True)
    qkvz = proj(out_dtype=F32, w_row_ranges=[(0, o_beta)], name="proj_gdn").reshape(b, s, -1)
    att = proj(out_dtype=BF16, w_row_ranges=[(o_fq, 3 * fox_w), (o_mq, mem_w)],
               name="proj_att").reshape(b, s, -1)
    gates = proj(out_dtype=BF16, act="sigmoid", w_row_ranges=[(o_gate, N_BRANCH * d)], name="proj_gate")

    o_a = _gdn(qkvz, sm, conv_w, gdn_norm_g[None, :])
    o_b = _fox(att, sm, fox_q_norm[None, :], fox_k_norm[None, :])
    kv_m = _memkv(mem, g_mem[None, :], w_mem_kv, mem_k_norm[None, :])
    o_m = _mem_attn(att, (3 * fox_w) // mem_w, kv_m, mem_q_norm[None, :])

    y = _merge(o_a.reshape(t, -1), o_b.reshape(t, -1), o_m.reshape(t, -1), w_up_gdn, w_up_fox, w_up_mem, gates)
    x1, h2 = _out_norm(y, w_out.astype(BF16), x.reshape(t, d), g_mlp[None, :])
    u = _matmul(h2, w_ff1, out_dtype=BF16, tile=FF1_TILE, act="relu2", name="ff1")
    out = _matmul_streamed(u, w_ff2.astype(BF16), x1, tile=FF2_TILE, name="ff2")
    return out.reshape(b, s, d)


def kernel(x, mem, g_mix, w_in, conv_w, a_log, dt_bias, gdn_norm_g, fox_b_f, fox_q_norm, fox_k_norm, g_mem, w_mem_kv, mem_q_norm, mem_k_norm, w_up_gdn, w_up_fox, w_up_mem, w_out, g_mlp, w_ff1, w_ff2):
    depth = w_in.shape[0]
    for l in range(depth):
        x = _layer(x, mem, g_mix[l], w_in[l], conv_w[l], a_log[l], dt_bias[l], gdn_norm_g[l], fox_b_f[l],
                   fox_q_norm[l], fox_k_norm[l], g_mem[l], w_mem_kv[l], mem_q_norm[l], mem_k_norm[l],
                   w_up_gdn[l], w_up_fox[l], w_up_mem[l], w_out[l], g_mlp[l], w_ff1[l], w_ff2[l])
    return x
```

```python
import functools
import math

import jax
import jax.numpy as jnp
from jax import lax
from jax.experimental import pallas as pl
from jax.experimental.pallas import tpu as pltpu

F32 = jnp.float32
BF16 = jnp.bfloat16
EPS = 1e-6

GDN_HEADS = 8
GDN_DK = 128
GDN_DV = 128
GDN_CONV = 4
GDN_CHUNK = 128
GDN_HEADS_PER_STEP = 8
GDN_ROWS_PER_STEP = 256
FOX_HEADS = 8
FOX_DH = 128
FOX_TQ = 1024
FOX_TK = 512
MEM_HEADS = 4
MEM_DH = 256
N_BRANCH = 3
LANES = 128
SUBLANES = 8
NEG_BIG = -1e30
LOG2E = math.log2(math.e)

V7X_VMEM_BYTES = 64 * 1024 * 1024
VMEM_LIMIT = V7X_VMEM_BYTES - 8 * 1024 * 1024

PROJ_TILE = (2048, 512)
FF1_TILE = (2048, 512)
FF2_TILE = (1024, 256)
MERGE_TILE = (2048, 256)
ROW_TILE = 512
LHS_SLABS = 8


def _tile(n, pref):
    return pref if n % pref == 0 else n


def _cparams(sem):
    return pltpu.CompilerParams(dimension_semantics=sem, vmem_limit_bytes=VMEM_LIMIT)


def _dot(a, b):
    return jnp.dot(a.astype(BF16), b.astype(BF16), preferred_element_type=F32)


def _dot_nt(a, b):
    return lax.dot_general(a.astype(BF16), b.astype(BF16), (((1,), (1,)), ((), ())),
                           preferred_element_type=F32)


def _split2(a):
    hi = a.astype(BF16)
    lo = (a - hi.astype(F32)).astype(BF16)
    return hi, lo


def _split3(a):
    hi = a.astype(BF16).astype(F32)
    r = a - hi
    mid = r.astype(BF16).astype(F32)
    lo = (r - mid).astype(BF16).astype(F32)
    return hi, mid, lo


def _dot_exact_lhs(l_bf16, v):
    hi, mid, lo = _split3(v)
    n = v.shape[1]
    d = functools.partial(jnp.dot, preferred_element_type=F32)
    both = d(l_bf16, jnp.concatenate([hi, mid], axis=1).astype(BF16))
    return both[:, :n] + both[:, n:] + d(l_bf16, lo.astype(BF16))


def _lane_col(a, idx):
    lane = lax.broadcasted_iota(jnp.int32, a.shape, 1)
    return jnp.sum(jnp.where(lane == idx, a, 0.0), axis=1, keepdims=True)


def _softplus(x):
    return jnp.maximum(x, 0.0) + jnp.log1p(jnp.exp(-jnp.abs(x)))


def _norm_small_kernel(x_ref, g_ref, ws_ref, bias_ref, alog_ref, h_ref, sm_ref,
                       carry_ref, wsplit_ref, lfull_ref, lblk_ref, *, tm):
    s = pl.program_id(1)

    @pl.when(s == 0)
    def _():
        carry_ref[...] = jnp.zeros_like(carry_ref)

    @pl.when((pl.program_id(0) == 0) & (s == 0))
    def _():
        w_hi, w_lo = _split2(ws_ref[...])
        wsplit_ref[0:LANES, :] = w_hi
        wsplit_ref[LANES:2 * LANES, :] = w_lo
        row = lax.broadcasted_iota(jnp.int32, (tm, tm), 0)
        col = lax.broadcasted_iota(jnp.int32, (tm, tm), 1)
        low = col <= row
        sh = GDN_CHUNK.bit_length() - 1
        lfull_ref[...] = jnp.where(low, 1.0, 0.0).astype(BF16)
        lblk_ref[...] = jnp.where(low, jnp.where((row >> sh) == (col >> sh), 1.0, 0.0), 0.0).astype(BF16)

    x = x_ref[0]
    h = x * lax.rsqrt(jnp.mean(x * x, axis=-1, keepdims=True) + EPS) * g_ref[...]
    h_hi, h_lo = _split2(h)
    h_ref[0] = h_hi

    nt = lambda p, q: lax.dot_general(p, q, (((1,), (1,)), ((), ())), preferred_element_type=F32)
    both = nt(h_hi, wsplit_ref[...])
    pre = both[:, :LANES] + both[:, LANES:] + nt(h_lo, wsplit_ref[0:LANES, :]) + bias_ref[...]
    lane = lax.broadcasted_iota(jnp.int32, pre.shape, 1)
    nh = GDN_HEADS
    beta = jax.nn.sigmoid(pre)
    gdec = -jnp.exp(alog_ref[...]) * _softplus(pre)
    logf = -_softplus(-pre)
    vals = jnp.where(lane < nh, beta, jnp.where(lane < 2 * nh, gdec, jnp.where(lane < 3 * nh, logf, 0.0)))

    cs_blk = _dot_exact_lhs(lblk_ref[...], vals)
    cs_full = _dot_exact_lhs(lfull_ref[...], vals) + carry_ref[0:1, :]
    hi, mid, lo = _split3(cs_full * LOG2E)
    pieces = jnp.where(lane < 3 * nh, hi, jnp.where(lane < 4 * nh, pltpu.roll(mid, nh, 1), pltpu.roll(lo, 2 * nh, 1)))
    sm_ref[0] = jnp.where(lane < nh, vals, jnp.where(lane < 2 * nh, cs_blk, jnp.where(lane < 5 * nh, pieces, 0.0)))
    carry_ref[...] = jnp.broadcast_to(cs_full[tm - 1:tm, :], carry_ref.shape)


def _norm_small(x, g, w_small, bias, alog):
    b, s, d = x.shape
    tm = _tile(s, ROW_TILE)
    return pl.pallas_call(
        functools.partial(_norm_small_kernel, tm=tm),
        out_shape=(jax.ShapeDtypeStruct((b, s, d), BF16), jax.ShapeDtypeStruct((b, s, LANES), F32)),
        grid=(b, s // tm),
        in_specs=[
            pl.BlockSpec((1, tm, d), lambda i, j: (i, j, 0)),
            pl.BlockSpec((1, d), lambda i, j: (0, 0)),
            pl.BlockSpec((LANES, d), lambda i, j: (0, 0)),
            pl.BlockSpec((1, LANES), lambda i, j: (0, 0)),
            pl.BlockSpec((1, LANES), lambda i, j: (0, 0)),
        ],
        out_specs=(
            pl.BlockSpec((1, tm, d), lambda i, j: (i, j, 0)),
            pl.BlockSpec((1, tm, LANES), lambda i, j: (i, j, 0)),
        ),
        scratch_shapes=[
            pltpu.VMEM((SUBLANES, LANES), F32),
            pltpu.VMEM((2 * LANES, d), BF16),
            pltpu.VMEM((tm, tm), BF16),
            pltpu.VMEM((tm, tm), BF16),
        ],
        compiler_params=_cparams(("arbitrary", "arbitrary")),
        name="norm_small",
    )(x, g, w_small, bias, alog)


def _out_norm_kernel(y_ref, w_ref, x_ref, g_ref, x1_ref, h_ref):
    x1 = x_ref[...] + jnp.dot(y_ref[...], w_ref[...], preferred_element_type=F32)
    x1_ref[...] = x1
    h_ref[...] = (x1 * lax.rsqrt(jnp.mean(x1 * x1, axis=-1, keepdims=True) + EPS) * g_ref[...]).astype(h_ref.dtype)


def _out_norm(y, w_bf16, x2d, g):
    t, d = x2d.shape
    k = y.shape[1]
    tm = _tile(t, ROW_TILE)
    row = lambda width: pl.BlockSpec((tm, width), lambda i: (i, 0))
    return pl.pallas_call(
        _out_norm_kernel,
        out_shape=(jax.ShapeDtypeStruct((t, d), F32), jax.ShapeDtypeStruct((t, d), BF16)),
        grid=(t // tm,),
        in_specs=[row(k), pl.BlockSpec((k, d), lambda i: (0, 0)), row(d), pl.BlockSpec((1, d), lambda i: (0, 0))],
        out_specs=(row(d), row(d)),
        compiler_params=_cparams(("arbitrary",)),
        name="out_norm",
    )(y, w_bf16, x2d, g)


def _stream_row_tile(a_hbm, a_buf, sem, *, tm, n_row_tiles, n_slabs):
    i = pl.program_id(0)
    j = pl.program_id(1)
    slot = i % 2
    slab = tm // n_slabs

    def slab_copy(row_tile, part, to_slot):
        rows = pl.ds(row_tile * tm + part * slab, slab)
        return pltpu.make_async_copy(a_hbm.at[rows, :], a_buf.at[to_slot, pl.ds(part * slab, slab), :],
                                     sem.at[to_slot])

    @pl.when((i == 0) & (j == 0))
    def _():
        for part in range(n_slabs):
            slab_copy(0, part, 0).start()

    @pl.when(j == 0)
    def _():
        for part in range(n_slabs):
            slab_copy(i, part, slot).wait()

    @pl.when((i + 1 < n_row_tiles) & (j < n_slabs))
    def _():
        slab_copy(i + 1, j, 1 - slot).start()

    return slot


def _n_slabs(tm, n_col_steps):
    n = min(LHS_SLABS, n_col_steps)
    assert tm % n == 0 and (tm // n) % (2 * SUBLANES) == 0, "slabs must be whole packed-bf16 sublane tiles"
    return n


def _row_tile_scratch(tm, k, dtype):
    return [pltpu.VMEM((2, tm, k), dtype), pltpu.SemaphoreType.DMA((2,))]


def _mm_kernel(a_hbm, w_ref, *rest, act, has_res, w_rows_are_outputs, stream):
    if has_res:
        r_ref, o_ref, a_buf, sem = rest
    else:
        o_ref, a_buf, sem = rest
    slot = _stream_row_tile(a_hbm, a_buf, sem, **stream)
    contract = (((1,), (1,)), ((), ())) if w_rows_are_outputs else (((1,), (0,)), ((), ()))
    acc = lax.dot_general(a_buf[slot], w_ref[...].astype(BF16), contract, preferred_element_type=F32)
    if act == "sigmoid":
        acc = 0.5 * jnp.tanh(0.5 * acc) + 0.5
    elif act == "relu2":
        r = jnp.maximum(acc, 0.0)
        acc = r * r
    if has_res:
        acc = acc + r_ref[...]
    o_ref[...] = acc.astype(o_ref.dtype)


def _matmul(a, w, *, out_dtype, tile, act=None, residual=None, w_rows_are_outputs=False,
            w_row_ranges=None, name="matmul"):
    m, k = a.shape
    if w_row_ranges is not None:
        assert w_rows_are_outputs
        n = sum(r for _, r in w_row_ranges)
    else:
        n = w.shape[0] if w_rows_are_outputs else w.shape[1]
    tm = _tile(m, tile[0])
    tn = _tile(n, tile[1])
    if w_row_ranges is not None:
        assert all(r % tn == 0 and f % 8 == 0 for f, r in w_row_ranges)

        def w_rows(i, j):
            start, first_blk = jnp.int32(0), 0
            for f, r in w_row_ranges:
                start = jnp.where(j >= first_blk, f + (j - first_blk) * tn, start)
                first_blk += r // tn
            return pl.multiple_of(start, 8), 0

        w_spec = pl.BlockSpec((pl.Element(tn), pl.Element(k)), w_rows)
    elif w_rows_are_outputs:
        w_spec = pl.BlockSpec((tn, k), lambda i, j: (j, 0))
    else:
        w_spec = pl.BlockSpec((k, tn), lambda i, j: (0, j))
    in_specs = [pl.BlockSpec(memory_space=pl.ANY), w_spec]
    args = [a, w]
    if residual is not None:
        in_specs.append(pl.BlockSpec((tm, tn), lambda i, j: (i, j)))
        args.append(residual)
    stream = dict(tm=tm, n_row_tiles=m // tm, n_slabs=_n_slabs(tm, n // tn))
    return pl.pallas_call(
        functools.partial(_mm_kernel, act=act, has_res=residual is not None,
                          w_rows_are_outputs=w_rows_are_outputs, stream=stream),
        out_shape=jax.ShapeDtypeStruct((m, n), out_dtype),
        grid=(m // tm, n // tn),
        in_specs=in_specs,
        out_specs=pl.BlockSpec((tm, tn), lambda i, j: (i, j)),
        scratch_shapes=_row_tile_scratch(tm, k, a.dtype),
        compiler_params=_cparams(("arbitrary", "arbitrary")),
        name=name,
    )(*args)


def _gdn_kernel(q_ref, k_ref, v_ref, z_ref, sm_ref, cwq_ref, cwk_ref, cwv_ref, gn_ref, o_ref,
                state_ref, tail_ref, ext_ref, qs, ks, vs, *, ts, nc, hb, nb):
    hg = pl.program_id(0)
    si = pl.program_id(1)
    c_sz = GDN_CHUNK

    @pl.when(si == 0)
    def _():
        state_ref[...] = jnp.zeros_like(state_ref)
        tail_ref[...] = jnp.zeros_like(tail_ref)

    def conv_silu(x_ref, w_ref, bi, idx, c):
        r0 = c * c_sz
        x = x_ref[bi, r0:r0 + c_sz, :]
        w = w_ref[...]
        y = x * w[0:1, :]
        for i in range(1, GDN_CONV):
            y = pltpu.roll(y, 1, 0) + x * w[i:i + 1, :]
        sl = SUBLANES
        ext_ref[idx, 0:sl, :] = tail_ref[bi, idx] if c == 0 else x_ref[bi, r0 - sl:r0, :]
        ext_ref[idx, sl:2 * sl, :] = x[0:sl, :]
        first = sl - (GDN_CONV - 1)
        head = ext_ref[idx, first:first + sl, :] * w[0:1, :]
        for i in range(1, GDN_CONV):
            head = head + ext_ref[idx, first + i:first + i + sl, :] * w[i:i + 1, :]
        y = jnp.concatenate([head, y[sl:, :]], axis=0)
        if c == nc - 1:
            tail_ref[bi, idx] = x[c_sz - sl:c_sz, :]
        return y * jax.nn.sigmoid(y)

    def prepare(c):
        rows = slice(c * c_sz, (c + 1) * c_sz)
        for bi in range(nb):
            q = conv_silu(q_ref, cwq_ref, bi, 0, c)
            k = conv_silu(k_ref, cwk_ref, bi, 1, c)
            vs[bi, rows, :] = conv_silu(v_ref, cwv_ref, bi, 2, c)
            for hh in range(hb):
                cs = slice(hh * LANES, (hh + 1) * LANES)
                qh = q[:, cs]
                kh = k[:, cs]
                qs[bi, rows, cs] = qh * (lax.rsqrt(jnp.sum(qh * qh, axis=-1, keepdims=True) + EPS)
                                         * (GDN_DK ** -0.5))
                ks[bi, rows, cs] = kh * lax.rsqrt(jnp.sum(kh * kh, axis=-1, keepdims=True) + EPS)

    ri = lax.broadcasted_iota(jnp.int32, (c_sz, c_sz), 0)
    ci = lax.broadcasted_iota(jnp.int32, (c_sz, c_sz), 1)
    strict = ri > ci
    incl = ri >= ci
    eye = jnp.where(ri == ci, 1.0, 0.0)
    n_lvl = c_sz.bit_length() - 1
    lvl_masks = []
    for l in range(n_lvl):
        same = (ri >> (l + 1)) == (ci >> (l + 1))
        lvl_masks.append(jnp.where(
            same, jnp.where(((ri >> l) & 1) == 1, jnp.where(((ci >> l) & 1) == 0, 1.0, 0.0), 0.0), 0.0))

    def advance(c):
        rows = slice(c * c_sz, (c + 1) * c_sz)
        chains = [(bi, hh) for bi in range(nb) for hh in range(hb)]
        mlows, rhss, lhs2s, qds, cds = [], [], [], [], []
        for bi, hh in chains:
            cs = slice(hh * LANES, (hh + 1) * LANES)
            hd = hg * hb + hh
            smc = sm_ref[bi, rows, :]
            beta = _lane_col(smc, hd)
            gam = _lane_col(smc, hd + GDN_HEADS)
            kc = ks[bi, rows, cs]
            qc = qs[bi, rows, cs]
            egam = jnp.exp(gam)
            g_last = gam[c_sz - 1:c_sz, :]
            gcol = jnp.broadcast_to(gam, (c_sz, c_sz))
            diff = gcol - gcol.T
            e = jnp.exp(jnp.where(incl, diff, 0.0))
            mlows.append(beta * _dot_nt(kc, kc) * jnp.where(strict, e, 0.0))
            qk = _dot_nt(qc, kc) * jnp.where(incl, e, 0.0)
            kd_t = (kc * jnp.exp(g_last - gam)).T
            lhs2s.append(jnp.concatenate([qk, kd_t], axis=0).astype(BF16))
            rhss.append(jnp.concatenate([kc * (beta * egam), vs[bi, rows, cs] * beta], axis=1).astype(BF16))
            qds.append(qc * egam)
            cds.append(jnp.exp(g_last))
        xs = [eye - lvl_masks[0] * m for m in mlows]
        mlows_bf = [m.astype(BF16) for m in mlows]
        for l in range(1, n_lvl):
            mask_bf = lvl_masks[l].astype(BF16)
            tl = [_dot(mask_bf * m, x) for m, x in zip(mlows_bf, xs)]
            xs = [x - _dot(x, t) for x, t in zip(xs, tl)]
        wus = [_dot(x, r) for x, r in zip(xs, rhss)]
        for n, (bi, hh) in enumerate(chains):
            cs = slice(hh * LANES, (hh + 1) * LANES)
            state = state_ref[n]
            wu = wus[n]
            a = _dot(jnp.concatenate([wu[:, :GDN_DK], qds[n]], axis=0), state)
            u = wu[:, GDN_DK:] - a[:c_sz]
            b2 = jnp.dot(lhs2s[n], u.astype(BF16), preferred_element_type=F32)
            o = a[c_sz:] + b2[:c_sz]
            state_ref[n] = state * cds[n] + b2[c_sz:]
            z = z_ref[bi, rows, cs]
            on = o * lax.rsqrt(jnp.mean(o * o, axis=-1, keepdims=True) + EPS) * gn_ref[...]
            o_ref[bi, rows, cs] = (on * (z * jax.nn.sigmoid(z))).astype(o_ref.dtype)

    for c in range(nc):
        prepare(c)
        advance(c)


def _gdn(qkvz, sm, conv_w, gn):
    b, s, _ = qkvz.shape
    nh = GDN_HEADS
    hb = GDN_HEADS_PER_STEP
    ng = nh // hb
    wb = hb * LANES
    ts = _tile(s, GDN_ROWS_PER_STEP)
    nc = ts // GDN_CHUNK
    blk = lambda off: pl.BlockSpec((b, ts, wb), lambda h, j: (0, j, off + h))
    cw = lambda off: pl.BlockSpec((GDN_CONV, wb), lambda h, j: (0, off + h))
    return pl.pallas_call(
        functools.partial(_gdn_kernel, ts=ts, nc=nc, hb=hb, nb=b),
        out_shape=jax.ShapeDtypeStruct((b, s, nh * GDN_DV), BF16),
        grid=(ng, s // ts),
        in_specs=[
            blk(0), blk(ng), blk(2 * ng), blk(3 * ng),
            pl.BlockSpec((b, ts, LANES), lambda h, j: (0, j, 0)),
            cw(0), cw(ng), cw(2 * ng),
            pl.BlockSpec((1, GDN_DV), lambda h, j: (0, 0)),
        ],
        out_specs=pl.BlockSpec((b, ts, wb), lambda h, j: (0, j, h)),
        scratch_shapes=[
            pltpu.VMEM((b * hb, GDN_DK, GDN_DV), F32),
            pltpu.VMEM((b, 3, SUBLANES, wb), F32),
            pltpu.VMEM((3, 2 * SUBLANES, wb), F32),
            pltpu.VMEM((b, ts, wb), F32),
            pltpu.VMEM((b, ts, wb), F32),
            pltpu.VMEM((b, ts, wb), F32),
        ],
        compiler_params=_cparams(("arbitrary", "arbitrary")),
        name="gdn",
    )(qkvz, qkvz, qkvz, qkvz, sm, conv_w, conv_w, conv_w, gn)


def _fox_kernel(q_ref, k_ref, v_ref, smq_ref, smk_ref, gq_ref, gk_ref, o_ref,
                kaug_ref, vaug_ref, m_s, acc_s, sa_ref, sb_ref, *, tq, tk, nk):
    hd = pl.program_id(1)
    qi = pl.program_id(2)
    n_sub = tq // tk
    n_lt = tk // LANES

    src = lax.broadcasted_iota(jnp.int32, (LANES, LANES), 0)
    dst = lax.broadcasted_iota(jnp.int32, (LANES, LANES), 1)
    first_piece = 2 * GDN_HEADS + hd

    def bias_cols(sm_block, dst0, sign, ones0):
        pick = jnp.where((dst >= dst0) & (dst < dst0 + 3) & (src == first_piece + FOX_HEADS * (dst - dst0)),
                         sign, 0.0).astype(BF16)
        lane = lax.broadcasted_iota(jnp.int32, sm_block.shape, 1)
        ones = jnp.where((lane >= ones0) & (lane < ones0 + 3), 1.0, 0.0)
        return (jnp.dot(sm_block.astype(BF16), pick, preferred_element_type=F32) + ones).astype(BF16)

    @pl.when(qi == 0)
    def _():
        def build(j, carry):
            rows = pl.ds(pl.multiple_of(j * tk, tk), tk)
            kk = k_ref[0, rows, :].astype(F32)
            kn = kk * lax.rsqrt(jnp.mean(kk * kk, axis=-1, keepdims=True) + EPS) * gk_ref[...]
            kaug_ref[rows, 0:FOX_DH] = kn.astype(BF16)
            kaug_ref[rows, FOX_DH:2 * FOX_DH] = bias_cols(smk_ref[0, rows, :], 3, -1.0, 0)
            vaug_ref[rows, 0:FOX_DH] = v_ref[0, rows, :]
            vaug_ref[rows, FOX_DH:2 * FOX_DH] = jnp.ones((tk, FOX_DH), BF16)
            return carry
        lax.fori_loop(0, nk, build, 0)

    qq = q_ref[0].astype(F32)
    qn = qq * (lax.rsqrt(jnp.mean(qq * qq, axis=-1, keepdims=True) + EPS) * (FOX_DH ** -0.5 * LOG2E)) * gq_ref[...]
    q_aug = jnp.concatenate([qn.astype(BF16), bias_cols(smq_ref[0], 0, 1.0, 3)], axis=1)

    m_s[...] = jnp.full_like(m_s, NEG_BIG)
    acc_s[...] = jnp.zeros_like(acc_s)

    def scores(s_ref, j):
        rows = pl.ds(pl.multiple_of(j * tk, tk), tk)
        s_ref[...] = lax.dot_general(q_aug, kaug_ref[rows, :], (((1,), (1,)), ((), ())),
                                     preferred_element_type=F32)

    def accumulate(s_ref, j, diag):
        rows = pl.ds(pl.multiple_of(j * tk, tk), tk)
        v_blk = vaug_ref[rows, :]
        groups = ([(0, tq, False)] if diag is None
                  else [(r * tk, (r + 1) * tk, r == diag) for r in range(diag, n_sub)])
        for lo_row, hi_row, masked in groups:
            rs = slice(lo_row, hi_row)
            tiles = [s_ref[rs, c * LANES:(c + 1) * LANES] for c in range(n_lt)]
            if masked:
                ri = lax.broadcasted_iota(jnp.int32, (tk, LANES), 0)
                ci = lax.broadcasted_iota(jnp.int32, (tk, LANES), 1)
                tiles = [jnp.where(ri >= ci + c * LANES, t, NEG_BIG) for c, t in enumerate(tiles)]
            mx = tiles[0]
            for t in tiles[1:]:
                mx = jnp.maximum(mx, t)
            m_prev = m_s[rs, :]
            m_next = jnp.maximum(m_prev, jnp.max(mx, axis=1, keepdims=True))
            p = jnp.concatenate([jnp.exp2(t - m_next) for t in tiles], axis=1).astype(BF16)
            alpha = jnp.exp2(m_prev - m_next)
            acc_s[rs, :] = (jnp.concatenate([alpha, alpha], axis=1) * acc_s[rs, :]
                            + jnp.dot(p, v_blk, preferred_element_type=F32))
            m_s[rs, :] = m_next

    scores(sa_ref, 0)

    def body(i, carry):
        j = 2 * i
        scores(sb_ref, j + 1)
        accumulate(sa_ref, j, None)
        scores(sa_ref, j + 2)
        accumulate(sb_ref, j + 1, None)
        return carry

    n_below = n_sub * qi
    lax.fori_loop(0, n_below // 2, body, 0)
    bufs = (sa_ref, sb_ref)
    for e in range(n_sub):
        if e + 1 < n_sub:
            scores(bufs[(e + 1) % 2], n_below + e + 1)
        accumulate(bufs[e % 2], n_below + e, e)

    acc = acc_s[...]
    o_ref[0] = (acc[:, :FOX_DH] / acc[:, FOX_DH:]).astype(o_ref.dtype)


def _fox(qkv, sm, gq, gk):
    b, s, _ = qkv.shape
    nh = FOX_HEADS
    tq = _tile(s, FOX_TQ)
    tk = _tile(tq, FOX_TK)
    assert (tq // tk) % 2 == 0, "the score pipeline consumes key blocks in pairs"
    nk = s // tk
    return pl.pallas_call(
        functools.partial(_fox_kernel, tq=tq, tk=tk, nk=nk),
        out_shape=jax.ShapeDtypeStruct((b, s, nh * FOX_DH), BF16),
        grid=(b, nh, s // tq),
        in_specs=[
            pl.BlockSpec((1, tq, FOX_DH), lambda i, h, j: (i, j, h)),
            pl.BlockSpec((1, s, FOX_DH), lambda i, h, j: (i, 0, nh + h)),
            pl.BlockSpec((1, s, FOX_DH), lambda i, h, j: (i, 0, 2 * nh + h)),
            pl.BlockSpec((1, tq, LANES), lambda i, h, j: (i, j, 0)),
            pl.BlockSpec((1, s, LANES), lambda i, h, j: (i, 0, 0)),
            pl.BlockSpec((1, FOX_DH), lambda i, h, j: (0, 0)),
            pl.BlockSpec((1, FOX_DH), lambda i, h, j: (0, 0)),
        ],
        out_specs=pl.BlockSpec((1, tq, FOX_DH), lambda i, h, j: (i, j, h)),
        scratch_shapes=[
            pltpu.VMEM((s, 2 * FOX_DH), BF16),
            pltpu.VMEM((s, 2 * FOX_DH), BF16),
            pltpu.VMEM((tq, LANES), F32),
            pltpu.VMEM((tq, 2 * FOX_DH), F32),
            pltpu.VMEM((tq, tk), F32),
            pltpu.VMEM((tq, tk), F32),
        ],
        compiler_params=_cparams(("arbitrary", "arbitrary", "arbitrary")),
        name="fox",
    )(qkv, qkv, qkv, sm, sm, gq, gk)


def _memkv_kernel(mem_ref, g_ref, w_ref, gk_ref, o_ref, *, n_k_tiles):
    j = pl.program_id(1)
    m = mem_ref[0]
    hn = (m * lax.rsqrt(jnp.mean(m * m, axis=-1, keepdims=True) + EPS) * g_ref[...]).astype(BF16)
    r = jnp.dot(hn, w_ref[...].astype(BF16), preferred_element_type=F32)
    parts = []
    for t in range(r.shape[1] // MEM_DH):
        rt = r[:, t * MEM_DH:(t + 1) * MEM_DH]
        parts.append(rt * lax.rsqrt(jnp.mean(rt * rt, axis=-1, keepdims=True) + EPS) * gk_ref[...])
    normed = jnp.concatenate(parts, axis=1)
    is_k = jnp.where(j < n_k_tiles, 1.0, 0.0)
    o_ref[0] = (is_k * normed + (1.0 - is_k) * r).astype(o_ref.dtype)


def _memkv(mem, g, w, gk):
    b, ml, d = mem.shape
    n = w.shape[1]
    tn = 2 * MEM_DH
    return pl.pallas_call(
        functools.partial(_memkv_kernel, n_k_tiles=(n // 2) // tn),
        out_shape=jax.ShapeDtypeStruct((b, ml, n), BF16),
        grid=(b, n // tn),
        in_specs=[
            pl.BlockSpec((1, ml, d), lambda i, j: (i, 0, 0)),
            pl.BlockSpec((1, d), lambda i, j: (0, 0)),
            pl.BlockSpec((d, tn), lambda i, j: (0, j)),
            pl.BlockSpec((1, MEM_DH), lambda i, j: (0, 0)),
        ],
        out_specs=pl.BlockSpec((1, ml, tn), lambda i, j: (i, 0, j)),
        compiler_params=_cparams(("arbitrary", "arbitrary")),
        name="memkv",
    )(mem, g, w, gk)


def _mem_kernel(q_ref, k_ref, v_ref, gq_ref, o_ref):
    for hd in range(MEM_HEADS):
        cs = slice(hd * MEM_DH, (hd + 1) * MEM_DH)
        qq = q_ref[0, :, cs].astype(F32)
        qn = qq * lax.rsqrt(jnp.mean(qq * qq, axis=-1, keepdims=True) + EPS) * gq_ref[...] * (MEM_DH ** -0.5)
        s = _dot_nt(qn, k_ref[0, :, cs])
        p = jnp.exp(s - jnp.max(s, axis=1, keepdims=True))
        p = p / jnp.sum(p, axis=1, keepdims=True)
        o_ref[0, :, cs] = jnp.dot(p.astype(BF16), v_ref[0, :, cs], preferred_element_type=F32).astype(o_ref.dtype)


def _mem_attn(qsrc, q_col_block, kv, gq):
    b, s, _ = qsrc.shape
    ml = kv.shape[1]
    wq = MEM_HEADS * MEM_DH
    tq = _tile(s, ROW_TILE)
    return pl.pallas_call(
        _mem_kernel,
        out_shape=jax.ShapeDtypeStruct((b, s, wq), BF16),
        grid=(b, s // tq),
        in_specs=[
            pl.BlockSpec((1, tq, wq), lambda i, j: (i, j, q_col_block)),
            pl.BlockSpec((1, ml, wq), lambda i, j: (i, 0, 0)),
            pl.BlockSpec((1, ml, wq), lambda i, j: (i, 0, 1)),
            pl.BlockSpec((1, MEM_DH), lambda i, j: (0, 0)),
        ],
        out_specs=pl.BlockSpec((1, tq, wq), lambda i, j: (i, j, 0)),
        compiler_params=_cparams(("arbitrary", "arbitrary")),
        name="mem_attn",
    )(qsrc, kv, kv, gq)


def _merge_kernel(oa_hbm, ob_hbm, om_hbm, wa_ref, wb_ref, wm_ref, ga_ref, gb_ref, gm_ref, y_ref,
                  oa_buf, oa_sem, ob_buf, ob_sem, om_buf, om_sem, *, stream):
    def branch(o_hbm, o_buf, o_sem, w_ref, g_ref):
        slot = _stream_row_tile(o_hbm, o_buf, o_sem, **stream)
        return g_ref[...].astype(F32) * jnp.dot(o_buf[slot], w_ref[...].astype(BF16), preferred_element_type=F32)

    y = (branch(oa_hbm, oa_buf, oa_sem, wa_ref, ga_ref) + branch(ob_hbm, ob_buf, ob_sem, wb_ref, gb_ref)
         + branch(om_hbm, om_buf, om_sem, wm_ref, gm_ref))
    y_ref[...] = y.astype(y_ref.dtype)


def _merge(oa, ob, om, wa, wb, wm, gates):
    t = oa.shape[0]
    dm = wa.shape[1]
    tm = _tile(t, MERGE_TILE[0])
    tn = _tile(dm, MERGE_TILE[1])
    nb = dm // tn
    hbm = pl.BlockSpec(memory_space=pl.ANY)
    w_spec = lambda kk: pl.BlockSpec((kk, tn), lambda i, j: (0, j))
    g_spec = lambda off: pl.BlockSpec((tm, tn), lambda i, j: (i, off * nb + j))
    stream = dict(tm=tm, n_row_tiles=t // tm, n_slabs=_n_slabs(tm, nb))
    return pl.pallas_call(
        functools.partial(_merge_kernel, stream=stream),
        out_shape=jax.ShapeDtypeStruct((t, dm), BF16),
        grid=(t // tm, nb),
        in_specs=[hbm, hbm, hbm,
                  w_spec(oa.shape[1]), w_spec(ob.shape[1]), w_spec(om.shape[1]),
                  g_spec(0), g_spec(1), g_spec(2)],
        out_specs=pl.BlockSpec((tm, tn), lambda i, j: (i, j)),
        scratch_shapes=(_row_tile_scratch(tm, oa.shape[1], oa.dtype) + _row_tile_scratch(tm, ob.shape[1], ob.dtype)
                        + _row_tile_scratch(tm, om.shape[1], om.dtype)),
        compiler_params=_cparams(("arbitrary", "arbitrary")),
        name="merge",
    )(oa, ob, om, wa, wb, wm, gates, gates, gates)


def _layer(x, mem, g_mix, w_in, conv_w, a_log, dt_bias, gdn_norm_g, fox_b_f, fox_q_norm, fox_k_norm,
           g_mem, w_mem_kv, mem_q_norm, mem_k_norm, w_up_gdn, w_up_fox, w_up_mem, w_out, g_mlp, w_ff1, w_ff2):
    b, s, d = x.shape
    t = b * s
    gdn_qk = GDN_HEADS * GDN_DK
    gdn_v = GDN_HEADS * GDN_DV
    fox_w = FOX_HEADS * FOX_DH
    mem_w = MEM_HEADS * MEM_DH
    o_z = 2 * gdn_qk + gdn_v
    o_beta = o_z + gdn_v
    o_dec = o_beta + GDN_HEADS
    o_fq = o_dec + GDN_HEADS
    o_ff = o_fq + 3 * fox_w
    o_mq = o_ff + FOX_HEADS
    o_gate = o_mq + mem_w

    wt = w_in.T
    n_small = 2 * GDN_HEADS + FOX_HEADS
    w_small = jnp.concatenate([wt[o_beta:o_fq], wt[o_ff:o_mq],
                               jnp.zeros((LANES - n_small, d), F32)], axis=0)
    zpad = jnp.zeros((LANES - n_small,), F32)
    bias = jnp.concatenate([jnp.zeros((GDN_HEADS,), F32), dt_bias.astype(F32), fox_b_f.astype(F32), zpad])[None, :]
    alog = jnp.concatenate([jnp.zeros((GDN_HEADS,), F32), a_log.astype(F32), jnp.zeros((FOX_HEADS,), F32), zpad])[None, :]

    h, sm = _norm_small(x, g_mix[None, :], w_small, bias, alog)
    h2d = h.reshape(t, d)

    proj = functools.partial(_matmul, h2d, wt, tile=PROJ_TILE, w_rows_are_outputs=True)
    qkvz = proj(out_dtype=F32, w_row_ranges=[(0, o_beta)], name="proj_gdn").reshape(b, s, -1)
    att = proj(out_dtype=BF16, w_row_ranges=[(o_fq, 3 * fox_w), (o_mq, mem_w)],
               name="proj_att").reshape(b, s, -1)
    gates = proj(out_dtype=BF16, act="sigmoid", w_row_ranges=[(o_gate, N_BRANCH * d)], name="proj_gate")

    o_a = _gdn(qkvz, sm, conv_w, gdn_norm_g[None, :])
    o_b = _fox(att, sm, fox_q_norm[None, :], fox_k_norm[None, :])
    kv_m = _memkv(mem, g_mem[None, :], w_mem_kv, mem_k_norm[None, :])
    o_m = _mem_attn(att, (3 * fox_w) // mem_w, kv_m, mem_q_norm[None, :])

    y = _merge(o_a.reshape(t, -1), o_b.reshape(t, -1), o_m.reshape(t, -1), w_up_gdn, w_up_fox, w_up_mem, gates)
    x1, h2 = _out_norm(y, w_out.astype(BF16), x.reshape(t, d), g_mlp[None, :])
    u = _matmul(h2, w_ff1, out_dtype=BF16, tile=FF1_TILE, act="relu2", name="ff1")
    out = _matmul(u, w_ff2.astype(BF16), out_dtype=F32, tile=FF2_TILE, residual=x1, name="ff2")
    return out.reshape(b, s, d)


def kernel(x, mem, g_mix, w_in, conv_w, a_log, dt_bias, gdn_norm_g, fox_b_f, fox_q_norm, fox_k_norm, g_mem, w_mem_kv, mem_q_norm, mem_k_norm, w_up_gdn, w_up_fox, w_up_mem, w_out, g_mlp, w_ff1, w_ff2):
    depth = w_in.shape[0]
    for l in range(depth):
        x = _layer(x, mem, g_mix[l], w_in[l], conv_w[l], a_log[l], dt_bias[l], gdn_norm_g[l], fox_b_f[l],
                   fox_q_norm[l], fox_k_norm[l], g_mem[l], w_mem_kv[l], mem_q_norm[l], mem_k_norm[l],
                   w_up_gdn[l], w_up_fox[l], w_up_mem[l], w_out[l], g_mlp[l], w_ff1[l], w_ff2[l])
    return x
```

```python
import functools
import math

import jax
import jax.numpy as jnp
from jax import lax
from jax.experimental import pallas as pl
from jax.experimental.pallas import tpu as pltpu

F32 = jnp.float32
BF16 = jnp.bfloat16
EPS = 1e-6

GDN_HEADS = 8
GDN_DK = 128
GDN_DV = 128
GDN_CONV = 4
GDN_CHUNK = 128
GDN_HEADS_PER_STEP = 8
GDN_ROWS_PER_STEP = 256
FOX_HEADS = 8
FOX_DH = 128
FOX_TQ = 1024
FOX_TK = 512
MEM_HEADS = 4
MEM_DH = 256
N_BRANCH = 3
LANES = 128
SUBLANES = 8
NEG_BIG = -1e30
LOG2E = math.log2(math.e)

V7X_VMEM_BYTES = 64 * 1024 * 1024
VMEM_LIMIT = V7X_VMEM_BYTES - 8 * 1024 * 1024

PROJ_TILE = (2048, 512)
FF1_TILE = (2048, 512)
FF2_TILE = (1024, 256)
MERGE_TILE = (2048, 256)
ROW_TILE = 512
LHS_SLABS = 8


def _tile(n, pref):
    return pref if n % pref == 0 else n


def _cparams(sem):
    return pltpu.CompilerParams(dimension_semantics=sem, vmem_limit_bytes=VMEM_LIMIT)


def _dot(a, b):
    return jnp.dot(a.astype(BF16), b.astype(BF16), preferred_element_type=F32)


def _dot_nt(a, b):
    return lax.dot_general(a.astype(BF16), b.astype(BF16), (((1,), (1,)), ((), ())),
                           preferred_element_type=F32)


def _split2(a):
    hi = a.astype(BF16)
    lo = (a - hi.astype(F32)).astype(BF16)
    return hi, lo


def _split3(a):
    hi = a.astype(BF16).astype(F32)
    r = a - hi
    mid = r.astype(BF16).astype(F32)
    lo = (r - mid).astype(BF16).astype(F32)
    return hi, mid, lo


def _dot_exact_lhs(l_bf16, v):
    hi, mid, lo = _split3(v)
    n = v.shape[1]
    d = functools.partial(jnp.dot, preferred_element_type=F32)
    both = d(l_bf16, jnp.concatenate([hi, mid], axis=1).astype(BF16))
    return both[:, :n] + both[:, n:] + d(l_bf16, lo.astype(BF16))


def _lane_col(a, idx):
    lane = lax.broadcasted_iota(jnp.int32, a.shape, 1)
    return jnp.sum(jnp.where(lane == idx, a, 0.0), axis=1, keepdims=True)


def _softplus(x):
    return jnp.maximum(x, 0.0) + jnp.log1p(jnp.exp(-jnp.abs(x)))


def _norm_small_kernel(x_ref, g_ref, ws_ref, bias_ref, alog_ref, h_ref, sm_ref,
                       carry_ref, wsplit_ref, lfull_ref, lblk_ref, *, tm):
    s = pl.program_id(1)

    @pl.when(s == 0)
    def _():
        carry_ref[...] = jnp.zeros_like(carry_ref)

    @pl.when((pl.program_id(0) == 0) & (s == 0))
    def _():
        w_hi, w_lo = _split2(ws_ref[...])
        wsplit_ref[0:LANES, :] = w_hi
        wsplit_ref[LANES:2 * LANES, :] = w_lo
        row = lax.broadcasted_iota(jnp.int32, (tm, tm), 0)
        col = lax.broadcasted_iota(jnp.int32, (tm, tm), 1)
        low = col <= row
        sh = GDN_CHUNK.bit_length() - 1
        lfull_ref[...] = jnp.where(low, 1.0, 0.0).astype(BF16)
        lblk_ref[...] = jnp.where(low, jnp.where((row >> sh) == (col >> sh), 1.0, 0.0), 0.0).astype(BF16)

    x = x_ref[0]
    h = x * lax.rsqrt(jnp.mean(x * x, axis=-1, keepdims=True) + EPS) * g_ref[...]
    h_hi, h_lo = _split2(h)
    h_ref[0] = h_hi

    nt = lambda p, q: lax.dot_general(p, q, (((1,), (1,)), ((), ())), preferred_element_type=F32)
    both = nt(h_hi, wsplit_ref[...])
    pre = both[:, :LANES] + both[:, LANES:] + nt(h_lo, wsplit_ref[0:LANES, :]) + bias_ref[...]
    lane = lax.broadcasted_iota(jnp.int32, pre.shape, 1)
    nh = GDN_HEADS
    beta = jax.nn.sigmoid(pre)
    gdec = -jnp.exp(alog_ref[...]) * _softplus(pre)
    logf = -_softplus(-pre)
    vals = jnp.where(lane < nh, beta, jnp.where(lane < 2 * nh, gdec, jnp.where(lane < 3 * nh, logf, 0.0)))

    cs_blk = _dot_exact_lhs(lblk_ref[...], vals)
    cs_full = _dot_exact_lhs(lfull_ref[...], vals) + carry_ref[0:1, :]
    hi, mid, lo = _split3(cs_full * LOG2E)
    pieces = jnp.where(lane < 3 * nh, hi, jnp.where(lane < 4 * nh, pltpu.roll(mid, nh, 1), pltpu.roll(lo, 2 * nh, 1)))
    sm_ref[0] = jnp.where(lane < nh, vals, jnp.where(lane < 2 * nh, cs_blk, jnp.where(lane < 5 * nh, pieces, 0.0)))
    carry_ref[...] = jnp.broadcast_to(cs_full[tm - 1:tm, :], carry_ref.shape)


def _norm_small(x, g, w_small, bias, alog):
    b, s, d = x.shape
    tm = _tile(s, ROW_TILE)
    return pl.pallas_call(
        functools.partial(_norm_small_kernel, tm=tm),
        out_shape=(jax.ShapeDtypeStruct((b, s, d), BF16), jax.ShapeDtypeStruct((b, s, LANES), F32)),
        grid=(b, s // tm),
        in_specs=[
            pl.BlockSpec((1, tm, d), lambda i, j: (i, j, 0)),
            pl.BlockSpec((1, d), lambda i, j: (0, 0)),
            pl.BlockSpec((LANES, d), lambda i, j: (0, 0)),
            pl.BlockSpec((1, LANES), lambda i, j: (0, 0)),
            pl.BlockSpec((1, LANES), lambda i, j: (0, 0)),
        ],
        out_specs=(
            pl.BlockSpec((1, tm, d), lambda i, j: (i, j, 0)),
            pl.BlockSpec((1, tm, LANES), lambda i, j: (i, j, 0)),
        ),
        scratch_shapes=[
            pltpu.VMEM((SUBLANES, LANES), F32),
            pltpu.VMEM((2 * LANES, d), BF16),
            pltpu.VMEM((tm, tm), BF16),
            pltpu.VMEM((tm, tm), BF16),
        ],
        compiler_params=_cparams(("arbitrary", "arbitrary")),
        name="norm_small",
    )(x, g, w_small, bias, alog)


def _out_norm_kernel(y_ref, w_ref, x_ref, g_ref, x1_ref, h_ref):
    x1 = x_ref[...] + jnp.dot(y_ref[...], w_ref[...], preferred_element_type=F32)
    x1_ref[...] = x1
    h_ref[...] = (x1 * lax.rsqrt(jnp.mean(x1 * x1, axis=-1, keepdims=True) + EPS) * g_ref[...]).astype(h_ref.dtype)


def _out_norm(y, w_bf16, x2d, g):
    t, d = x2d.shape
    k = y.shape[1]
    tm = _tile(t, ROW_TILE)
    row = lambda width: pl.BlockSpec((tm, width), lambda i: (i, 0))
    return pl.pallas_call(
        _out_norm_kernel,
        out_shape=(jax.ShapeDtypeStruct((t, d), F32), jax.ShapeDtypeStruct((t, d), BF16)),
        grid=(t // tm,),
        in_specs=[row(k), pl.BlockSpec((k, d), lambda i: (0, 0)), row(d), pl.BlockSpec((1, d), lambda i: (0, 0))],
        out_specs=(row(d), row(d)),
        compiler_params=_cparams(("arbitrary",)),
        name="out_norm",
    )(y, w_bf16, x2d, g)


def _stream_row_tile(a_hbm, a_buf, sem, *, tm, n_row_tiles, n_slabs):
    i = pl.program_id(0)
    j = pl.program_id(1)
    slot = i % 2
    slab = tm // n_slabs

    def slab_copy(row_tile, part, to_slot):
        rows = pl.ds(row_tile * tm + part * slab, slab)
        return pltpu.make_async_copy(a_hbm.at[rows, :], a_buf.at[to_slot, pl.ds(part * slab, slab), :],
                                     sem.at[to_slot])

    @pl.when((i == 0) & (j == 0))
    def _():
        for part in range(n_slabs):
            slab_copy(0, part, 0).start()

    @pl.when(j == 0)
    def _():
        for part in range(n_slabs):
            slab_copy(i, part, slot).wait()

    @pl.when((i + 1 < n_row_tiles) & (j < n_slabs))
    def _():
        slab_copy(i + 1, j, 1 - slot).start()

    return slot


def _n_slabs(tm, n_col_steps):
    n = min(LHS_SLABS, n_col_steps)
    assert tm % n == 0 and (tm // n) % (2 * SUBLANES) == 0, "slabs must be whole packed-bf16 sublane tiles"
    return n


def _row_tile_scratch(tm, k, dtype):
    return [pltpu.VMEM((2, tm, k), dtype), pltpu.SemaphoreType.DMA((2,))]


def _mm_kernel(a_hbm, w_ref, *rest, act, has_res, w_rows_are_outputs, stream):
    if has_res:
        r_ref, o_ref, a_buf, sem = rest
    else:
        o_ref, a_buf, sem = rest
    slot = _stream_row_tile(a_hbm, a_buf, sem, **stream)
    contract = (((1,), (1,)), ((), ())) if w_rows_are_outputs else (((1,), (0,)), ((), ()))
    acc = lax.dot_general(a_buf[slot], w_ref[...].astype(BF16), contract, preferred_element_type=F32)
    if act == "sigmoid":
        acc = 0.5 * jnp.tanh(0.5 * acc) + 0.5
    elif act == "relu2":
        r = jnp.maximum(acc, 0.0)
        acc = r * r
    if has_res:
        acc = acc + r_ref[...]
    o_ref[...] = acc.astype(o_ref.dtype)


def _matmul(a, w, *, out_dtype, tile, act=None, residual=None, w_rows_are_outputs=False,
            w_row_ranges=None, name="matmul"):
    m, k = a.shape
    if w_row_ranges is not None:
        assert w_rows_are_outputs
        n = sum(r for _, r in w_row_ranges)
    else:
        n = w.shape[0] if w_rows_are_outputs else w.shape[1]
    tm = _tile(m, tile[0])
    tn = _tile(n, tile[1])
    if w_row_ranges is not None:
        assert all(r % tn == 0 and f % 8 == 0 for f, r in w_row_ranges)

        def w_rows(i, j):
            start, first_blk = jnp.int32(0), 0
            for f, r in w_row_ranges:
                start = jnp.where(j >= first_blk, f + (j - first_blk) * tn, start)
                first_blk += r // tn
            return pl.multiple_of(start, 8), 0

        w_spec = pl.BlockSpec((pl.Element(tn), pl.Element(k)), w_rows)
    elif w_rows_are_outputs:
        w_spec = pl.BlockSpec((tn, k), lambda i, j: (j, 0))
    else:
        w_spec = pl.BlockSpec((k, tn), lambda i, j: (0, j))
    in_specs = [pl.BlockSpec(memory_space=pl.ANY), w_spec]
    args = [a, w]
    if residual is not None:
        in_specs.append(pl.BlockSpec((tm, tn), lambda i, j: (i, j)))
        args.append(residual)
    stream = dict(tm=tm, n_row_tiles=m // tm, n_slabs=_n_slabs(tm, n // tn))
    return pl.pallas_call(
        functools.partial(_mm_kernel, act=act, has_res=residual is not None,
                          w_rows_are_outputs=w_rows_are_outputs, stream=stream),
        out_shape=jax.ShapeDtypeStruct((m, n), out_dtype),
        grid=(m // tm, n // tn),
        in_specs=in_specs,
        out_specs=pl.BlockSpec((tm, tn), lambda i, j: (i, j)),
        scratch_shapes=_row_tile_scratch(tm, k, a.dtype),
        compiler_params=_cparams(("arbitrary", "arbitrary")),
        name=name,
    )(*args)


def _gdn_kernel(q_ref, k_ref, v_ref, z_ref, sm_ref, cwq_ref, cwk_ref, cwv_ref, gn_ref, o_ref,
                state_ref, tail_ref, ext_ref, qs, ks, vs, *, ts, nc, hb, nb):
    hg = pl.program_id(0)
    si = pl.program_id(1)
    c_sz = GDN_CHUNK

    @pl.when(si == 0)
    def _():
        state_ref[...] = jnp.zeros_like(state_ref)
        tail_ref[...] = jnp.zeros_like(tail_ref)

    def conv_silu(x_ref, w_ref, bi, idx, c):
        r0 = c * c_sz
        x = x_ref[bi, r0:r0 + c_sz, :]
        w = w_ref[...]
        y = x * w[0:1, :]
        for i in range(1, GDN_CONV):
            y = pltpu.roll(y, 1, 0) + x * w[i:i + 1, :]
        sl = SUBLANES
        ext_ref[idx, 0:sl, :] = tail_ref[bi, idx] if c == 0 else x_ref[bi, r0 - sl:r0, :]
        ext_ref[idx, sl:2 * sl, :] = x[0:sl, :]
        first = sl - (GDN_CONV - 1)
        head = ext_ref[idx, first:first + sl, :] * w[0:1, :]
        for i in range(1, GDN_CONV):
            head = head + ext_ref[idx, first + i:first + i + sl, :] * w[i:i + 1, :]
        y = jnp.concatenate([head, y[sl:, :]], axis=0)
        if c == nc - 1:
            tail_ref[bi, idx] = x[c_sz - sl:c_sz, :]
        return y * jax.nn.sigmoid(y)

    def prepare(c):
        rows = slice(c * c_sz, (c + 1) * c_sz)
        for bi in range(nb):
            q = conv_silu(q_ref, cwq_ref, bi, 0, c)
            k = conv_silu(k_ref, cwk_ref, bi, 1, c)
            vs[bi, rows, :] = conv_silu(v_ref, cwv_ref, bi, 2, c)
            for hh in range(hb):
                cs = slice(hh * LANES, (hh + 1) * LANES)
                qh = q[:, cs]
                kh = k[:, cs]
                qs[bi, rows, cs] = qh * (lax.rsqrt(jnp.sum(qh * qh, axis=-1, keepdims=True) + EPS)
                                         * (GDN_DK ** -0.5))
                ks[bi, rows, cs] = kh * lax.rsqrt(jnp.sum(kh * kh, axis=-1, keepdims=True) + EPS)

    ri = lax.broadcasted_iota(jnp.int32, (c_sz, c_sz), 0)
    ci = lax.broadcasted_iota(jnp.int32, (c_sz, c_sz), 1)
    strict = ri > ci
    incl = ri >= ci
    eye = jnp.where(ri == ci, 1.0, 0.0)
    n_lvl = c_sz.bit_length() - 1
    lvl_masks = []
    for l in range(n_lvl):
        same = (ri >> (l + 1)) == (ci >> (l + 1))
        lvl_masks.append(jnp.where(
            same, jnp.where(((ri >> l) & 1) == 1, jnp.where(((ci >> l) & 1) == 0, 1.0, 0.0), 0.0), 0.0))

    def advance(c):
        rows = slice(c * c_sz, (c + 1) * c_sz)
        chains = [(bi, hh) for bi in range(nb) for hh in range(hb)]
        mlows, rhss, lhs2s, qds, cds = [], [], [], [], []
        for bi, hh in chains:
            cs = slice(hh * LANES, (hh + 1) * LANES)
            hd = hg * hb + hh
            smc = sm_ref[bi, rows, :]
            beta = _lane_col(smc, hd)
            gam = _lane_col(smc, hd + GDN_HEADS)
            kc = ks[bi, rows, cs]
            qc = qs[bi, rows, cs]
            egam = jnp.exp(gam)
            g_last = gam[c_sz - 1:c_sz, :]
            gcol = jnp.broadcast_to(gam, (c_sz, c_sz))
            diff = gcol - gcol.T
            e = jnp.exp(jnp.where(incl, diff, 0.0))
            mlows.append(beta * _dot_nt(kc, kc) * jnp.where(strict, e, 0.0))
            qk = _dot_nt(qc, kc) * jnp.where(incl, e, 0.0)
            kd_t = (kc * jnp.exp(g_last - gam)).T
            lhs2s.append(jnp.concatenate([qk, kd_t], axis=0).astype(BF16))
            rhss.append(jnp.concatenate([kc * (beta * egam), vs[bi, rows, cs] * beta], axis=1).astype(BF16))
            qds.append(qc * egam)
            cds.append(jnp.exp(g_last))
        xs = [eye - lvl_masks[0] * m for m in mlows]
        mlows_bf = [m.astype(BF16) for m in mlows]
        for l in range(1, n_lvl):
            mask_bf = lvl_masks[l].astype(BF16)
            tl = [_dot(mask_bf * m, x) for m, x in zip(mlows_bf, xs)]
            xs = [x - _dot(x, t) for x, t in zip(xs, tl)]
        wus = [_dot(x, r) for x, r in zip(xs, rhss)]
        for n, (bi, hh) in enumerate(chains):
            cs = slice(hh * LANES, (hh + 1) * LANES)
            state = state_ref[n]
            wu = wus[n]
            a = _dot(jnp.concatenate([wu[:, :GDN_DK], qds[n]], axis=0), state)
            u = wu[:, GDN_DK:] - a[:c_sz]
            b2 = jnp.dot(lhs2s[n], u.astype(BF16), preferred_element_type=F32)
            o = a[c_sz:] + b2[:c_sz]
            state_ref[n] = state * cds[n] + b2[c_sz:]
            z = z_ref[bi, rows, cs]
            on = o * lax.rsqrt(jnp.mean(o * o, axis=-1, keepdims=True) + EPS) * gn_ref[...]
            o_ref[bi, rows, cs] = (on * (z * jax.nn.sigmoid(z))).astype(o_ref.dtype)

    for c in range(nc):
        prepare(c)
        advance(c)


def _gdn(qkvz, sm, conv_w, gn):
    b, s, _ = qkvz.shape
    nh = GDN_HEADS
    hb = GDN_HEADS_PER_STEP
    ng = nh // hb
    wb = hb * LANES
    ts = _tile(s, GDN_ROWS_PER_STEP)
    nc = ts // GDN_CHUNK
    blk = lambda off: pl.BlockSpec((b, ts, wb), lambda h, j: (0, j, off + h))
    cw = lambda off: pl.BlockSpec((GDN_CONV, wb), lambda h, j: (0, off + h))
    return pl.pallas_call(
        functools.partial(_gdn_kernel, ts=ts, nc=nc, hb=hb, nb=b),
        out_shape=jax.ShapeDtypeStruct((b, s, nh * GDN_DV), BF16),
        grid=(ng, s // ts),
        in_specs=[
            blk(0), blk(ng), blk(2 * ng), blk(3 * ng),
            pl.BlockSpec((b, ts, LANES), lambda h, j: (0, j, 0)),
            cw(0), cw(ng), cw(2 * ng),
            pl.BlockSpec((1, GDN_DV), lambda h, j: (0, 0)),
        ],
        out_specs=pl.BlockSpec((b, ts, wb), lambda h, j: (0, j, h)),
        scratch_shapes=[
            pltpu.VMEM((b * hb, GDN_DK, GDN_DV), F32),
            pltpu.VMEM((b, 3, SUBLANES, wb), F32),
            pltpu.VMEM((3, 2 * SUBLANES, wb), F32),
            pltpu.VMEM((b, ts, wb), F32),
            pltpu.VMEM((b, ts, wb), F32),
            pltpu.VMEM((b, ts, wb), F32),
        ],
        compiler_params=_cparams(("arbitrary", "arbitrary")),
        name="gdn",
    )(qkvz, qkvz, qkvz, qkvz, sm, conv_w, conv_w, conv_w, gn)


def _fox_kernel(q_ref, k_ref, v_ref, smq_ref, smk_ref, gq_ref, gk_ref, o_ref,
                kaug_ref, vaug_ref, m_s, acc_s, sa_ref, sb_ref, *, tq, tk, nk):
    hd = pl.program_id(1)
    qi = pl.program_id(2)
    n_sub = tq // tk
    n_lt = tk // LANES

    src = lax.broadcasted_iota(jnp.int32, (LANES, LANES), 0)
    dst = lax.broadcasted_iota(jnp.int32, (LANES, LANES), 1)
    first_piece = 2 * GDN_HEADS + hd

    def bias_cols(sm_block, dst0, sign, ones0):
        pick = jnp.where((dst >= dst0) & (dst < dst0 + 3) & (src == first_piece + FOX_HEADS * (dst - dst0)),
                         sign, 0.0).astype(BF16)
        lane = lax.broadcasted_iota(jnp.int32, sm_block.shape, 1)
        ones = jnp.where((lane >= ones0) & (lane < ones0 + 3), 1.0, 0.0)
        return (jnp.dot(sm_block.astype(BF16), pick, preferred_element_type=F32) + ones).astype(BF16)

    @pl.when(qi == 0)
    def _():
        def build(j, carry):
            rows = pl.ds(pl.multiple_of(j * tk, tk), tk)
            kk = k_ref[0, rows, :].astype(F32)
            kn = kk * lax.rsqrt(jnp.mean(kk * kk, axis=-1, keepdims=True) + EPS) * gk_ref[...]
            kaug_ref[rows, 0:FOX_DH] = kn.astype(BF16)
            kaug_ref[rows, FOX_DH:2 * FOX_DH] = bias_cols(smk_ref[0, rows, :], 3, -1.0, 0)
            vaug_ref[rows, 0:FOX_DH] = v_ref[0, rows, :]
            vaug_ref[rows, FOX_DH:2 * FOX_DH] = jnp.ones((tk, FOX_DH), BF16)
            return carry
        lax.fori_loop(0, nk, build, 0)

    qq = q_ref[0].astype(F32)
    qn = qq * (lax.rsqrt(jnp.mean(qq * qq, axis=-1, keepdims=True) + EPS) * (FOX_DH ** -0.5 * LOG2E)) * gq_ref[...]
    q_aug = jnp.concatenate([qn.astype(BF16), bias_cols(smq_ref[0], 0, 1.0, 3)], axis=1)

    m_s[...] = jnp.full_like(m_s, NEG_BIG)
    acc_s[...] = jnp.zeros_like(acc_s)

    def scores(s_ref, j):
        rows = pl.ds(pl.multiple_of(j * tk, tk), tk)
        s_ref[...] = lax.dot_general(q_aug, kaug_ref[rows, :], (((1,), (1,)), ((), ())),
                                     preferred_element_type=F32)

    def accumulate(s_ref, j, diag):
        rows = pl.ds(pl.multiple_of(j * tk, tk), tk)
        v_blk = vaug_ref[rows, :]
        groups = ([(0, tq, False)] if diag is None
                  else [(r * tk, (r + 1) * tk, r == diag) for r in range(diag, n_sub)])
        for lo_row, hi_row, masked in groups:
            rs = slice(lo_row, hi_row)
            tiles = [s_ref[rs, c * LANES:(c + 1) * LANES] for c in range(n_lt)]
            if masked:
                ri = lax.broadcasted_iota(jnp.int32, (tk, LANES), 0)
                ci = lax.broadcasted_iota(jnp.int32, (tk, LANES), 1)
                tiles = [jnp.where(ri >= ci + c * LANES, t, NEG_BIG) for c, t in enumerate(tiles)]
            mx = tiles[0]
            for t in tiles[1:]:
                mx = jnp.maximum(mx, t)
            m_prev = m_s[rs, :]
            m_next = jnp.maximum(m_prev, jnp.max(mx, axis=1, keepdims=True))
            p = jnp.concatenate([jnp.exp2(t - m_next) for t in tiles], axis=1).astype(BF16)
            alpha = jnp.exp2(m_prev - m_next)
            acc_s[rs, :] = (jnp.concatenate([alpha, alpha], axis=1) * acc_s[rs, :]
                            + jnp.dot(p, v_blk, preferred_element_type=F32))
            m_s[rs, :] = m_next

    scores(sa_ref, 0)

    def body(i, carry):
        j = 2 * i
        scores(sb_ref, j + 1)
        accumulate(sa_ref, j, None)
        scores(sa_ref, j + 2)
        accumulate(sb_ref, j + 1, None)
        return carry

    n_below = n_sub * qi
    lax.fori_loop(0, n_below // 2, body, 0)
    bufs = (sa_ref, sb_ref)
    for e in range(n_sub):
        if e + 1 < n_sub:
            scores(bufs[(e + 1) % 2], n_below + e + 1)
        accumulate(bufs[e % 2], n_below + e, e)

    acc = acc_s[...]
    o_ref[0] = (acc[:, :FOX_DH] / acc[:, FOX_DH:]).astype(o_ref.dtype)


def _fox(qkv, sm, gq, gk):
    b, s, _ = qkv.shape
    nh = FOX_HEADS
    tq = _tile(s, FOX_TQ)
    tk = _tile(tq, FOX_TK)
    assert (tq // tk) % 2 == 0, "the score pipeline consumes key blocks in pairs"
    nk = s // tk
    return pl.pallas_call(
        functools.partial(_fox_kernel, tq=tq, tk=tk, nk=nk),
        out_shape=jax.ShapeDtypeStruct((b, s, nh * FOX_DH), BF16),
        grid=(b, nh, s // tq),
        in_specs=[
            pl.BlockSpec((1, tq, FOX_DH), lambda i, h, j: (i, j, h)),
            pl.BlockSpec((1, s, FOX_DH), lambda i, h, j: (i, 0, nh + h)),
            pl.BlockSpec((1, s, FOX_DH), lambda i, h, j: (i, 0, 2 * nh + h)),
            pl.BlockSpec((1, tq, LANES), lambda i, h, j: (i, j, 0)),
            pl.BlockSpec((1, s, LANES), lambda i, h, j: (i, 0, 0)),
            pl.BlockSpec((1, FOX_DH), lambda i, h, j: (0, 0)),
            pl.BlockSpec((1, FOX_DH), lambda i, h, j: (0, 0)),
        ],
        out_specs=pl.BlockSpec((1, tq, FOX_DH), lambda i, h, j: (i, j, h)),
        scratch_shapes=[
            pltpu.VMEM((s, 2 * FOX_DH), BF16),
            pltpu.VMEM((s, 2 * FOX_DH), BF16),
            pltpu.VMEM((tq, LANES), F32),
            pltpu.VMEM((tq, 2 * FOX_DH), F32),
            pltpu.VMEM((tq, tk), F32),
            pltpu.VMEM((tq, tk), F32),
        ],
        compiler_params=_cparams(("arbitrary", "arbitrary", "arbitrary")),
        name="fox",
    )(qkv, qkv, qkv, sm, sm, gq, gk)


def _memkv_kernel(mem_ref, g_ref, w_ref, gk_ref, o_ref, *, n_k_tiles):
    j = pl.program_id(1)
    m = mem_ref[0]
    hn = (m * lax.rsqrt(jnp.mean(m * m, axis=-1, keepdims=True) + EPS) * g_ref[...]).astype(BF16)
    r = jnp.dot(hn, w_ref[...].astype(BF16), preferred_element_type=F32)
    parts = []
    for t in range(r.shape[1] // MEM_DH):
        rt = r[:, t * MEM_DH:(t + 1) * MEM_DH]
        parts.append(rt * lax.rsqrt(jnp.mean(rt * rt, axis=-1, keepdims=True) + EPS) * gk_ref[...])
    normed = jnp.concatenate(parts, axis=1)
    is_k = jnp.where(j < n_k_tiles, 1.0, 0.0)
    o_ref[0] = (is_k * normed + (1.0 - is_k) * r).astype(o_ref.dtype)


def _memkv(mem, g, w, gk):
    b, ml, d = mem.shape
    n = w.shape[1]
    tn = 2 * MEM_DH
    return pl.pallas_call(
        functools.partial(_memkv_kernel, n_k_tiles=(n // 2) // tn),
        out_shape=jax.ShapeDtypeStruct((b, ml, n), BF16),
        grid=(b, n // tn),
        in_specs=[
            pl.BlockSpec((1, ml, d), lambda i, j: (i, 0, 0)),
            pl.BlockSpec((1, d), lambda i, j: (0, 0)),
            pl.BlockSpec((d, tn), lambda i, j: (0, j)),
            pl.BlockSpec((1, MEM_DH), lambda i, j: (0, 0)),
        ],
        out_specs=pl.BlockSpec((1, ml, tn), lambda i, j: (i, 0, j)),
        compiler_params=_cparams(("arbitrary", "arbitrary")),
        name="memkv",
    )(mem, g, w, gk)


def _mem_kernel(q_ref, k_ref, v_ref, gq_ref, o_ref):
    for hd in range(MEM_HEADS):
        cs = slice(hd * MEM_DH, (hd + 1) * MEM_DH)
        qq = q_ref[0, :, cs].astype(F32)
        qn = qq * lax.rsqrt(jnp.mean(qq * qq, axis=-1, keepdims=True) + EPS) * gq_ref[...] * (MEM_DH ** -0.5)
        s = _dot_nt(qn, k_ref[0, :, cs])
        p = jnp.exp(s - jnp.max(s, axis=1, keepdims=True))
        p = p / jnp.sum(p, axis=1, keepdims=True)
        o_ref[0, :, cs] = jnp.dot(p.astype(BF16), v_ref[0, :, cs], preferred_element_type=F32).astype(o_ref.dtype)


def _mem_attn(qsrc, q_col_block, kv, gq):
    b, s, _ = qsrc.shape
    ml = kv.shape[1]
    wq = MEM_HEADS * MEM_DH
    tq = _tile(s, ROW_TILE)
    return pl.pallas_call(
        _mem_kernel,
        out_shape=jax.ShapeDtypeStruct((b, s, wq), BF16),
        grid=(b, s // tq),
        in_specs=[
            pl.BlockSpec((1, tq, wq), lambda i, j: (i, j, q_col_block)),
            pl.BlockSpec((1, ml, wq), lambda i, j: (i, 0, 0)),
            pl.BlockSpec((1, ml, wq), lambda i, j: (i, 0, 1)),
            pl.BlockSpec((1, MEM_DH), lambda i, j: (0, 0)),
        ],
        out_specs=pl.BlockSpec((1, tq, wq), lambda i, j: (i, j, 0)),
        compiler_params=_cparams(("arbitrary", "arbitrary")),
        name="mem_attn",
    )(qsrc, kv, kv, gq)


def _merge_kernel(oa_ref, ob_ref, om_ref, wa_ref, wb_ref, wm_ref, ga_ref, gb_ref, gm_ref, y_ref):
    d = lambda o_ref, w_ref: jnp.dot(o_ref[...], w_ref[...].astype(BF16), preferred_element_type=F32)
    y = (ga_ref[...].astype(F32) * d(oa_ref, wa_ref)
         + gb_ref[...].astype(F32) * d(ob_ref, wb_ref)
         + gm_ref[...].astype(F32) * d(om_ref, wm_ref))
    y_ref[...] = y.astype(y_ref.dtype)


def _merge(oa, ob, om, wa, wb, wm, gates):
    t, ka = oa.shape
    dm = wa.shape[1]
    tm = _tile(t, MERGE_TILE[0])
    tn = _tile(dm, MERGE_TILE[1])
    nb = dm // tn
    a_spec = lambda kk: pl.BlockSpec((tm, kk), lambda i, j: (i, 0))
    w_spec = lambda kk: pl.BlockSpec((kk, tn), lambda i, j: (0, j))
    g_spec = lambda off: pl.BlockSpec((tm, tn), lambda i, j: (i, off * nb + j))
    return pl.pallas_call(
        _merge_kernel,
        out_shape=jax.ShapeDtypeStruct((t, dm), BF16),
        grid=(t // tm, nb),
        in_specs=[a_spec(ka), a_spec(ob.shape[1]), a_spec(om.shape[1]),
                  w_spec(ka), w_spec(ob.shape[1]), w_spec(om.shape[1]),
                  g_spec(0), g_spec(1), g_spec(2)],
        out_specs=pl.BlockSpec((tm, tn), lambda i, j: (i, j)),
        compiler_params=_cparams(("arbitrary", "arbitrary")),
        name="merge",
    )(oa, ob, om, wa, wb, wm, gates, gates, gates)


def _layer(x, mem, g_mix, w_in, conv_w, a_log, dt_bias, gdn_norm_g, fox_b_f, fox_q_norm, fox_k_norm,
           g_mem, w_mem_kv, mem_q_norm, mem_k_norm, w_up_gdn, w_up_fox, w_up_mem, w_out, g_mlp, w_ff1, w_ff2):
    b, s, d = x.shape
    t = b * s
    gdn_qk = GDN_HEADS * GDN_DK
    gdn_v = GDN_HEADS * GDN_DV
    fox_w = FOX_HEADS * FOX_DH
    mem_w = MEM_HEADS * MEM_DH
    o_z = 2 * gdn_qk + gdn_v
    o_beta = o_z + gdn_v
    o_dec = o_beta + GDN_HEADS
    o_fq = o_dec + GDN_HEADS
    o_ff = o_fq + 3 * fox_w
    o_mq = o_ff + FOX_HEADS
    o_gate = o_mq + mem_w

    wt = w_in.T
    n_small = 2 * GDN_HEADS + FOX_HEADS
    w_small = jnp.concatenate([wt[o_beta:o_fq], wt[o_ff:o_mq],
                               jnp.zeros((LANES - n_small, d), F32)], axis=0)
    zpad = jnp.zeros((LANES - n_small,), F32)
    bias = jnp.concatenate([jnp.zeros((GDN_HEADS,), F32), dt_bias.astype(F32), fox_b_f.astype(F32), zpad])[None, :]
    alog = jnp.concatenate([jnp.zeros((GDN_HEADS,), F32), a_log.astype(F32), jnp.zeros((FOX_HEADS,), F32), zpad])[None, :]

    h, sm = _norm_small(x, g_mix[None, :], w_small, bias, alog)
    h2d = h.reshape(t, d)

    proj = functools.partial(_matmul, h2d, wt, tile=PROJ_TILE, w_rows_are_outputs=True)
    qkvz = proj(out_dtype=F32, w_row_ranges=[(0, o_beta)], name="proj_gdn").reshape(b, s, -1)
    att = proj(out_dtype=BF16, w_row_ranges=[(o_fq, 3 * fox_w), (o_mq, mem_w)],
               name="proj_att").reshape(b, s, -1)
    gates = proj(out_dtype=BF16, act="sigmoid", w_row_ranges=[(o_gate, N_BRANCH * d)], name="proj_gate")

    o_a = _gdn(qkvz, sm, conv_w, gdn_norm_g[None, :])
    o_b = _fox(att, sm, fox_q_norm[None, :], fox_k_norm[None, :])
    kv_m = _memkv(mem, g_mem[None, :], w_mem_kv, mem_k_norm[None, :])
    o_m = _mem_attn(att, (3 * fox_w) // mem_w, kv_m, mem_q_norm[None, :])

    y = _merge(o_a.reshape(t, -1), o_b.reshape(t, -1), o_m.reshape(t, -1), w_up_gdn, w_up_fox, w_up_mem, gates)
    x1, h2 = _out_norm(y, w_out.astype(BF16), x.reshape(t, d), g_mlp[None, :])
    u = _matmul(h2, w_ff1, out_dtype=BF16, tile=FF1_TILE, act="relu2", name="ff1")
    out = _matmul(u, w_ff2.astype(BF16), out_dtype=F32, tile=FF2_TILE, residual=x1, name="ff2")
    return out.reshape(b, s, d)


def kernel(x, mem, g_mix, w_in, conv_w, a_log, dt_bias, gdn_norm_g, fox_b_f, fox_q_norm, fox_k_norm, g_mem, w_mem_kv, mem_q_norm, mem_k_norm, w_up_gdn, w_up_fox, w_up_mem, w_out, g_mlp, w_ff1, w_ff2):
    depth = w_in.shape[0]
    for l in range(depth):
        x = _layer(x, mem, g_mix[l], w_in[l], conv_w[l], a_log[l], dt_bias[l], gdn_norm_g[l], fox_b_f[l],
                   fox_q_norm[l], fox_k_norm[l], g_mem[l], w_mem_kv[l], mem_q_norm[l], mem_k_norm[l],
                   w_up_gdn[l], w_up_fox[l], w_up_mem[l], w_out[l], g_mlp[l], w_ff1[l], w_ff2[l])
    return x
```

```python
import functools
import math

import jax
import jax.numpy as jnp
from jax import lax
from jax.experimental import pallas as pl
from jax.experimental.pallas import tpu as pltpu

F32 = jnp.float32
BF16 = jnp.bfloat16
EPS = 1e-6

GDN_HEADS = 8
GDN_DK = 128
GDN_DV = 128
GDN_CONV = 4
GDN_CHUNK = 128
GDN_HEADS_PER_STEP = 8
GDN_ROWS_PER_STEP = 256
FOX_HEADS = 8
FOX_DH = 128
FOX_TQ = 1024
FOX_TK = 512
MEM_HEADS = 4
MEM_DH = 256
N_BRANCH = 3
LANES = 128
SUBLANES = 8
NEG_BIG = -1e30
LOG2E = math.log2(math.e)

V7X_VMEM_BYTES = 64 * 1024 * 1024
VMEM_LIMIT = V7X_VMEM_BYTES - 8 * 1024 * 1024

PROJ_TILE = (2048, 512)
PROJ_BF16_TILE = (2048, 1024)
FF1_TILE = (2048, 1024)
FF2_TILE = (1024, 256)
MERGE_TILE = (2048, 256)
ROW_TILE = 512
LHS_SLABS = 8


def _tile(n, pref):
    return pref if n % pref == 0 else n


def _cparams(sem):
    return pltpu.CompilerParams(dimension_semantics=sem, vmem_limit_bytes=VMEM_LIMIT)


def _dot(a, b):
    return jnp.dot(a.astype(BF16), b.astype(BF16), preferred_element_type=F32)


def _dot_nt(a, b):
    return lax.dot_general(a.astype(BF16), b.astype(BF16), (((1,), (1,)), ((), ())),
                           preferred_element_type=F32)


def _split2(a):
    hi = a.astype(BF16)
    lo = (a - hi.astype(F32)).astype(BF16)
    return hi, lo


def _split3(a):
    hi = a.astype(BF16).astype(F32)
    r = a - hi
    mid = r.astype(BF16).astype(F32)
    lo = (r - mid).astype(BF16).astype(F32)
    return hi, mid, lo


def _dot_exact_lhs(l_bf16, v):
    hi, mid, lo = _split3(v)
    n = v.shape[1]
    d = functools.partial(jnp.dot, preferred_element_type=F32)
    both = d(l_bf16, jnp.concatenate([hi, mid], axis=1).astype(BF16))
    return both[:, :n] + both[:, n:] + d(l_bf16, lo.astype(BF16))


def _lane_col(a, idx):
    lane = lax.broadcasted_iota(jnp.int32, a.shape, 1)
    return jnp.sum(jnp.where(lane == idx, a, 0.0), axis=1, keepdims=True)


def _softplus(x):
    return jnp.maximum(x, 0.0) + jnp.log1p(jnp.exp(-jnp.abs(x)))


def _norm_small_kernel(x_ref, g_ref, ws_ref, bias_ref, alog_ref, h_ref, sm_ref,
                       carry_ref, wsplit_ref, lfull_ref, lblk_ref, *, tm):
    s = pl.program_id(1)

    @pl.when(s == 0)
    def _():
        carry_ref[...] = jnp.zeros_like(carry_ref)

    @pl.when((pl.program_id(0) == 0) & (s == 0))
    def _():
        w_hi, w_lo = _split2(ws_ref[...])
        wsplit_ref[0:LANES, :] = w_hi
        wsplit_ref[LANES:2 * LANES, :] = w_lo
        row = lax.broadcasted_iota(jnp.int32, (tm, tm), 0)
        col = lax.broadcasted_iota(jnp.int32, (tm, tm), 1)
        low = col <= row
        sh = GDN_CHUNK.bit_length() - 1
        lfull_ref[...] = jnp.where(low, 1.0, 0.0).astype(BF16)
        lblk_ref[...] = jnp.where(low, jnp.where((row >> sh) == (col >> sh), 1.0, 0.0), 0.0).astype(BF16)

    x = x_ref[0]
    h = x * lax.rsqrt(jnp.mean(x * x, axis=-1, keepdims=True) + EPS) * g_ref[...]
    h_hi, h_lo = _split2(h)
    h_ref[0] = h_hi

    nt = lambda p, q: lax.dot_general(p, q, (((1,), (1,)), ((), ())), preferred_element_type=F32)
    both = nt(h_hi, wsplit_ref[...])
    pre = both[:, :LANES] + both[:, LANES:] + nt(h_lo, wsplit_ref[0:LANES, :]) + bias_ref[...]
    lane = lax.broadcasted_iota(jnp.int32, pre.shape, 1)
    nh = GDN_HEADS
    beta = jax.nn.sigmoid(pre)
    gdec = -jnp.exp(alog_ref[...]) * _softplus(pre)
    logf = -_softplus(-pre)
    vals = jnp.where(lane < nh, beta, jnp.where(lane < 2 * nh, gdec, jnp.where(lane < 3 * nh, logf, 0.0)))

    cs_blk = _dot_exact_lhs(lblk_ref[...], vals)
    cs_full = _dot_exact_lhs(lfull_ref[...], vals) + carry_ref[0:1, :]
    hi, mid, lo = _split3(cs_full * LOG2E)
    pieces = jnp.where(lane < 3 * nh, hi, jnp.where(lane < 4 * nh, pltpu.roll(mid, nh, 1), pltpu.roll(lo, 2 * nh, 1)))
    sm_ref[0] = jnp.where(lane < nh, vals, jnp.where(lane < 2 * nh, cs_blk, jnp.where(lane < 5 * nh, pieces, 0.0)))
    carry_ref[...] = jnp.broadcast_to(cs_full[tm - 1:tm, :], carry_ref.shape)


def _norm_small(x, g, w_small, bias, alog):
    b, s, d = x.shape
    tm = _tile(s, ROW_TILE)
    return pl.pallas_call(
        functools.partial(_norm_small_kernel, tm=tm),
        out_shape=(jax.ShapeDtypeStruct((b, s, d), BF16), jax.ShapeDtypeStruct((b, s, LANES), F32)),
        grid=(b, s // tm),
        in_specs=[
            pl.BlockSpec((1, tm, d), lambda i, j: (i, j, 0)),
            pl.BlockSpec((1, d), lambda i, j: (0, 0)),
            pl.BlockSpec((LANES, d), lambda i, j: (0, 0)),
            pl.BlockSpec((1, LANES), lambda i, j: (0, 0)),
            pl.BlockSpec((1, LANES), lambda i, j: (0, 0)),
        ],
        out_specs=(
            pl.BlockSpec((1, tm, d), lambda i, j: (i, j, 0)),
            pl.BlockSpec((1, tm, LANES), lambda i, j: (i, j, 0)),
        ),
        scratch_shapes=[
            pltpu.VMEM((SUBLANES, LANES), F32),
            pltpu.VMEM((2 * LANES, d), BF16),
            pltpu.VMEM((tm, tm), BF16),
            pltpu.VMEM((tm, tm), BF16),
        ],
        compiler_params=_cparams(("arbitrary", "arbitrary")),
        name="norm_small",
    )(x, g, w_small, bias, alog)


def _out_norm_kernel(y_ref, w_ref, x_ref, g_ref, x1_ref, h_ref):
    x1 = x_ref[...] + jnp.dot(y_ref[...], w_ref[...], preferred_element_type=F32)
    x1_ref[...] = x1
    h_ref[...] = (x1 * lax.rsqrt(jnp.mean(x1 * x1, axis=-1, keepdims=True) + EPS) * g_ref[...]).astype(h_ref.dtype)


def _out_norm(y, w_bf16, x2d, g):
    t, d = x2d.shape
    k = y.shape[1]
    tm = _tile(t, ROW_TILE)
    row = lambda width: pl.BlockSpec((tm, width), lambda i: (i, 0))
    return pl.pallas_call(
        _out_norm_kernel,
        out_shape=(jax.ShapeDtypeStruct((t, d), F32), jax.ShapeDtypeStruct((t, d), BF16)),
        grid=(t // tm,),
        in_specs=[row(k), pl.BlockSpec((k, d), lambda i: (0, 0)), row(d), pl.BlockSpec((1, d), lambda i: (0, 0))],
        out_specs=(row(d), row(d)),
        compiler_params=_cparams(("arbitrary",)),
        name="out_norm",
    )(y, w_bf16, x2d, g)


def _stream_row_tile(a_hbm, a_buf, sem, *, tm, n_row_tiles, n_slabs):
    i = pl.program_id(0)
    j = pl.program_id(1)
    slot = i % 2
    slab = tm // n_slabs

    def slab_copy(row_tile, part, to_slot):
        rows = pl.ds(row_tile * tm + part * slab, slab)
        return pltpu.make_async_copy(a_hbm.at[rows, :], a_buf.at[to_slot, pl.ds(part * slab, slab), :],
                                     sem.at[to_slot])

    @pl.when((i == 0) & (j == 0))
    def _():
        for part in range(n_slabs):
            slab_copy(0, part, 0).start()

    @pl.when(j == 0)
    def _():
        for part in range(n_slabs):
            slab_copy(i, part, slot).wait()

    @pl.when((i + 1 < n_row_tiles) & (j < n_slabs))
    def _():
        slab_copy(i + 1, j, 1 - slot).start()

    return slot


def _n_slabs(tm, n_col_steps):
    for n in range(min(LHS_SLABS, n_col_steps), 0, -1):
        if tm % n == 0 and (tm // n) % (2 * SUBLANES) == 0:
            return n
    raise ValueError("row tile cannot be split into slabs")


def _row_tile_scratch(tm, k, dtype):
    return [pltpu.VMEM((2, tm, k), dtype), pltpu.SemaphoreType.DMA((2,))]


def _mm_kernel(a_hbm, w_ref, *rest, act, has_res, w_rows_are_outputs, stream):
    if has_res:
        r_ref, o_ref, a_buf, sem = rest
    else:
        o_ref, a_buf, sem = rest
    slot = _stream_row_tile(a_hbm, a_buf, sem, **stream)
    contract = (((1,), (1,)), ((), ())) if w_rows_are_outputs else (((1,), (0,)), ((), ()))
    acc = lax.dot_general(a_buf[slot], w_ref[...].astype(BF16), contract, preferred_element_type=F32)
    if act == "sigmoid":
        acc = 0.5 * jnp.tanh(0.5 * acc) + 0.5
    elif act == "relu2":
        r = jnp.maximum(acc, 0.0)
        acc = r * r
    if has_res:
        acc = acc + r_ref[...]
    o_ref[...] = acc.astype(o_ref.dtype)


def _matmul(a, w, *, out_dtype, tile, act=None, residual=None, w_rows_are_outputs=False,
            w_row_ranges=None, name="matmul"):
    m, k = a.shape
    if w_row_ranges is not None:
        assert w_rows_are_outputs
        n = sum(r for _, r in w_row_ranges)
    else:
        n = w.shape[0] if w_rows_are_outputs else w.shape[1]
    tm = _tile(m, tile[0])
    tn = _tile(n, tile[1])
    if w_row_ranges is not None:
        assert all(r % tn == 0 and f % 8 == 0 for f, r in w_row_ranges)

        def w_rows(i, j):
            start, first_blk = jnp.int32(0), 0
            for f, r in w_row_ranges:
                start = jnp.where(j >= first_blk, f + (j - first_blk) * tn, start)
                first_blk += r // tn
            return pl.multiple_of(start, 8), 0

        w_spec = pl.BlockSpec((pl.Element(tn), pl.Element(k)), w_rows)
    elif w_rows_are_outputs:
        w_spec = pl.BlockSpec((tn, k), lambda i, j: (j, 0))
    else:
        w_spec = pl.BlockSpec((k, tn), lambda i, j: (0, j))
    in_specs = [pl.BlockSpec(memory_space=pl.ANY), w_spec]
    args = [a, w]
    if residual is not None:
        in_specs.append(pl.BlockSpec((tm, tn), lambda i, j: (i, j)))
        args.append(residual)
    stream = dict(tm=tm, n_row_tiles=m // tm, n_slabs=_n_slabs(tm, n // tn))
    return pl.pallas_call(
        functools.partial(_mm_kernel, act=act, has_res=residual is not None,
                          w_rows_are_outputs=w_rows_are_outputs, stream=stream),
        out_shape=jax.ShapeDtypeStruct((m, n), out_dtype),
        grid=(m // tm, n // tn),
        in_specs=in_specs,
        out_specs=pl.BlockSpec((tm, tn), lambda i, j: (i, j)),
        scratch_shapes=_row_tile_scratch(tm, k, a.dtype),
        compiler_params=_cparams(("arbitrary", "arbitrary")),
        name=name,
    )(*args)


def _gdn_kernel(q_ref, k_ref, v_ref, z_ref, sm_ref, cwq_ref, cwk_ref, cwv_ref, gn_ref, o_ref,
                state_ref, tail_ref, ext_ref, qs, ks, vs, *, ts, nc, hb, nb):
    hg = pl.program_id(0)
    si = pl.program_id(1)
    c_sz = GDN_CHUNK

    @pl.when(si == 0)
    def _():
        state_ref[...] = jnp.zeros_like(state_ref)
        tail_ref[...] = jnp.zeros_like(tail_ref)

    def conv_silu(x_ref, w_ref, bi, idx, c):
        r0 = c * c_sz
        x = x_ref[bi, r0:r0 + c_sz, :]
        w = w_ref[...]
        y = x * w[0:1, :]
        for i in range(1, GDN_CONV):
            y = pltpu.roll(y, 1, 0) + x * w[i:i + 1, :]
        sl = SUBLANES
        ext_ref[idx, 0:sl, :] = tail_ref[bi, idx] if c == 0 else x_ref[bi, r0 - sl:r0, :]
        ext_ref[idx, sl:2 * sl, :] = x[0:sl, :]
        first = sl - (GDN_CONV - 1)
        head = ext_ref[idx, first:first + sl, :] * w[0:1, :]
        for i in range(1, GDN_CONV):
            head = head + ext_ref[idx, first + i:first + i + sl, :] * w[i:i + 1, :]
        y = jnp.concatenate([head, y[sl:, :]], axis=0)
        if c == nc - 1:
            tail_ref[bi, idx] = x[c_sz - sl:c_sz, :]
        return y * jax.nn.sigmoid(y)

    def prepare(c):
        rows = slice(c * c_sz, (c + 1) * c_sz)
        for bi in range(nb):
            q = conv_silu(q_ref, cwq_ref, bi, 0, c)
            k = conv_silu(k_ref, cwk_ref, bi, 1, c)
            vs[bi, rows, :] = conv_silu(v_ref, cwv_ref, bi, 2, c)
            for hh in range(hb):
                cs = slice(hh * LANES, (hh + 1) * LANES)
                qh = q[:, cs]
                kh = k[:, cs]
                qs[bi, rows, cs] = qh * (lax.rsqrt(jnp.sum(qh * qh, axis=-1, keepdims=True) + EPS)
                                         * (GDN_DK ** -0.5))
                ks[bi, rows, cs] = kh * lax.rsqrt(jnp.sum(kh * kh, axis=-1, keepdims=True) + EPS)

    ri = lax.broadcasted_iota(jnp.int32, (c_sz, c_sz), 0)
    ci = lax.broadcasted_iota(jnp.int32, (c_sz, c_sz), 1)
    strict = ri > ci
    incl = ri >= ci
    eye = jnp.where(ri == ci, 1.0, 0.0)
    n_lvl = c_sz.bit_length() - 1
    lvl_masks = []
    for l in range(n_lvl):
        same = (ri >> (l + 1)) == (ci >> (l + 1))
        lvl_masks.append(jnp.where(
            same, jnp.where(((ri >> l) & 1) == 1, jnp.where(((ci >> l) & 1) == 0, 1.0, 0.0), 0.0), 0.0))

    def advance(c):
        rows = slice(c * c_sz, (c + 1) * c_sz)
        chains = [(bi, hh) for bi in range(nb) for hh in range(hb)]
        mlows, rhss, lhs2s, qds, cds = [], [], [], [], []
        for bi, hh in chains:
            cs = slice(hh * LANES, (hh + 1) * LANES)
            hd = hg * hb + hh
            smc = sm_ref[bi, rows, :]
            beta = _lane_col(smc, hd)
            gam = _lane_col(smc, hd + GDN_HEADS)
            kc = ks[bi, rows, cs]
            qc = qs[bi, rows, cs]
            egam = jnp.exp(gam)
            g_last = gam[c_sz - 1:c_sz, :]
            gcol = jnp.broadcast_to(gam, (c_sz, c_sz))
            diff = gcol - gcol.T
            e = jnp.exp(jnp.where(incl, diff, 0.0))
            mlows.append(beta * _dot_nt(kc, kc) * jnp.where(strict, e, 0.0))
            qk = _dot_nt(qc, kc) * jnp.where(incl, e, 0.0)
            kd_t = (kc * jnp.exp(g_last - gam)).T
            lhs2s.append(jnp.concatenate([qk, kd_t], axis=0).astype(BF16))
            rhss.append(jnp.concatenate([kc * (beta * egam), vs[bi, rows, cs] * beta], axis=1).astype(BF16))
            qds.append(qc * egam)
            cds.append(jnp.exp(g_last))
        xs = [eye - lvl_masks[0] * m for m in mlows]
        mlows_bf = [m.astype(BF16) for m in mlows]
        for l in range(1, n_lvl):
            mask_bf = lvl_masks[l].astype(BF16)
            tl = [_dot(mask_bf * m, x) for m, x in zip(mlows_bf, xs)]
            xs = [x - _dot(x, t) for x, t in zip(xs, tl)]
        wus = [_dot(x, r) for x, r in zip(xs, rhss)]
        for n, (bi, hh) in enumerate(chains):
            cs = slice(hh * LANES, (hh + 1) * LANES)
            state = state_ref[n]
            wu = wus[n]
            a = _dot(jnp.concatenate([wu[:, :GDN_DK], qds[n]], axis=0), state)
            u = wu[:, GDN_DK:] - a[:c_sz]
            b2 = jnp.dot(lhs2s[n], u.astype(BF16), preferred_element_type=F32)
            o = a[c_sz:] + b2[:c_sz]
            state_ref[n] = state * cds[n] + b2[c_sz:]
            z = z_ref[bi, rows, cs]
            on = o * lax.rsqrt(jnp.mean(o * o, axis=-1, keepdims=True) + EPS) * gn_ref[...]
            o_ref[bi, rows, cs] = (on * (z * jax.nn.sigmoid(z))).astype(o_ref.dtype)

    for c in range(nc):
        prepare(c)
        advance(c)


def _gdn(qkvz, sm, conv_w, gn):
    b, s, _ = qkvz.shape
    nh = GDN_HEADS
    hb = GDN_HEADS_PER_STEP
    ng = nh // hb
    wb = hb * LANES
    ts = _tile(s, GDN_ROWS_PER_STEP)
    nc = ts // GDN_CHUNK
    blk = lambda off: pl.BlockSpec((b, ts, wb), lambda h, j: (0, j, off + h))
    cw = lambda off: pl.BlockSpec((GDN_CONV, wb), lambda h, j: (0, off + h))
    return pl.pallas_call(
        functools.partial(_gdn_kernel, ts=ts, nc=nc, hb=hb, nb=b),
        out_shape=jax.ShapeDtypeStruct((b, s, nh * GDN_DV), BF16),
        grid=(ng, s // ts),
        in_specs=[
            blk(0), blk(ng), blk(2 * ng), blk(3 * ng),
            pl.BlockSpec((b, ts, LANES), lambda h, j: (0, j, 0)),
            cw(0), cw(ng), cw(2 * ng),
            pl.BlockSpec((1, GDN_DV), lambda h, j: (0, 0)),
        ],
        out_specs=pl.BlockSpec((b, ts, wb), lambda h, j: (0, j, h)),
        scratch_shapes=[
            pltpu.VMEM((b * hb, GDN_DK, GDN_DV), F32),
            pltpu.VMEM((b, 3, SUBLANES, wb), F32),
            pltpu.VMEM((3, 2 * SUBLANES, wb), F32),
            pltpu.VMEM((b, ts, wb), F32),
            pltpu.VMEM((b, ts, wb), F32),
            pltpu.VMEM((b, ts, wb), F32),
        ],
        compiler_params=_cparams(("arbitrary", "arbitrary")),
        name="gdn",
    )(qkvz, qkvz, qkvz, qkvz, sm, conv_w, conv_w, conv_w, gn)


def _fox_kernel(q_ref, k_ref, v_ref, smq_ref, smk_ref, gq_ref, gk_ref, o_ref,
                kaug_ref, vaug_ref, m_s, acc_s, sa_ref, sb_ref, *, tq, tk, nk):
    hd = pl.program_id(1)
    qi = pl.program_id(2)
    n_sub = tq // tk
    n_lt = tk // LANES

    src = lax.broadcasted_iota(jnp.int32, (LANES, LANES), 0)
    dst = lax.broadcasted_iota(jnp.int32, (LANES, LANES), 1)
    first_piece = 2 * GDN_HEADS + hd

    def bias_cols(sm_block, dst0, sign, ones0):
        pick = jnp.where((dst >= dst0) & (dst < dst0 + 3) & (src == first_piece + FOX_HEADS * (dst - dst0)),
                         sign, 0.0).astype(BF16)
        lane = lax.broadcasted_iota(jnp.int32, sm_block.shape, 1)
        ones = jnp.where((lane >= ones0) & (lane < ones0 + 3), 1.0, 0.0)
        return (jnp.dot(sm_block.astype(BF16), pick, preferred_element_type=F32) + ones).astype(BF16)

    @pl.when(qi == 0)
    def _():
        def build(j, carry):
            rows = pl.ds(pl.multiple_of(j * tk, tk), tk)
            kk = k_ref[0, rows, :].astype(F32)
            kn = kk * lax.rsqrt(jnp.mean(kk * kk, axis=-1, keepdims=True) + EPS) * gk_ref[...]
            kaug_ref[rows, 0:FOX_DH] = kn.astype(BF16)
            kaug_ref[rows, FOX_DH:2 * FOX_DH] = bias_cols(smk_ref[0, rows, :], 3, -1.0, 0)
            vaug_ref[rows, 0:FOX_DH] = v_ref[0, rows, :]
            vaug_ref[rows, FOX_DH:2 * FOX_DH] = jnp.ones((tk, FOX_DH), BF16)
            return carry
        lax.fori_loop(0, nk, build, 0)

    qq = q_ref[0].astype(F32)
    qn = qq * (lax.rsqrt(jnp.mean(qq * qq, axis=-1, keepdims=True) + EPS) * (FOX_DH ** -0.5 * LOG2E)) * gq_ref[...]
    q_aug = jnp.concatenate([qn.astype(BF16), bias_cols(smq_ref[0], 0, 1.0, 3)], axis=1)

    m_s[...] = jnp.full_like(m_s, NEG_BIG)
    acc_s[...] = jnp.zeros_like(acc_s)

    def scores(s_ref, j):
        rows = pl.ds(pl.multiple_of(j * tk, tk), tk)
        s_ref[...] = lax.dot_general(q_aug, kaug_ref[rows, :], (((1,), (1,)), ((), ())),
                                     preferred_element_type=F32)

    def accumulate(s_ref, j, diag):
        rows = pl.ds(pl.multiple_of(j * tk, tk), tk)
        v_blk = vaug_ref[rows, :]
        groups = ([(0, tq, False)] if diag is None
                  else [(r * tk, (r + 1) * tk, r == diag) for r in range(diag, n_sub)])
        for lo_row, hi_row, masked in groups:
            rs = slice(lo_row, hi_row)
            tiles = [s_ref[rs, c * LANES:(c + 1) * LANES] for c in range(n_lt)]
            if masked:
                ri = lax.broadcasted_iota(jnp.int32, (tk, LANES), 0)
                ci = lax.broadcasted_iota(jnp.int32, (tk, LANES), 1)
                tiles = [jnp.where(ri >= ci + c * LANES, t, NEG_BIG) for c, t in enumerate(tiles)]
            mx = tiles[0]
            for t in tiles[1:]:
                mx = jnp.maximum(mx, t)
            m_prev = m_s[rs, :]
            m_next = jnp.maximum(m_prev, jnp.max(mx, axis=1, keepdims=True))
            p = jnp.concatenate([jnp.exp2(t - m_next) for t in tiles], axis=1).astype(BF16)
            alpha = jnp.exp2(m_prev - m_next)
            acc_s[rs, :] = (jnp.concatenate([alpha, alpha], axis=1) * acc_s[rs, :]
                            + jnp.dot(p, v_blk, preferred_element_type=F32))
            m_s[rs, :] = m_next

    scores(sa_ref, 0)

    def body(i, carry):
        j = 2 * i
        scores(sb_ref, j + 1)
        accumulate(sa_ref, j, None)
        scores(sa_ref, j + 2)
        accumulate(sb_ref, j + 1, None)
        return carry

    n_below = n_sub * qi
    lax.fori_loop(0, n_below // 2, body, 0)
    bufs = (sa_ref, sb_ref)
    for e in range(n_sub):
        if e + 1 < n_sub:
            scores(bufs[(e + 1) % 2], n_below + e + 1)
        accumulate(bufs[e % 2], n_below + e, e)

    acc = acc_s[...]
    o_ref[0] = (acc[:, :FOX_DH] / acc[:, FOX_DH:]).astype(o_ref.dtype)


def _fox(qkv, sm, gq, gk):
    b, s, _ = qkv.shape
    nh = FOX_HEADS
    tq = _tile(s, FOX_TQ)
    tk = _tile(tq, FOX_TK)
    assert (tq // tk) % 2 == 0, "the score pipeline consumes key blocks in pairs"
    nk = s // tk
    return pl.pallas_call(
        functools.partial(_fox_kernel, tq=tq, tk=tk, nk=nk),
        out_shape=jax.ShapeDtypeStruct((b, s, nh * FOX_DH), BF16),
        grid=(b, nh, s // tq),
        in_specs=[
            pl.BlockSpec((1, tq, FOX_DH), lambda i, h, j: (i, j, h)),
            pl.BlockSpec((1, s, FOX_DH), lambda i, h, j: (i, 0, nh + h)),
            pl.BlockSpec((1, s, FOX_DH), lambda i, h, j: (i, 0, 2 * nh + h)),
            pl.BlockSpec((1, tq, LANES), lambda i, h, j: (i, j, 0)),
            pl.BlockSpec((1, s, LANES), lambda i, h, j: (i, 0, 0)),
            pl.BlockSpec((1, FOX_DH), lambda i, h, j: (0, 0)),
            pl.BlockSpec((1, FOX_DH), lambda i, h, j: (0, 0)),
        ],
        out_specs=pl.BlockSpec((1, tq, FOX_DH), lambda i, h, j: (i, j, h)),
        scratch_shapes=[
            pltpu.VMEM((s, 2 * FOX_DH), BF16),
            pltpu.VMEM((s, 2 * FOX_DH), BF16),
            pltpu.VMEM((tq, LANES), F32),
            pltpu.VMEM((tq, 2 * FOX_DH), F32),
            pltpu.VMEM((tq, tk), F32),
            pltpu.VMEM((tq, tk), F32),
        ],
        compiler_params=_cparams(("arbitrary", "arbitrary", "arbitrary")),
        name="fox",
    )(qkv, qkv, qkv, sm, sm, gq, gk)


def _memkv_kernel(mem_ref, g_ref, w_ref, gk_ref, o_ref, *, n_k_tiles):
    j = pl.program_id(1)
    m = mem_ref[0]
    hn = (m * lax.rsqrt(jnp.mean(m * m, axis=-1, keepdims=True) + EPS) * g_ref[...]).astype(BF16)
    r = jnp.dot(hn, w_ref[...].astype(BF16), preferred_element_type=F32)
    parts = []
    for t in range(r.shape[1] // MEM_DH):
        rt = r[:, t * MEM_DH:(t + 1) * MEM_DH]
        parts.append(rt * lax.rsqrt(jnp.mean(rt * rt, axis=-1, keepdims=True) + EPS) * gk_ref[...])
    normed = jnp.concatenate(parts, axis=1)
    is_k = jnp.where(j < n_k_tiles, 1.0, 0.0)
    o_ref[0] = (is_k * normed + (1.0 - is_k) * r).astype(o_ref.dtype)


def _memkv(mem, g, w, gk):
    b, ml, d = mem.shape
    n = w.shape[1]
    tn = 2 * MEM_DH
    return pl.pallas_call(
        functools.partial(_memkv_kernel, n_k_tiles=(n // 2) // tn),
        out_shape=jax.ShapeDtypeStruct((b, ml, n), BF16),
        grid=(b, n // tn),
        in_specs=[
            pl.BlockSpec((1, ml, d), lambda i, j: (i, 0, 0)),
            pl.BlockSpec((1, d), lambda i, j: (0, 0)),
            pl.BlockSpec((d, tn), lambda i, j: (0, j)),
            pl.BlockSpec((1, MEM_DH), lambda i, j: (0, 0)),
        ],
        out_specs=pl.BlockSpec((1, ml, tn), lambda i, j: (i, 0, j)),
        compiler_params=_cparams(("arbitrary", "arbitrary")),
        name="memkv",
    )(mem, g, w, gk)


def _mem_kernel(q_ref, k_ref, v_ref, gq_ref, o_ref):
    for hd in range(MEM_HEADS):
        cs = slice(hd * MEM_DH, (hd + 1) * MEM_DH)
        qq = q_ref[0, :, cs].astype(F32)
        qn = qq * lax.rsqrt(jnp.mean(qq * qq, axis=-1, keepdims=True) + EPS) * gq_ref[...] * (MEM_DH ** -0.5)
        s = _dot_nt(qn, k_ref[0, :, cs])
        p = jnp.exp(s - jnp.max(s, axis=1, keepdims=True))
        p = p / jnp.sum(p, axis=1, keepdims=True)
        o_ref[0, :, cs] = jnp.dot(p.astype(BF16), v_ref[0, :, cs], preferred_element_type=F32).astype(o_ref.dtype)


def _mem_attn(qsrc, q_col_block, kv, gq):
    b, s, _ = qsrc.shape
    ml = kv.shape[1]
    wq = MEM_HEADS * MEM_DH
    tq = _tile(s, ROW_TILE)
    return pl.pallas_call(
        _mem_kernel,
        out_shape=jax.ShapeDtypeStruct((b, s, wq), BF16),
        grid=(b, s // tq),
        in_specs=[
            pl.BlockSpec((1, tq, wq), lambda i, j: (i, j, q_col_block)),
            pl.BlockSpec((1, ml, wq), lambda i, j: (i, 0, 0)),
            pl.BlockSpec((1, ml, wq), lambda i, j: (i, 0, 1)),
            pl.BlockSpec((1, MEM_DH), lambda i, j: (0, 0)),
        ],
        out_specs=pl.BlockSpec((1, tq, wq), lambda i, j: (i, j, 0)),
        compiler_params=_cparams(("arbitrary", "arbitrary")),
        name="mem_attn",
    )(qsrc, kv, kv, gq)


def _merge_kernel(oa_ref, ob_ref, om_ref, wa_ref, wb_ref, wm_ref, ga_ref, gb_ref, gm_ref, y_ref):
    d = lambda o_ref, w_ref: jnp.dot(o_ref[...], w_ref[...].astype(BF16), preferred_element_type=F32)
    y = (ga_ref[...].astype(F32) * d(oa_ref, wa_ref)
         + gb_ref[...].astype(F32) * d(ob_ref, wb_ref)
         + gm_ref[...].astype(F32) * d(om_ref, wm_ref))
    y_ref[...] = y.astype(y_ref.dtype)


def _merge(oa, ob, om, wa, wb, wm, gates):
    t, ka = oa.shape
    dm = wa.shape[1]
    tm = _tile(t, MERGE_TILE[0])
    tn = _tile(dm, MERGE_TILE[1])
    nb = dm // tn
    a_spec = lambda kk: pl.BlockSpec((tm, kk), lambda i, j: (i, 0))
    w_spec = lambda kk: pl.BlockSpec((kk, tn), lambda i, j: (0, j))
    g_spec = lambda off: pl.BlockSpec((tm, tn), lambda i, j: (i, off * nb + j))
    return pl.pallas_call(
        _merge_kernel,
        out_shape=jax.ShapeDtypeStruct((t, dm), BF16),
        grid=(t // tm, nb),
        in_specs=[a_spec(ka), a_spec(ob.shape[1]), a_spec(om.shape[1]),
                  w_spec(ka), w_spec(ob.shape[1]), w_spec(om.shape[1]),
                  g_spec(0), g_spec(1), g_spec(2)],
        out_specs=pl.BlockSpec((tm, tn), lambda i, j: (i, j)),
        compiler_params=_cparams(("arbitrary", "arbitrary")),
        name="merge",
    )(oa, ob, om, wa, wb, wm, gates, gates, gates)


def _layer(x, mem, g_mix, w_in, conv_w, a_log, dt_bias, gdn_norm_g, fox_b_f, fox_q_norm, fox_k_norm,
           g_mem, w_mem_kv, mem_q_norm, mem_k_norm, w_up_gdn, w_up_fox, w_up_mem, w_out, g_mlp, w_ff1, w_ff2):
    b, s, d = x.shape
    t = b * s
    gdn_qk = GDN_HEADS * GDN_DK
    gdn_v = GDN_HEADS * GDN_DV
    fox_w = FOX_HEADS * FOX_DH
    mem_w = MEM_HEADS * MEM_DH
    o_z = 2 * gdn_qk + gdn_v
    o_beta = o_z + gdn_v
    o_dec = o_beta + GDN_HEADS
    o_fq = o_dec + GDN_HEADS
    o_ff = o_fq + 3 * fox_w
    o_mq = o_ff + FOX_HEADS
    o_gate = o_mq + mem_w

    wt = w_in.T
    n_small = 2 * GDN_HEADS + FOX_HEADS
    w_small = jnp.concatenate([wt[o_beta:o_fq], wt[o_ff:o_mq],
                               jnp.zeros((LANES - n_small, d), F32)], axis=0)
    zpad = jnp.zeros((LANES - n_small,), F32)
    bias = jnp.concatenate([jnp.zeros((GDN_HEADS,), F32), dt_bias.astype(F32), fox_b_f.astype(F32), zpad])[None, :]
    alog = jnp.concatenate([jnp.zeros((GDN_HEADS,), F32), a_log.astype(F32), jnp.zeros((FOX_HEADS,), F32), zpad])[None, :]

    h, sm = _norm_small(x, g_mix[None, :], w_small, bias, alog)
    h2d = h.reshape(t, d)

    proj = functools.partial(_matmul, h2d, wt, tile=PROJ_TILE, w_rows_are_outputs=True)
    qkvz = proj(out_dtype=F32, w_row_ranges=[(0, o_beta)], name="proj_gdn").reshape(b, s, -1)
    att = proj(out_dtype=BF16, tile=PROJ_BF16_TILE, w_row_ranges=[(o_fq, 3 * fox_w), (o_mq, mem_w)],
               name="proj_att").reshape(b, s, -1)
    gates = proj(out_dtype=BF16, tile=PROJ_BF16_TILE, act="sigmoid", w_row_ranges=[(o_gate, N_BRANCH * d)],
                 name="proj_gate")

    o_a = _gdn(qkvz, sm, conv_w, gdn_norm_g[None, :])
    o_b = _fox(att, sm, fox_q_norm[None, :], fox_k_norm[None, :])
    kv_m = _memkv(mem, g_mem[None, :], w_mem_kv, mem_k_norm[None, :])
    o_m = _mem_attn(att, (3 * fox_w) // mem_w, kv_m, mem_q_norm[None, :])

    y = _merge(o_a.reshape(t, -1), o_b.reshape(t, -1), o_m.reshape(t, -1), w_up_gdn, w_up_fox, w_up_mem, gates)
    x1, h2 = _out_norm(y, w_out.astype(BF16), x.reshape(t, d), g_mlp[None, :])
    u = _matmul(h2, w_ff1, out_dtype=BF16, tile=FF1_TILE, act="relu2", name="ff1")
    out = _matmul(u, w_ff2.astype(BF16), out_dtype=F32, tile=FF2_TILE, residual=x1, name="ff2")
    return out.reshape(b, s, d)


def kernel(x, mem, g_mix, w_in, conv_w, a_log, dt_bias, gdn_norm_g, fox_b_f, fox_q_norm, fox_k_norm, g_mem, w_mem_kv, mem_q_norm, mem_k_norm, w_up_gdn, w_up_fox, w_up_mem, w_out, g_mlp, w_ff1, w_ff2):
    depth = w_in.shape[0]
    for l in range(depth):
        x = _layer(x, mem, g_mix[l], w_in[l], conv_w[l], a_log[l], dt_bias[l], gdn_norm_g[l], fox_b_f[l],
                   fox_q_norm[l], fox_k_norm[l], g_mem[l], w_mem_kv[l], mem_q_norm[l], mem_k_norm[l],
                   w_up_gdn[l], w_up_fox[l], w_up_mem[l], w_out[l], g_mlp[l], w_ff1[l], w_ff2[l])
    return x
```

```python
import functools
import math

import jax
import jax.numpy as jnp
from jax import lax
from jax.experimental import pallas as pl
from jax.experimental.pallas import tpu as pltpu

F32 = jnp.float32
BF16 = jnp.bfloat16
EPS = 1e-6

GDN_HEADS = 8
GDN_DK = 128
GDN_DV = 128
GDN_CONV = 4
GDN_CHUNK = 128
GDN_HEADS_PER_STEP = 8
GDN_ROWS_PER_STEP = 256
FOX_HEADS = 8
FOX_DH = 128
FOX_TQ = 1024
FOX_TK = 512
MEM_HEADS = 4
MEM_DH = 256
N_BRANCH = 3
LANES = 128
SUBLANES = 8
NEG_BIG = -1e30
LOG2E = math.log2(math.e)

V7X_VMEM_BYTES = 64 * 1024 * 1024
VMEM_LIMIT = V7X_VMEM_BYTES - 8 * 1024 * 1024

PROJ_TILE = (2048, 512)
PROJ_BF16_TILE = (2048, 1024)
FF1_TILE = (2048, 1024)
FF2_TILE = (1024, 256)
MERGE_TILE = (2048, 256)
ROW_TILE = 512
LHS_SLABS = 8


def _tile(n, pref):
    return pref if n % pref == 0 else n


def _cparams(sem):
    return pltpu.CompilerParams(dimension_semantics=sem, vmem_limit_bytes=VMEM_LIMIT)


def _dot(a, b):
    return jnp.dot(a.astype(BF16), b.astype(BF16), preferred_element_type=F32)


def _dot_nt(a, b):
    return lax.dot_general(a.astype(BF16), b.astype(BF16), (((1,), (1,)), ((), ())),
                           preferred_element_type=F32)


def _split2(a):
    hi = a.astype(BF16)
    lo = (a - hi.astype(F32)).astype(BF16)
    return hi, lo


def _split3(a):
    hi = a.astype(BF16).astype(F32)
    r = a - hi
    mid = r.astype(BF16).astype(F32)
    lo = (r - mid).astype(BF16).astype(F32)
    return hi, mid, lo


def _dot_exact_lhs(l_bf16, v):
    hi, mid, lo = _split3(v)
    n = v.shape[1]
    d = functools.partial(jnp.dot, preferred_element_type=F32)
    both = d(l_bf16, jnp.concatenate([hi, mid], axis=1).astype(BF16))
    return both[:, :n] + both[:, n:] + d(l_bf16, lo.astype(BF16))


def _lane_col(a, idx):
    lane = lax.broadcasted_iota(jnp.int32, a.shape, 1)
    return jnp.sum(jnp.where(lane == idx, a, 0.0), axis=1, keepdims=True)


def _softplus(x):
    return jnp.maximum(x, 0.0) + jnp.log1p(jnp.exp(-jnp.abs(x)))


def _norm_small_kernel(x_ref, g_ref, ws_ref, bias_ref, alog_ref, h_ref, sm_ref,
                       carry_ref, wsplit_ref, lfull_ref, lblk_ref, *, tm):
    s = pl.program_id(1)

    @pl.when(s == 0)
    def _():
        carry_ref[...] = jnp.zeros_like(carry_ref)

    @pl.when((pl.program_id(0) == 0) & (s == 0))
    def _():
        w_hi, w_lo = _split2(ws_ref[...])
        wsplit_ref[0:LANES, :] = w_hi
        wsplit_ref[LANES:2 * LANES, :] = w_lo
        row = lax.broadcasted_iota(jnp.int32, (tm, tm), 0)
        col = lax.broadcasted_iota(jnp.int32, (tm, tm), 1)
        low = col <= row
        sh = GDN_CHUNK.bit_length() - 1
        lfull_ref[...] = jnp.where(low, 1.0, 0.0).astype(BF16)
        lblk_ref[...] = jnp.where(low, jnp.where((row >> sh) == (col >> sh), 1.0, 0.0), 0.0).astype(BF16)

    x = x_ref[0]
    h = x * lax.rsqrt(jnp.mean(x * x, axis=-1, keepdims=True) + EPS) * g_ref[...]
    h_hi, h_lo = _split2(h)
    h_ref[0] = h_hi

    nt = lambda p, q: lax.dot_general(p, q, (((1,), (1,)), ((), ())), preferred_element_type=F32)
    both = nt(h_hi, wsplit_ref[...])
    pre = both[:, :LANES] + both[:, LANES:] + nt(h_lo, wsplit_ref[0:LANES, :]) + bias_ref[...]
    lane = lax.broadcasted_iota(jnp.int32, pre.shape, 1)
    nh = GDN_HEADS
    beta = jax.nn.sigmoid(pre)
    gdec = -jnp.exp(alog_ref[...]) * _softplus(pre)
    logf = -_softplus(-pre)
    vals = jnp.where(lane < nh, beta, jnp.where(lane < 2 * nh, gdec, jnp.where(lane < 3 * nh, logf, 0.0)))

    cs_blk = _dot_exact_lhs(lblk_ref[...], vals)
    cs_full = _dot_exact_lhs(lfull_ref[...], vals) + carry_ref[0:1, :]
    hi, mid, lo = _split3(cs_full * LOG2E)
    pieces = jnp.where(lane < 3 * nh, hi, jnp.where(lane < 4 * nh, pltpu.roll(mid, nh, 1), pltpu.roll(lo, 2 * nh, 1)))
    sm_ref[0] = jnp.where(lane < nh, vals, jnp.where(lane < 2 * nh, cs_blk, jnp.where(lane < 5 * nh, pieces, 0.0)))
    carry_ref[...] = jnp.broadcast_to(cs_full[tm - 1:tm, :], carry_ref.shape)


def _norm_small(x, g, w_small, bias, alog):
    b, s, d = x.shape
    tm = _tile(s, ROW_TILE)
    return pl.pallas_call(
        functools.partial(_norm_small_kernel, tm=tm),
        out_shape=(jax.ShapeDtypeStruct((b, s, d), BF16), jax.ShapeDtypeStruct((b, s, LANES), F32)),
        grid=(b, s // tm),
        in_specs=[
            pl.BlockSpec((1, tm, d), lambda i, j: (i, j, 0)),
            pl.BlockSpec((1, d), lambda i, j: (0, 0)),
            pl.BlockSpec((LANES, d), lambda i, j: (0, 0)),
            pl.BlockSpec((1, LANES), lambda i, j: (0, 0)),
            pl.BlockSpec((1, LANES), lambda i, j: (0, 0)),
        ],
        out_specs=(
            pl.BlockSpec((1, tm, d), lambda i, j: (i, j, 0)),
            pl.BlockSpec((1, tm, LANES), lambda i, j: (i, j, 0)),
        ),
        scratch_shapes=[
            pltpu.VMEM((SUBLANES, LANES), F32),
            pltpu.VMEM((2 * LANES, d), BF16),
            pltpu.VMEM((tm, tm), BF16),
            pltpu.VMEM((tm, tm), BF16),
        ],
        compiler_params=_cparams(("arbitrary", "arbitrary")),
        name="norm_small",
    )(x, g, w_small, bias, alog)


def _out_norm_kernel(y_ref, w_ref, x_ref, g_ref, x1_ref, h_ref):
    x1 = x_ref[...] + jnp.dot(y_ref[...], w_ref[...], preferred_element_type=F32)
    x1_ref[...] = x1
    h_ref[...] = (x1 * lax.rsqrt(jnp.mean(x1 * x1, axis=-1, keepdims=True) + EPS) * g_ref[...]).astype(h_ref.dtype)


def _out_norm(y, w_bf16, x2d, g):
    t, d = x2d.shape
    k = y.shape[1]
    tm = _tile(t, ROW_TILE)
    row = lambda width: pl.BlockSpec((tm, width), lambda i: (i, 0))
    return pl.pallas_call(
        _out_norm_kernel,
        out_shape=(jax.ShapeDtypeStruct((t, d), F32), jax.ShapeDtypeStruct((t, d), BF16)),
        grid=(t // tm,),
        in_specs=[row(k), pl.BlockSpec((k, d), lambda i: (0, 0)), row(d), pl.BlockSpec((1, d), lambda i: (0, 0))],
        out_specs=(row(d), row(d)),
        compiler_params=_cparams(("arbitrary",)),
        name="out_norm",
    )(y, w_bf16, x2d, g)


def _stream_row_tile(a_hbm, a_buf, sem, *, tm, n_row_tiles, n_slabs):
    i = pl.program_id(0)
    j = pl.program_id(1)
    slot = i % 2
    slab = tm // n_slabs

    def slab_copy(row_tile, part, to_slot):
        rows = pl.ds(row_tile * tm + part * slab, slab)
        return pltpu.make_async_copy(a_hbm.at[rows, :], a_buf.at[to_slot, pl.ds(part * slab, slab), :],
                                     sem.at[to_slot])

    @pl.when((i == 0) & (j == 0))
    def _():
        for part in range(n_slabs):
            slab_copy(0, part, 0).start()

    @pl.when(j == 0)
    def _():
        for part in range(n_slabs):
            slab_copy(i, part, slot).wait()

    @pl.when((i + 1 < n_row_tiles) & (j < n_slabs))
    def _():
        slab_copy(i + 1, j, 1 - slot).start(priority=1)

    return slot


def _n_slabs(tm, n_col_steps):
    for n in range(min(LHS_SLABS, n_col_steps), 0, -1):
        if tm % n == 0 and (tm // n) % (2 * SUBLANES) == 0:
            return n
    raise ValueError("row tile cannot be split into slabs")


def _row_tile_scratch(tm, k, dtype):
    return [pltpu.VMEM((2, tm, k), dtype), pltpu.SemaphoreType.DMA((2,))]


def _mm_kernel(a_hbm, w_ref, *rest, act, has_res, w_rows_are_outputs, stream):
    if has_res:
        r_ref, o_ref, a_buf, sem = rest
    else:
        o_ref, a_buf, sem = rest
    slot = _stream_row_tile(a_hbm, a_buf, sem, **stream)
    contract = (((1,), (1,)), ((), ())) if w_rows_are_outputs else (((1,), (0,)), ((), ()))
    acc = lax.dot_general(a_buf[slot], w_ref[...].astype(BF16), contract, preferred_element_type=F32)
    if act == "sigmoid":
        acc = 0.5 * jnp.tanh(0.5 * acc) + 0.5
    elif act == "relu2":
        r = jnp.maximum(acc, 0.0)
        acc = r * r
    if has_res:
        acc = acc + r_ref[...]
    o_ref[...] = acc.astype(o_ref.dtype)


def _matmul(a, w, *, out_dtype, tile, act=None, residual=None, w_rows_are_outputs=False,
            w_row_ranges=None, name="matmul"):
    m, k = a.shape
    if w_row_ranges is not None:
        assert w_rows_are_outputs
        n = sum(r for _, r in w_row_ranges)
    else:
        n = w.shape[0] if w_rows_are_outputs else w.shape[1]
    tm = _tile(m, tile[0])
    tn = _tile(n, tile[1])
    if w_row_ranges is not None:
        assert all(r % tn == 0 and f % 8 == 0 for f, r in w_row_ranges)

        def w_rows(i, j):
            start, first_blk = jnp.int32(0), 0
            for f, r in w_row_ranges:
                start = jnp.where(j >= first_blk, f + (j - first_blk) * tn, start)
                first_blk += r // tn
            return pl.multiple_of(start, 8), 0

        w_spec = pl.BlockSpec((pl.Element(tn), pl.Element(k)), w_rows)
    elif w_rows_are_outputs:
        w_spec = pl.BlockSpec((tn, k), lambda i, j: (j, 0))
    else:
        w_spec = pl.BlockSpec((k, tn), lambda i, j: (0, j))
    in_specs = [pl.BlockSpec(memory_space=pl.ANY), w_spec]
    args = [a, w]
    if residual is not None:
        in_specs.append(pl.BlockSpec((tm, tn), lambda i, j: (i, j)))
        args.append(residual)
    stream = dict(tm=tm, n_row_tiles=m // tm, n_slabs=_n_slabs(tm, n // tn))
    return pl.pallas_call(
        functools.partial(_mm_kernel, act=act, has_res=residual is not None,
                          w_rows_are_outputs=w_rows_are_outputs, stream=stream),
        out_shape=jax.ShapeDtypeStruct((m, n), out_dtype),
        grid=(m // tm, n // tn),
        in_specs=in_specs,
        out_specs=pl.BlockSpec((tm, tn), lambda i, j: (i, j)),
        scratch_shapes=_row_tile_scratch(tm, k, a.dtype),
        compiler_params=_cparams(("arbitrary", "arbitrary")),
        name=name,
    )(*args)


def _gdn_kernel(q_ref, k_ref, v_ref, z_ref, sm_ref, cwq_ref, cwk_ref, cwv_ref, gn_ref, o_ref,
                state_ref, tail_ref, ext_ref, qs, ks, vs, *, ts, nc, hb, nb):
    hg = pl.program_id(0)
    si = pl.program_id(1)
    c_sz = GDN_CHUNK

    @pl.when(si == 0)
    def _():
        state_ref[...] = jnp.zeros_like(state_ref)
        tail_ref[...] = jnp.zeros_like(tail_ref)

    def conv_silu(x_ref, w_ref, bi, idx, c):
        r0 = c * c_sz
        x = x_ref[bi, r0:r0 + c_sz, :]
        w = w_ref[...]
        y = x * w[0:1, :]
        for i in range(1, GDN_CONV):
            y = pltpu.roll(y, 1, 0) + x * w[i:i + 1, :]
        sl = SUBLANES
        ext_ref[idx, 0:sl, :] = tail_ref[bi, idx] if c == 0 else x_ref[bi, r0 - sl:r0, :]
        ext_ref[idx, sl:2 * sl, :] = x[0:sl, :]
        first = sl - (GDN_CONV - 1)
        head = ext_ref[idx, first:first + sl, :] * w[0:1, :]
        for i in range(1, GDN_CONV):
            head = head + ext_ref[idx, first + i:first + i + sl, :] * w[i:i + 1, :]
        y = jnp.concatenate([head, y[sl:, :]], axis=0)
        if c == nc - 1:
            tail_ref[bi, idx] = x[c_sz - sl:c_sz, :]
        return y * jax.nn.sigmoid(y)

    def prepare(c):
        rows = slice(c * c_sz, (c + 1) * c_sz)
        for bi in range(nb):
            q = conv_silu(q_ref, cwq_ref, bi, 0, c)
            k = conv_silu(k_ref, cwk_ref, bi, 1, c)
            vs[bi, rows, :] = conv_silu(v_ref, cwv_ref, bi, 2, c)
            for hh in range(hb):
                cs = slice(hh * LANES, (hh + 1) * LANES)
                qh = q[:, cs]
                kh = k[:, cs]
                qs[bi, rows, cs] = qh * (lax.rsqrt(jnp.sum(qh * qh, axis=-1, keepdims=True) + EPS)
                                         * (GDN_DK ** -0.5))
                ks[bi, rows, cs] = kh * lax.rsqrt(jnp.sum(kh * kh, axis=-1, keepdims=True) + EPS)

    ri = lax.broadcasted_iota(jnp.int32, (c_sz, c_sz), 0)
    ci = lax.broadcasted_iota(jnp.int32, (c_sz, c_sz), 1)
    strict = ri > ci
    incl = ri >= ci
    eye = jnp.where(ri == ci, 1.0, 0.0)
    n_lvl = c_sz.bit_length() - 1
    lvl_masks = []
    for l in range(n_lvl):
        same = (ri >> (l + 1)) == (ci >> (l + 1))
        lvl_masks.append(jnp.where(
            same, jnp.where(((ri >> l) & 1) == 1, jnp.where(((ci >> l) & 1) == 0, 1.0, 0.0), 0.0), 0.0))

    def advance(c):
        rows = slice(c * c_sz, (c + 1) * c_sz)
        chains = [(bi, hh) for bi in range(nb) for hh in range(hb)]
        mlows, rhss, lhs2s, qds, cds = [], [], [], [], []
        for bi, hh in chains:
            cs = slice(hh * LANES, (hh + 1) * LANES)
            hd = hg * hb + hh
            smc = sm_ref[bi, rows, :]
            beta = _lane_col(smc, hd)
            gam = _lane_col(smc, hd + GDN_HEADS)
            kc = ks[bi, rows, cs]
            qc = qs[bi, rows, cs]
            egam = jnp.exp(gam)
            g_last = gam[c_sz - 1:c_sz, :]
            gcol = jnp.broadcast_to(gam, (c_sz, c_sz))
            diff = gcol - gcol.T
            e = jnp.exp(jnp.where(incl, diff, 0.0))
            mlows.append(beta * _dot_nt(kc, kc) * jnp.where(strict, e, 0.0))
            qk = _dot_nt(qc, kc) * jnp.where(incl, e, 0.0)
            kd_t = (kc * jnp.exp(g_last - gam)).T
            lhs2s.append(jnp.concatenate([qk, kd_t], axis=0).astype(BF16))
            rhss.append(jnp.concatenate([kc * (beta * egam), vs[bi, rows, cs] * beta], axis=1).astype(BF16))
            qds.append(qc * egam)
            cds.append(jnp.exp(g_last))
        xs = [eye - lvl_masks[0] * m for m in mlows]
        mlows_bf = [m.astype(BF16) for m in mlows]
        for l in range(1, n_lvl):
            mask_bf = lvl_masks[l].astype(BF16)
            tl = [_dot(mask_bf * m, x) for m, x in zip(mlows_bf, xs)]
            xs = [x - _dot(x, t) for x, t in zip(xs, tl)]
        wus = [_dot(x, r) for x, r in zip(xs, rhss)]
        for n, (bi, hh) in enumerate(chains):
            cs = slice(hh * LANES, (hh + 1) * LANES)
            state = state_ref[n]
            wu = wus[n]
            a = _dot(jnp.concatenate([wu[:, :GDN_DK], qds[n]], axis=0), state)
            u = wu[:, GDN_DK:] - a[:c_sz]
            b2 = jnp.dot(lhs2s[n], u.astype(BF16), preferred_element_type=F32)
            o = a[c_sz:] + b2[:c_sz]
            state_ref[n] = state * cds[n] + b2[c_sz:]
            z = z_ref[bi, rows, cs]
            on = o * lax.rsqrt(jnp.mean(o * o, axis=-1, keepdims=True) + EPS) * gn_ref[...]
            o_ref[bi, rows, cs] = (on * (z * jax.nn.sigmoid(z))).astype(o_ref.dtype)

    for c in range(nc):
        prepare(c)
        advance(c)


def _gdn(qkvz, sm, conv_w, gn):
    b, s, _ = qkvz.shape
    nh = GDN_HEADS
    hb = GDN_HEADS_PER_STEP
    ng = nh // hb
    wb = hb * LANES
    ts = _tile(s, GDN_ROWS_PER_STEP)
    nc = ts // GDN_CHUNK
    blk = lambda off: pl.BlockSpec((b, ts, wb), lambda h, j: (0, j, off + h))
    cw = lambda off: pl.BlockSpec((GDN_CONV, wb), lambda h, j: (0, off + h))
    return pl.pallas_call(
        functools.partial(_gdn_kernel, ts=ts, nc=nc, hb=hb, nb=b),
        out_shape=jax.ShapeDtypeStruct((b, s, nh * GDN_DV), BF16),
        grid=(ng, s // ts),
        in_specs=[
            blk(0), blk(ng), blk(2 * ng), blk(3 * ng),
            pl.BlockSpec((b, ts, LANES), lambda h, j: (0, j, 0)),
            cw(0), cw(ng), cw(2 * ng),
            pl.BlockSpec((1, GDN_DV), lambda h, j: (0, 0)),
        ],
        out_specs=pl.BlockSpec((b, ts, wb), lambda h, j: (0, j, h)),
        scratch_shapes=[
            pltpu.VMEM((b * hb, GDN_DK, GDN_DV), F32),
            pltpu.VMEM((b, 3, SUBLANES, wb), F32),
            pltpu.VMEM((3, 2 * SUBLANES, wb), F32),
            pltpu.VMEM((b, ts, wb), F32),
            pltpu.VMEM((b, ts, wb), F32),
            pltpu.VMEM((b, ts, wb), F32),
        ],
        compiler_params=_cparams(("arbitrary", "arbitrary")),
        name="gdn",
    )(qkvz, qkvz, qkvz, qkvz, sm, conv_w, conv_w, conv_w, gn)


def _fox_kernel(q_ref, k_ref, v_ref, smq_ref, smk_ref, gq_ref, gk_ref, o_ref,
                kaug_ref, vaug_ref, m_s, acc_s, sa_ref, sb_ref, *, tq, tk, nk):
    hd = pl.program_id(1)
    qi = pl.program_id(2)
    n_sub = tq // tk
    n_lt = tk // LANES

    src = lax.broadcasted_iota(jnp.int32, (LANES, LANES), 0)
    dst = lax.broadcasted_iota(jnp.int32, (LANES, LANES), 1)
    first_piece = 2 * GDN_HEADS + hd

    def bias_cols(sm_block, dst0, sign, ones0):
        pick = jnp.where((dst >= dst0) & (dst < dst0 + 3) & (src == first_piece + FOX_HEADS * (dst - dst0)),
                         sign, 0.0).astype(BF16)
        lane = lax.broadcasted_iota(jnp.int32, sm_block.shape, 1)
        ones = jnp.where((lane >= ones0) & (lane < ones0 + 3), 1.0, 0.0)
        return (jnp.dot(sm_block.astype(BF16), pick, preferred_element_type=F32) + ones).astype(BF16)

    @pl.when(qi == 0)
    def _():
        def build(j, carry):
            rows = pl.ds(pl.multiple_of(j * tk, tk), tk)
            kk = k_ref[0, rows, :].astype(F32)
            kn = kk * lax.rsqrt(jnp.mean(kk * kk, axis=-1, keepdims=True) + EPS) * gk_ref[...]
            kaug_ref[rows, 0:FOX_DH] = kn.astype(BF16)
            kaug_ref[rows, FOX_DH:2 * FOX_DH] = bias_cols(smk_ref[0, rows, :], 3, -1.0, 0)
            vaug_ref[rows, 0:FOX_DH] = v_ref[0, rows, :]
            vaug_ref[rows, FOX_DH:2 * FOX_DH] = jnp.ones((tk, FOX_DH), BF16)
            return carry
        lax.fori_loop(0, nk, build, 0)

    qq = q_ref[0].astype(F32)
    qn = qq * (lax.rsqrt(jnp.mean(qq * qq, axis=-1, keepdims=True) + EPS) * (FOX_DH ** -0.5 * LOG2E)) * gq_ref[...]
    q_aug = jnp.concatenate([qn.astype(BF16), bias_cols(smq_ref[0], 0, 1.0, 3)], axis=1)

    m_s[...] = jnp.full_like(m_s, NEG_BIG)
    acc_s[...] = jnp.zeros_like(acc_s)

    def scores(s_ref, j):
        rows = pl.ds(pl.multiple_of(j * tk, tk), tk)
        s_ref[...] = lax.dot_general(q_aug, kaug_ref[rows, :], (((1,), (1,)), ((), ())),
                                     preferred_element_type=F32)

    def accumulate(s_ref, j, diag):
        rows = pl.ds(pl.multiple_of(j * tk, tk), tk)
        v_blk = vaug_ref[rows, :]
        groups = ([(0, tq, False)] if diag is None
                  else [(r * tk, (r + 1) * tk, r == diag) for r in range(diag, n_sub)])
        for lo_row, hi_row, masked in groups:
            rs = slice(lo_row, hi_row)
            tiles = [s_ref[rs, c * LANES:(c + 1) * LANES] for c in range(n_lt)]
            if masked:
                ri = lax.broadcasted_iota(jnp.int32, (tk, LANES), 0)
                ci = lax.broadcasted_iota(jnp.int32, (tk, LANES), 1)
                tiles = [jnp.where(ri >= ci + c * LANES, t, NEG_BIG) for c, t in enumerate(tiles)]
            mx = tiles[0]
            for t in tiles[1:]:
                mx = jnp.maximum(mx, t)
            m_prev = m_s[rs, :]
            m_next = jnp.maximum(m_prev, jnp.max(mx, axis=1, keepdims=True))
            p = jnp.concatenate([jnp.exp2(t - m_next) for t in tiles], axis=1).astype(BF16)
            alpha = jnp.exp2(m_prev - m_next)
            acc_s[rs, :] = (jnp.concatenate([alpha, alpha], axis=1) * acc_s[rs, :]
                            + jnp.dot(p, v_blk, preferred_element_type=F32))
            m_s[rs, :] = m_next

    scores(sa_ref, 0)

    def body(i, carry):
        j = 2 * i
        scores(sb_ref, j + 1)
        accumulate(sa_ref, j, None)
        scores(sa_ref, j + 2)
        accumulate(sb_ref, j + 1, None)
        return carry

    n_below = n_sub * qi
    lax.fori_loop(0, n_below // 2, body, 0)
    bufs = (sa_ref, sb_ref)
    for e in range(n_sub):
        if e + 1 < n_sub:
            scores(bufs[(e + 1) % 2], n_below + e + 1)
        accumulate(bufs[e % 2], n_below + e, e)

    acc = acc_s[...]
    o_ref[0] = (acc[:, :FOX_DH] / acc[:, FOX_DH:]).astype(o_ref.dtype)


def _fox(qkv, sm, gq, gk):
    b, s, _ = qkv.shape
    nh = FOX_HEADS
    tq = _tile(s, FOX_TQ)
    tk = _tile(tq, FOX_TK)
    assert (tq // tk) % 2 == 0, "the score pipeline consumes key blocks in pairs"
    nk = s // tk
    return pl.pallas_call(
        functools.partial(_fox_kernel, tq=tq, tk=tk, nk=nk),
        out_shape=jax.ShapeDtypeStruct((b, s, nh * FOX_DH), BF16),
        grid=(b, nh, s // tq),
        in_specs=[
            pl.BlockSpec((1, tq, FOX_DH), lambda i, h, j: (i, j, h)),
            pl.BlockSpec((1, s, FOX_DH), lambda i, h, j: (i, 0, nh + h)),
            pl.BlockSpec((1, s, FOX_DH), lambda i, h, j: (i, 0, 2 * nh + h)),
            pl.BlockSpec((1, tq, LANES), lambda i, h, j: (i, j, 0)),
            pl.BlockSpec((1, s, LANES), lambda i, h, j: (i, 0, 0)),
            pl.BlockSpec((1, FOX_DH), lambda i, h, j: (0, 0)),
            pl.BlockSpec((1, FOX_DH), lambda i, h, j: (0, 0)),
        ],
        out_specs=pl.BlockSpec((1, tq, FOX_DH), lambda i, h, j: (i, j, h)),
        scratch_shapes=[
            pltpu.VMEM((s, 2 * FOX_DH), BF16),
            pltpu.VMEM((s, 2 * FOX_DH), BF16),
            pltpu.VMEM((tq, LANES), F32),
            pltpu.VMEM((tq, 2 * FOX_DH), F32),
            pltpu.VMEM((tq, tk), F32),
            pltpu.VMEM((tq, tk), F32),
        ],
        compiler_params=_cparams(("arbitrary", "arbitrary", "arbitrary")),
        name="fox",
    )(qkv, qkv, qkv, sm, sm, gq, gk)


def _memkv_kernel(mem_ref, g_ref, w_ref, gk_ref, o_ref, *, n_k_tiles):
    j = pl.program_id(1)
    m = mem_ref[0]
    hn = (m * lax.rsqrt(jnp.mean(m * m, axis=-1, keepdims=True) + EPS) * g_ref[...]).astype(BF16)
    r = jnp.dot(hn, w_ref[...].astype(BF16), preferred_element_type=F32)
    parts = []
    for t in range(r.shape[1] // MEM_DH):
        rt = r[:, t * MEM_DH:(t + 1) * MEM_DH]
        parts.append(rt * lax.rsqrt(jnp.mean(rt * rt, axis=-1, keepdims=True) + EPS) * gk_ref[...])
    normed = jnp.concatenate(parts, axis=1)
    is_k = jnp.where(j < n_k_tiles, 1.0, 0.0)
    o_ref[0] = (is_k * normed + (1.0 - is_k) * r).astype(o_ref.dtype)


def _memkv(mem, g, w, gk):
    b, ml, d = mem.shape
    n = w.shape[1]
    tn = 2 * MEM_DH
    return pl.pallas_call(
        functools.partial(_memkv_kernel, n_k_tiles=(n // 2) // tn),
        out_shape=jax.ShapeDtypeStruct((b, ml, n), BF16),
        grid=(b, n // tn),
        in_specs=[
            pl.BlockSpec((1, ml, d), lambda i, j: (i, 0, 0)),
            pl.BlockSpec((1, d), lambda i, j: (0, 0)),
            pl.BlockSpec((d, tn), lambda i, j: (0, j)),
            pl.BlockSpec((1, MEM_DH), lambda i, j: (0, 0)),
        ],
        out_specs=pl.BlockSpec((1, ml, tn), lambda i, j: (i, 0, j)),
        compiler_params=_cparams(("arbitrary", "arbitrary")),
        name="memkv",
    )(mem, g, w, gk)


def _mem_kernel(q_ref, k_ref, v_ref, gq_ref, o_ref):
    for hd in range(MEM_HEADS):
        cs = slice(hd * MEM_DH, (hd + 1) * MEM_DH)
        qq = q_ref[0, :, cs].astype(F32)
        qn = qq * lax.rsqrt(jnp.mean(qq * qq, axis=-1, keepdims=True) + EPS) * gq_ref[...] * (MEM_DH ** -0.5)
        s = _dot_nt(qn, k_ref[0, :, cs])
        p = jnp.exp(s - jnp.max(s, axis=1, keepdims=True))
        p = p / jnp.sum(p, axis=1, keepdims=True)
        o_ref[0, :, cs] = jnp.dot(p.astype(BF16), v_ref[0, :, cs], preferred_element_type=F32).astype(o_ref.dtype)


def _mem_attn(qsrc, q_col_block, kv, gq):
    b, s, _ = qsrc.shape
    ml = kv.shape[1]
    wq = MEM_HEADS * MEM_DH
    tq = _tile(s, ROW_TILE)
    return pl.pallas_call(
        _mem_kernel,
        out_shape=jax.ShapeDtypeStruct((b, s, wq), BF16),
        grid=(b, s // tq),
        in_specs=[
            pl.BlockSpec((1, tq, wq), lambda i, j: (i, j, q_col_block)),
            pl.BlockSpec((1, ml, wq), lambda i, j: (i, 0, 0)),
            pl.BlockSpec((1, ml, wq), lambda i, j: (i, 0, 1)),
            pl.BlockSpec((1, MEM_DH), lambda i, j: (0, 0)),
        ],
        out_specs=pl.BlockSpec((1, tq, wq), lambda i, j: (i, j, 0)),
        compiler_params=_cparams(("arbitrary", "arbitrary")),
        name="mem_attn",
    )(qsrc, kv, kv, gq)


def _merge_kernel(oa_ref, ob_ref, om_ref, wa_ref, wb_ref, wm_ref, ga_ref, gb_ref, gm_ref, y_ref):
    d = lambda o_ref, w_ref: jnp.dot(o_ref[...], w_ref[...].astype(BF16), preferred_element_type=F32)
    y = (ga_ref[...].astype(F32) * d(oa_ref, wa_ref)
         + gb_ref[...].astype(F32) * d(ob_ref, wb_ref)
         + gm_ref[...].astype(F32) * d(om_ref, wm_ref))
    y_ref[...] = y.astype(y_ref.dtype)


def _merge(oa, ob, om, wa, wb, wm, gates):
    t, ka = oa.shape
    dm = wa.shape[1]
    tm = _tile(t, MERGE_TILE[0])
    tn = _tile(dm, MERGE_TILE[1])
    nb = dm // tn
    a_spec = lambda kk: pl.BlockSpec((tm, kk), lambda i, j: (i, 0))
    w_spec = lambda kk: pl.BlockSpec((kk, tn), lambda i, j: (0, j))
    g_spec = lambda off: pl.BlockSpec((tm, tn), lambda i, j: (i, off * nb + j))
    return pl.pallas_call(
        _merge_kernel,
        out_shape=jax.ShapeDtypeStruct((t, dm), BF16),
        grid=(t // tm, nb),
        in_specs=[a_spec(ka), a_spec(ob.shape[1]), a_spec(om.shape[1]),
                  w_spec(ka), w_spec(ob.shape[1]), w_spec(om.shape[1]),
                  g_spec(0), g_spec(1), g_spec(2)],
        out_specs=pl.BlockSpec((tm, tn), lambda i, j: (i, j)),
        compiler_params=_cparams(("arbitrary", "arbitrary")),
        name="merge",
    )(oa, ob, om, wa, wb, wm, gates, gates, gates)


def _layer(x, mem, g_mix, w_in, conv_w, a_log, dt_bias, gdn_norm_g, fox_b_f, fox_q_norm, fox_k_norm,
           g_mem, w_mem_kv, mem_q_norm, mem_k_norm, w_up_gdn, w_up_fox, w_up_mem, w_out, g_mlp, w_ff1, w_ff2):
    b, s, d = x.shape
    t = b * s
    gdn_qk = GDN_HEADS * GDN_DK
    gdn_v = GDN_HEADS * GDN_DV
    fox_w = FOX_HEADS * FOX_DH
    mem_w = MEM_HEADS * MEM_DH
    o_z = 2 * gdn_qk + gdn_v
    o_beta = o_z + gdn_v
    o_dec = o_beta + GDN_HEADS
    o_fq = o_dec + GDN_HEADS
    o_ff = o_fq + 3 * fox_w
    o_mq = o_ff + FOX_HEADS
    o_gate = o_mq + mem_w

    wt = w_in.T
    n_small = 2 * GDN_HEADS + FOX_HEADS
    w_small = jnp.concatenate([wt[o_beta:o_fq], wt[o_ff:o_mq],
                               jnp.zeros((LANES - n_small, d), F32)], axis=0)
    zpad = jnp.zeros((LANES - n_small,), F32)
    bias = jnp.concatenate([jnp.zeros((GDN_HEADS,), F32), dt_bias.astype(F32), fox_b_f.astype(F32), zpad])[None, :]
    alog = jnp.concatenate([jnp.zeros((GDN_HEADS,), F32), a_log.astype(F32), jnp.zeros((FOX_HEADS,), F32), zpad])[None, :]

    h, sm = _norm_small(x, g_mix[None, :], w_small, bias, alog)
    h2d = h.reshape(t, d)

    proj = functools.partial(_matmul, h2d, wt, tile=PROJ_TILE, w_rows_are_outputs=True)
    qkvz = proj(out_dtype=F32, w_row_ranges=[(0, o_beta)], name="proj_gdn").reshape(b, s, -1)
    att = proj(out_dtype=BF16, tile=PROJ_BF16_TILE, w_row_ranges=[(o_fq, 3 * fox_w), (o_mq, mem_w)],
               name="proj_att").reshape(b, s, -1)
    gates = proj(out_dtype=BF16, tile=PROJ_BF16_TILE, act="sigmoid", w_row_ranges=[(o_gate, N_BRANCH * d)],
                 name="proj_gate")

    o_a = _gdn(qkvz, sm, conv_w, gdn_norm_g[None, :])
    o_b = _fox(att, sm, fox_q_norm[None, :], fox_k_norm[None, :])
    kv_m = _memkv(mem, g_mem[None, :], w_mem_kv, mem_k_norm[None, :])
    o_m = _mem_attn(att, (3 * fox_w) // mem_w, kv_m, mem_q_norm[None, :])

    y = _merge(o_a.reshape(t, -1), o_b.reshape(t, -1), o_m.reshape(t, -1), w_up_gdn, w_up_fox, w_up_mem, gates)
    x1, h2 = _out_norm(y, w_out.astype(BF16), x.reshape(t, d), g_mlp[None, :])
    u = _matmul(h2, w_ff1, out_dtype=BF16, tile=FF1_TILE, act="relu2", name="ff1")
    out = _matmul(u, w_ff2.astype(BF16), out_dtype=F32, tile=FF2_TILE, residual=x1, name="ff2")
    return out.reshape(b, s, d)


def kernel(x, mem, g_mix, w_in, conv_w, a_log, dt_bias, gdn_norm_g, fox_b_f, fox_q_norm, fox_k_norm, g_mem, w_mem_kv, mem_q_norm, mem_k_norm, w_up_gdn, w_up_fox, w_up_mem, w_out, g_mlp, w_ff1, w_ff2):
    depth = w_in.shape[0]
    for l in range(depth):
        x = _layer(x, mem, g_mix[l], w_in[l], conv_w[l], a_log[l], dt_bias[l], gdn_norm_g[l], fox_b_f[l],
                   fox_q_norm[l], fox_k_norm[l], g_mem[l], w_mem_kv[l], mem_q_norm[l], mem_k_norm[l],
                   w_up_gdn[l], w_up_fox[l], w_up_mem[l], w_out[l], g_mlp[l], w_ff1[l], w_ff2[l])
    return x
```

```python
import functools
import math

import jax
import jax.numpy as jnp
from jax import lax
from jax.experimental import pallas as pl
from jax.experimental.pallas import tpu as pltpu

F32 = jnp.float32
BF16 = jnp.bfloat16
EPS = 1e-6

GDN_HEADS = 8
GDN_DK = 128
GDN_DV = 128
GDN_CONV = 4
GDN_CHUNK = 128
GDN_HEADS_PER_STEP = 8
GDN_ROWS_PER_STEP = 256
FOX_HEADS = 8
FOX_DH = 128
FOX_TQ = 1024
FOX_TK = 512
MEM_HEADS = 4
MEM_DH = 256
N_BRANCH = 3
LANES = 128
SUBLANES = 8
NEG_BIG = -1e30
LOG2E = math.log2(math.e)

V7X_VMEM_BYTES = 64 * 1024 * 1024
VMEM_LIMIT = V7X_VMEM_BYTES - 8 * 1024 * 1024

PROJ_TILE = (2048, 512)
PROJ_BF16_TILE = (2048, 1024)
FF1_TILE = (2048, 1024)
FF2_TILE = (1024, 256)
MERGE_TILE = (2048, 256)
ROW_TILE = 512
LHS_SLABS = 8


def _tile(n, pref):
    return pref if n % pref == 0 else n


def _cparams(sem):
    return pltpu.CompilerParams(dimension_semantics=sem, vmem_limit_bytes=VMEM_LIMIT)


def _dot(a, b):
    return jnp.dot(a.astype(BF16), b.astype(BF16), preferred_element_type=F32)


def _dot_nt(a, b):
    return lax.dot_general(a.astype(BF16), b.astype(BF16), (((1,), (1,)), ((), ())),
                           preferred_element_type=F32)


def _split2(a):
    hi = a.astype(BF16)
    lo = (a - hi.astype(F32)).astype(BF16)
    return hi, lo


def _split3(a):
    hi = a.astype(BF16).astype(F32)
    r = a - hi
    mid = r.astype(BF16).astype(F32)
    lo = (r - mid).astype(BF16).astype(F32)
    return hi, mid, lo


def _dot_exact_lhs(l_bf16, v):
    hi, mid, lo = _split3(v)
    n = v.shape[1]
    d = functools.partial(jnp.dot, preferred_element_type=F32)
    both = d(l_bf16, jnp.concatenate([hi, mid], axis=1).astype(BF16))
    return both[:, :n] + both[:, n:] + d(l_bf16, lo.astype(BF16))


def _lane_col(a, idx):
    lane = lax.broadcasted_iota(jnp.int32, a.shape, 1)
    return jnp.sum(jnp.where(lane == idx, a, 0.0), axis=1, keepdims=True)


def _softplus(x):
    return jnp.maximum(x, 0.0) + jnp.log1p(jnp.exp(-jnp.abs(x)))


def _norm_small_kernel(x_ref, g_ref, ws_ref, bias_ref, alog_ref, h_ref, sm_ref,
                       carry_ref, wsplit_ref, lfull_ref, lblk_ref, *, tm):
    s = pl.program_id(1)

    @pl.when(s == 0)
    def _():
        carry_ref[...] = jnp.zeros_like(carry_ref)

    @pl.when((pl.program_id(0) == 0) & (s == 0))
    def _():
        w_hi, w_lo = _split2(ws_ref[...])
        wsplit_ref[0:LANES, :] = w_hi
        wsplit_ref[LANES:2 * LANES, :] = w_lo
        row = lax.broadcasted_iota(jnp.int32, (tm, tm), 0)
        col = lax.broadcasted_iota(jnp.int32, (tm, tm), 1)
        low = col <= row
        sh = GDN_CHUNK.bit_length() - 1
        lfull_ref[...] = jnp.where(low, 1.0, 0.0).astype(BF16)
        lblk_ref[...] = jnp.where(low, jnp.where((row >> sh) == (col >> sh), 1.0, 0.0), 0.0).astype(BF16)

    x = x_ref[0]
    h = x * lax.rsqrt(jnp.mean(x * x, axis=-1, keepdims=True) + EPS) * g_ref[...]
    h_hi, h_lo = _split2(h)
    h_ref[0] = h_hi

    nt = lambda p, q: lax.dot_general(p, q, (((1,), (1,)), ((), ())), preferred_element_type=F32)
    both = nt(h_hi, wsplit_ref[...])
    pre = both[:, :LANES] + both[:, LANES:] + nt(h_lo, wsplit_ref[0:LANES, :]) + bias_ref[...]
    lane = lax.broadcasted_iota(jnp.int32, pre.shape, 1)
    nh = GDN_HEADS
    beta = jax.nn.sigmoid(pre)
    gdec = -jnp.exp(alog_ref[...]) * _softplus(pre)
    logf = -_softplus(-pre)
    vals = jnp.where(lane < nh, beta, jnp.where(lane < 2 * nh, gdec, jnp.where(lane < 3 * nh, logf, 0.0)))

    cs_blk = _dot_exact_lhs(lblk_ref[...], vals)
    cs_full = _dot_exact_lhs(lfull_ref[...], vals) + carry_ref[0:1, :]
    hi, mid, lo = _split3(cs_full * LOG2E)
    pieces = jnp.where(lane < 3 * nh, hi, jnp.where(lane < 4 * nh, pltpu.roll(mid, nh, 1), pltpu.roll(lo, 2 * nh, 1)))
    sm_ref[0] = jnp.where(lane < nh, vals, jnp.where(lane < 2 * nh, cs_blk, jnp.where(lane < 5 * nh, pieces, 0.0)))
    carry_ref[...] = jnp.broadcast_to(cs_full[tm - 1:tm, :], carry_ref.shape)


def _norm_small(x, g, w_small, bias, alog):
    b, s, d = x.shape
    tm = _tile(s, ROW_TILE)
    return pl.pallas_call(
        functools.partial(_norm_small_kernel, tm=tm),
        out_shape=(jax.ShapeDtypeStruct((b, s, d), BF16), jax.ShapeDtypeStruct((b, s, LANES), F32)),
        grid=(b, s // tm),
        in_specs=[
            pl.BlockSpec((1, tm, d), lambda i, j: (i, j, 0)),
            pl.BlockSpec((1, d), lambda i, j: (0, 0)),
            pl.BlockSpec((LANES, d), lambda i, j: (0, 0)),
            pl.BlockSpec((1, LANES), lambda i, j: (0, 0)),
            pl.BlockSpec((1, LANES), lambda i, j: (0, 0)),
        ],
        out_specs=(
            pl.BlockSpec((1, tm, d), lambda i, j: (i, j, 0)),
            pl.BlockSpec((1, tm, LANES), lambda i, j: (i, j, 0)),
        ),
        scratch_shapes=[
            pltpu.VMEM((SUBLANES, LANES), F32),
            pltpu.VMEM((2 * LANES, d), BF16),
            pltpu.VMEM((tm, tm), BF16),
            pltpu.VMEM((tm, tm), BF16),
        ],
        compiler_params=_cparams(("arbitrary", "arbitrary")),
        name="norm_small",
    )(x, g, w_small, bias, alog)


def _out_norm_kernel(y_ref, w_ref, x_ref, g_ref, x1_ref, h_ref):
    x1 = x_ref[...] + jnp.dot(y_ref[...], w_ref[...], preferred_element_type=F32)
    x1_ref[...] = x1
    h_ref[...] = (x1 * lax.rsqrt(jnp.mean(x1 * x1, axis=-1, keepdims=True) + EPS) * g_ref[...]).astype(h_ref.dtype)


def _out_norm(y, w_bf16, x2d, g):
    t, d = x2d.shape
    k = y.shape[1]
    tm = _tile(t, ROW_TILE)
    row = lambda width: pl.BlockSpec((tm, width), lambda i: (i, 0))
    return pl.pallas_call(
        _out_norm_kernel,
        out_shape=(jax.ShapeDtypeStruct((t, d), F32), jax.ShapeDtypeStruct((t, d), BF16)),
        grid=(t // tm,),
        in_specs=[row(k), pl.BlockSpec((k, d), lambda i: (0, 0), pipeline_mode=pl.Buffered(1)), row(d),
                  pl.BlockSpec((1, d), lambda i: (0, 0))],
        out_specs=(row(d), row(d)),
        compiler_params=_cparams(("arbitrary",)),
        name="out_norm",
    )(y, w_bf16, x2d, g)


def _stream_row_tile(a_hbm, a_buf, sem, *, tm, n_row_tiles, n_slabs):
    i = pl.program_id(0)
    j = pl.program_id(1)
    slot = i % 2
    slab = tm // n_slabs

    def slab_copy(row_tile, part, to_slot):
        rows = pl.ds(row_tile * tm + part * slab, slab)
        return pltpu.make_async_copy(a_hbm.at[rows, :], a_buf.at[to_slot, pl.ds(part * slab, slab), :],
                                     sem.at[to_slot])

    @pl.when((i == 0) & (j == 0))
    def _():
        for part in range(n_slabs):
            slab_copy(0, part, 0).start()

    @pl.when(j == 0)
    def _():
        for part in range(n_slabs):
            slab_copy(i, part, slot).wait()

    @pl.when((i + 1 < n_row_tiles) & (j < n_slabs))
    def _():
        slab_copy(i + 1, j, 1 - slot).start()

    return slot


def _n_slabs(tm, n_col_steps):
    for n in range(min(LHS_SLABS, n_col_steps), 0, -1):
        if tm % n == 0 and (tm // n) % (2 * SUBLANES) == 0:
            return n
    raise ValueError("row tile cannot be split into slabs")


def _row_tile_scratch(tm, k, dtype):
    return [pltpu.VMEM((2, tm, k), dtype), pltpu.SemaphoreType.DMA((2,))]


def _mm_kernel(a_hbm, w_ref, *rest, act, has_res, w_rows_are_outputs, stream):
    if has_res:
        r_ref, o_ref, a_buf, sem = rest
    else:
        o_ref, a_buf, sem = rest
    slot = _stream_row_tile(a_hbm, a_buf, sem, **stream)
    contract = (((1,), (1,)), ((), ())) if w_rows_are_outputs else (((1,), (0,)), ((), ()))
    acc = lax.dot_general(a_buf[slot], w_ref[...].astype(BF16), contract, preferred_element_type=F32)
    if act == "sigmoid":
        acc = 0.5 * jnp.tanh(0.5 * acc) + 0.5
    elif act == "relu2":
        r = jnp.maximum(acc, 0.0)
        acc = r * r
    if has_res:
        acc = acc + r_ref[...]
    o_ref[...] = acc.astype(o_ref.dtype)


def _matmul(a, w, *, out_dtype, tile, act=None, residual=None, w_rows_are_outputs=False,
            w_row_ranges=None, name="matmul"):
    m, k = a.shape
    if w_row_ranges is not None:
        assert w_rows_are_outputs
        n = sum(r for _, r in w_row_ranges)
    else:
        n = w.shape[0] if w_rows_are_outputs else w.shape[1]
    tm = _tile(m, tile[0])
    tn = _tile(n, tile[1])
    if w_row_ranges is not None:
        assert all(r % tn == 0 and f % 8 == 0 for f, r in w_row_ranges)

        def w_rows(i, j):
            start, first_blk = jnp.int32(0), 0
            for f, r in w_row_ranges:
                start = jnp.where(j >= first_blk, f + (j - first_blk) * tn, start)
                first_blk += r // tn
            return pl.multiple_of(start, 8), 0

        w_spec = pl.BlockSpec((pl.Element(tn), pl.Element(k)), w_rows)
    elif w_rows_are_outputs:
        w_spec = pl.BlockSpec((tn, k), lambda i, j: (j, 0))
    else:
        w_spec = pl.BlockSpec((k, tn), lambda i, j: (0, j))
    in_specs = [pl.BlockSpec(memory_space=pl.ANY), w_spec]
    args = [a, w]
    if residual is not None:
        in_specs.append(pl.BlockSpec((tm, tn), lambda i, j: (i, j)))
        args.append(residual)
    stream = dict(tm=tm, n_row_tiles=m // tm, n_slabs=_n_slabs(tm, n // tn))
    return pl.pallas_call(
        functools.partial(_mm_kernel, act=act, has_res=residual is not None,
                          w_rows_are_outputs=w_rows_are_outputs, stream=stream),
        out_shape=jax.ShapeDtypeStruct((m, n), out_dtype),
        grid=(m // tm, n // tn),
        in_specs=in_specs,
        out_specs=pl.BlockSpec((tm, tn), lambda i, j: (i, j)),
        scratch_shapes=_row_tile_scratch(tm, k, a.dtype),
        compiler_params=_cparams(("arbitrary", "arbitrary")),
        name=name,
    )(*args)


def _gdn_kernel(q_ref, k_ref, v_ref, z_ref, sm_ref, cwq_ref, cwk_ref, cwv_ref, gn_ref, o_ref,
                state_ref, tail_ref, ext_ref, qs, ks, vs, *, ts, nc, hb, nb):
    hg = pl.program_id(0)
    si = pl.program_id(1)
    c_sz = GDN_CHUNK

    @pl.when(si == 0)
    def _():
        state_ref[...] = jnp.zeros_like(state_ref)
        tail_ref[...] = jnp.zeros_like(tail_ref)

    def conv_silu(x_ref, w_ref, bi, idx, c):
        r0 = c * c_sz
        x = x_ref[bi, r0:r0 + c_sz, :]
        w = w_ref[...]
        y = x * w[0:1, :]
        for i in range(1, GDN_CONV):
            y = pltpu.roll(y, 1, 0) + x * w[i:i + 1, :]
        sl = SUBLANES
        ext_ref[idx, 0:sl, :] = tail_ref[bi, idx] if c == 0 else x_ref[bi, r0 - sl:r0, :]
        ext_ref[idx, sl:2 * sl, :] = x[0:sl, :]
        first = sl - (GDN_CONV - 1)
        head = ext_ref[idx, first:first + sl, :] * w[0:1, :]
        for i in range(1, GDN_CONV):
            head = head + ext_ref[idx, first + i:first + i + sl, :] * w[i:i + 1, :]
        y = jnp.concatenate([head, y[sl:, :]], axis=0)
        if c == nc - 1:
            tail_ref[bi, idx] = x[c_sz - sl:c_sz, :]
        return y * jax.nn.sigmoid(y)

    def prepare(c):
        rows = slice(c * c_sz, (c + 1) * c_sz)
        for bi in range(nb):
            q = conv_silu(q_ref, cwq_ref, bi, 0, c)
            k = conv_silu(k_ref, cwk_ref, bi, 1, c)
            vs[bi, rows, :] = conv_silu(v_ref, cwv_ref, bi, 2, c)
            for hh in range(hb):
                cs = slice(hh * LANES, (hh + 1) * LANES)
                qh = q[:, cs]
                kh = k[:, cs]
                qs[bi, rows, cs] = qh * (lax.rsqrt(jnp.sum(qh * qh, axis=-1, keepdims=True) + EPS)
                                         * (GDN_DK ** -0.5))
                ks[bi, rows, cs] = kh * lax.rsqrt(jnp.sum(kh * kh, axis=-1, keepdims=True) + EPS)

    ri = lax.broadcasted_iota(jnp.int32, (c_sz, c_sz), 0)
    ci = lax.broadcasted_iota(jnp.int32, (c_sz, c_sz), 1)
    strict = ri > ci
    incl = ri >= ci
    eye = jnp.where(ri == ci, 1.0, 0.0)
    n_lvl = c_sz.bit_length() - 1
    lvl_masks = []
    for l in range(n_lvl):
        same = (ri >> (l + 1)) == (ci >> (l + 1))
        lvl_masks.append(jnp.where(
            same, jnp.where(((ri >> l) & 1) == 1, jnp.where(((ci >> l) & 1) == 0, 1.0, 0.0), 0.0), 0.0))

    def advance(c):
        rows = slice(c * c_sz, (c + 1) * c_sz)
        chains = [(bi, hh) for bi in range(nb) for hh in range(hb)]
        mlows, rhss, lhs2s, qds, cds = [], [], [], [], []
        for bi, hh in chains:
            cs = slice(hh * LANES, (hh + 1) * LANES)
            hd = hg * hb + hh
            smc = sm_ref[bi, rows, :]
            beta = _lane_col(smc, hd)
            gam = _lane_col(smc, hd + GDN_HEADS)
            kc = ks[bi, rows, cs]
            qc = qs[bi, rows, cs]
            egam = jnp.exp(gam)
            g_last = gam[c_sz - 1:c_sz, :]
            gcol = jnp.broadcast_to(gam, (c_sz, c_sz))
            diff = gcol - gcol.T
            e = jnp.exp(jnp.where(incl, diff, 0.0))
            mlows.append(beta * _dot_nt(kc, kc) * jnp.where(strict, e, 0.0))
            qk = _dot_nt(qc, kc) * jnp.where(incl, e, 0.0)
            kd_t = (kc * jnp.exp(g_last - gam)).T
            lhs2s.append(jnp.concatenate([qk, kd_t], axis=0).astype(BF16))
            rhss.append(jnp.concatenate([kc * (beta * egam), vs[bi, rows, cs] * beta], axis=1).astype(BF16))
            qds.append(qc * egam)
            cds.append(jnp.exp(g_last))
        xs = [eye - lvl_masks[0] * m for m in mlows]
        mlows_bf = [m.astype(BF16) for m in mlows]
        for l in range(1, n_lvl):
            mask_bf = lvl_masks[l].astype(BF16)
            tl = [_dot(mask_bf * m, x) for m, x in zip(mlows_bf, xs)]
            xs = [x - _dot(x, t) for x, t in zip(xs, tl)]
        wus = [_dot(x, r) for x, r in zip(xs, rhss)]
        for n, (bi, hh) in enumerate(chains):
            cs = slice(hh * LANES, (hh + 1) * LANES)
            state = state_ref[n]
            wu = wus[n]
            a = _dot(jnp.concatenate([wu[:, :GDN_DK], qds[n]], axis=0), state)
            u = wu[:, GDN_DK:] - a[:c_sz]
            b2 = jnp.dot(lhs2s[n], u.astype(BF16), preferred_element_type=F32)
            o = a[c_sz:] + b2[:c_sz]
            state_ref[n] = state * cds[n] + b2[c_sz:]
            z = z_ref[bi, rows, cs]
            on = o * lax.rsqrt(jnp.mean(o * o, axis=-1, keepdims=True) + EPS) * gn_ref[...]
            o_ref[bi, rows, cs] = (on * (z * jax.nn.sigmoid(z))).astype(o_ref.dtype)

    for c in range(nc):
        prepare(c)
        advance(c)


def _gdn(qkvz, sm, conv_w, gn):
    b, s, _ = qkvz.shape
    nh = GDN_HEADS
    hb = GDN_HEADS_PER_STEP
    ng = nh // hb
    wb = hb * LANES
    ts = _tile(s, GDN_ROWS_PER_STEP)
    nc = ts // GDN_CHUNK
    blk = lambda off: pl.BlockSpec((b, ts, wb), lambda h, j: (0, j, off + h))
    cw = lambda off: pl.BlockSpec((GDN_CONV, wb), lambda h, j: (0, off + h))
    return pl.pallas_call(
        functools.partial(_gdn_kernel, ts=ts, nc=nc, hb=hb, nb=b),
        out_shape=jax.ShapeDtypeStruct((b, s, nh * GDN_DV), BF16),
        grid=(ng, s // ts),
        in_specs=[
            blk(0), blk(ng), blk(2 * ng), blk(3 * ng),
            pl.BlockSpec((b, ts, LANES), lambda h, j: (0, j, 0)),
            cw(0), cw(ng), cw(2 * ng),
            pl.BlockSpec((1, GDN_DV), lambda h, j: (0, 0)),
        ],
        out_specs=pl.BlockSpec((b, ts, wb), lambda h, j: (0, j, h)),
        scratch_shapes=[
            pltpu.VMEM((b * hb, GDN_DK, GDN_DV), F32),
            pltpu.VMEM((b, 3, SUBLANES, wb), F32),
            pltpu.VMEM((3, 2 * SUBLANES, wb), F32),
            pltpu.VMEM((b, ts, wb), F32),
            pltpu.VMEM((b, ts, wb), F32),
            pltpu.VMEM((b, ts, wb), F32),
        ],
        compiler_params=_cparams(("arbitrary", "arbitrary")),
        name="gdn",
    )(qkvz, qkvz, qkvz, qkvz, sm, conv_w, conv_w, conv_w, gn)


def _fox_kernel(q_ref, k_ref, v_ref, smq_ref, smk_ref, gq_ref, gk_ref, o_ref,
                kaug_ref, vaug_ref, m_s, acc_s, sa_ref, sb_ref, *, tq, tk, nk):
    hd = pl.program_id(1)
    qi = pl.program_id(2)
    n_sub = tq // tk
    n_lt = tk // LANES

    src = lax.broadcasted_iota(jnp.int32, (LANES, LANES), 0)
    dst = lax.broadcasted_iota(jnp.int32, (LANES, LANES), 1)
    first_piece = 2 * GDN_HEADS + hd

    def bias_cols(sm_block, dst0, sign, ones0):
        pick = jnp.where((dst >= dst0) & (dst < dst0 + 3) & (src == first_piece + FOX_HEADS * (dst - dst0)),
                         sign, 0.0).astype(BF16)
        lane = lax.broadcasted_iota(jnp.int32, sm_block.shape, 1)
        ones = jnp.where((lane >= ones0) & (lane < ones0 + 3), 1.0, 0.0)
        return (jnp.dot(sm_block.astype(BF16), pick, preferred_element_type=F32) + ones).astype(BF16)

    @pl.when(qi == 0)
    def _():
        def build(j, carry):
            rows = pl.ds(pl.multiple_of(j * tk, tk), tk)
            kk = k_ref[0, rows, :].astype(F32)
            kn = kk * lax.rsqrt(jnp.mean(kk * kk, axis=-1, keepdims=True) + EPS) * gk_ref[...]
            kaug_ref[rows, 0:FOX_DH] = kn.astype(BF16)
            kaug_ref[rows, FOX_DH:2 * FOX_DH] = bias_cols(smk_ref[0, rows, :], 3, -1.0, 0)
            vaug_ref[rows, 0:FOX_DH] = v_ref[0, rows, :]
            vaug_ref[rows, FOX_DH:2 * FOX_DH] = jnp.ones((tk, FOX_DH), BF16)
            return carry
        lax.fori_loop(0, nk, build, 0)

    qq = q_ref[0].astype(F32)
    qn = qq * (lax.rsqrt(jnp.mean(qq * qq, axis=-1, keepdims=True) + EPS) * (FOX_DH ** -0.5 * LOG2E)) * gq_ref[...]
    q_aug = jnp.concatenate([qn.astype(BF16), bias_cols(smq_ref[0], 0, 1.0, 3)], axis=1)

    m_s[...] = jnp.full_like(m_s, NEG_BIG)
    acc_s[...] = jnp.zeros_like(acc_s)

    def scores(s_ref, j):
        rows = pl.ds(pl.multiple_of(j * tk, tk), tk)
        s_ref[...] = lax.dot_general(q_aug, kaug_ref[rows, :], (((1,), (1,)), ((), ())),
                                     preferred_element_type=F32)

    def accumulate(s_ref, j, diag):
        rows = pl.ds(pl.multiple_of(j * tk, tk), tk)
        v_blk = vaug_ref[rows, :]
        groups = ([(0, tq, False)] if diag is None
                  else [(r * tk, (r + 1) * tk, r == diag) for r in range(diag, n_sub)])
        for lo_row, hi_row, masked in groups:
            rs = slice(lo_row, hi_row)
            tiles = [s_ref[rs, c * LANES:(c + 1) * LANES] for c in range(n_lt)]
            if masked:
                ri = lax.broadcasted_iota(jnp.int32, (tk, LANES), 0)
                ci = lax.broadcasted_iota(jnp.int32, (tk, LANES), 1)
                tiles = [jnp.where(ri >= ci + c * LANES, t, NEG_BIG) for c, t in enumerate(tiles)]
            mx = tiles[0]
            for t in tiles[1:]:
                mx = jnp.maximum(mx, t)
            m_prev = m_s[rs, :]
            m_next = jnp.maximum(m_prev, jnp.max(mx, axis=1, keepdims=True))
            p = jnp.concatenate([jnp.exp2(t - m_next) for t in tiles], axis=1).astype(BF16)
            alpha = jnp.exp2(m_prev - m_next)
            acc_s[rs, :] = (jnp.concatenate([alpha, alpha], axis=1) * acc_s[rs, :]
                            + jnp.dot(p, v_blk, preferred_element_type=F32))
            m_s[rs, :] = m_next

    scores(sa_ref, 0)

    def body(i, carry):
        j = 2 * i
        scores(sb_ref, j + 1)
        accumulate(sa_ref, j, None)
        scores(sa_ref, j + 2)
        accumulate(sb_ref, j + 1, None)
        return carry

    n_below = n_sub * qi
    lax.fori_loop(0, n_below // 2, body, 0)
    bufs = (sa_ref, sb_ref)
    for e in range(n_sub):
        if e + 1 < n_sub:
            scores(bufs[(e + 1) % 2], n_below + e + 1)
        accumulate(bufs[e % 2], n_below + e, e)

    acc = acc_s[...]
    o_ref[0] = (acc[:, :FOX_DH] / acc[:, FOX_DH:]).astype(o_ref.dtype)


def _fox(qkv, sm, gq, gk):
    b, s, _ = qkv.shape
    nh = FOX_HEADS
    tq = _tile(s, FOX_TQ)
    tk = _tile(tq, FOX_TK)
    assert (tq // tk) % 2 == 0, "the score pipeline consumes key blocks in pairs"
    nk = s // tk
    return pl.pallas_call(
        functools.partial(_fox_kernel, tq=tq, tk=tk, nk=nk),
        out_shape=jax.ShapeDtypeStruct((b, s, nh * FOX_DH), BF16),
        grid=(b, nh, s // tq),
        in_specs=[
            pl.BlockSpec((1, tq, FOX_DH), lambda i, h, j: (i, j, h)),
            pl.BlockSpec((1, s, FOX_DH), lambda i, h, j: (i, 0, nh + h)),
            pl.BlockSpec((1, s, FOX_DH), lambda i, h, j: (i, 0, 2 * nh + h)),
            pl.BlockSpec((1, tq, LANES), lambda i, h, j: (i, j, 0)),
            pl.BlockSpec((1, s, LANES), lambda i, h, j: (i, 0, 0)),
            pl.BlockSpec((1, FOX_DH), lambda i, h, j: (0, 0)),
            pl.BlockSpec((1, FOX_DH), lambda i, h, j: (0, 0)),
        ],
        out_specs=pl.BlockSpec((1, tq, FOX_DH), lambda i, h, j: (i, j, h)),
        scratch_shapes=[
            pltpu.VMEM((s, 2 * FOX_DH), BF16),
            pltpu.VMEM((s, 2 * FOX_DH), BF16),
            pltpu.VMEM((tq, LANES), F32),
            pltpu.VMEM((tq, 2 * FOX_DH), F32),
            pltpu.VMEM((tq, tk), F32),
            pltpu.VMEM((tq, tk), F32),
        ],
        compiler_params=_cparams(("arbitrary", "arbitrary", "arbitrary")),
        name="fox",
    )(qkv, qkv, qkv, sm, sm, gq, gk)


def _memkv_kernel(mem_ref, g_ref, w_ref, gk_ref, o_ref, *, n_k_tiles):
    j = pl.program_id(1)
    m = mem_ref[0]
    hn = (m * lax.rsqrt(jnp.mean(m * m, axis=-1, keepdims=True) + EPS) * g_ref[...]).astype(BF16)
    r = jnp.dot(hn, w_ref[...].astype(BF16), preferred_element_type=F32)
    parts = []
    for t in range(r.shape[1] // MEM_DH):
        rt = r[:, t * MEM_DH:(t + 1) * MEM_DH]
        parts.append(rt * lax.rsqrt(jnp.mean(rt * rt, axis=-1, keepdims=True) + EPS) * gk_ref[...])
    normed = jnp.concatenate(parts, axis=1)
    is_k = jnp.where(j < n_k_tiles, 1.0, 0.0)
    o_ref[0] = (is_k * normed + (1.0 - is_k) * r).astype(o_ref.dtype)


def _memkv(mem, g, w, gk):
    b, ml, d = mem.shape
    n = w.shape[1]
    tn = 2 * MEM_DH
    return pl.pallas_call(
        functools.partial(_memkv_kernel, n_k_tiles=(n // 2) // tn),
        out_shape=jax.ShapeDtypeStruct((b, ml, n), BF16),
        grid=(b, n // tn),
        in_specs=[
            pl.BlockSpec((1, ml, d), lambda i, j: (i, 0, 0)),
            pl.BlockSpec((1, d), lambda i, j: (0, 0)),
            pl.BlockSpec((d, tn), lambda i, j: (0, j)),
            pl.BlockSpec((1, MEM_DH), lambda i, j: (0, 0)),
        ],
        out_specs=pl.BlockSpec((1, ml, tn), lambda i, j: (i, 0, j)),
        compiler_params=_cparams(("arbitrary", "arbitrary")),
        name="memkv",
    )(mem, g, w, gk)


def _mem_kernel(q_ref, k_ref, v_ref, gq_ref, o_ref):
    for hd in range(MEM_HEADS):
        cs = slice(hd * MEM_DH, (hd + 1) * MEM_DH)
        qq = q_ref[0, :, cs].astype(F32)
        qn = qq * lax.rsqrt(jnp.mean(qq * qq, axis=-1, keepdims=True) + EPS) * gq_ref[...] * (MEM_DH ** -0.5)
        s = _dot_nt(qn, k_ref[0, :, cs])
        p = jnp.exp(s - jnp.max(s, axis=1, keepdims=True))
        p = p / jnp.sum(p, axis=1, keepdims=True)
        o_ref[0, :, cs] = jnp.dot(p.astype(BF16), v_ref[0, :, cs], preferred_element_type=F32).astype(o_ref.dtype)


def _mem_attn(qsrc, q_col_block, kv, gq):
    b, s, _ = qsrc.shape
    ml = kv.shape[1]
    wq = MEM_HEADS * MEM_DH
    tq = _tile(s, ROW_TILE)
    return pl.pallas_call(
        _mem_kernel,
        out_shape=jax.ShapeDtypeStruct((b, s, wq), BF16),
        grid=(b, s // tq),
        in_specs=[
            pl.BlockSpec((1, tq, wq), lambda i, j: (i, j, q_col_block)),
            pl.BlockSpec((1, ml, wq), lambda i, j: (i, 0, 0)),
            pl.BlockSpec((1, ml, wq), lambda i, j: (i, 0, 1)),
            pl.BlockSpec((1, MEM_DH), lambda i, j: (0, 0)),
        ],
        out_specs=pl.BlockSpec((1, tq, wq), lambda i, j: (i, j, 0)),
        compiler_params=_cparams(("arbitrary", "arbitrary")),
        name="mem_attn",
    )(qsrc, kv, kv, gq)


def _merge_kernel(oa_ref, ob_ref, om_ref, wa_ref, wb_ref, wm_ref, ga_ref, gb_ref, gm_ref, y_ref):
    d = lambda o_ref, w_ref: jnp.dot(o_ref[...], w_ref[...].astype(BF16), preferred_element_type=F32)
    y = (ga_ref[...].astype(F32) * d(oa_ref, wa_ref)
         + gb_ref[...].astype(F32) * d(ob_ref, wb_ref)
         + gm_ref[...].astype(F32) * d(om_ref, wm_ref))
    y_ref[...] = y.astype(y_ref.dtype)


def _merge(oa, ob, om, wa, wb, wm, gates):
    t, ka = oa.shape
    dm = wa.shape[1]
    tm = _tile(t, MERGE_TILE[0])
    tn = _tile(dm, MERGE_TILE[1])
    nb = dm // tn
    a_spec = lambda kk: pl.BlockSpec((tm, kk), lambda i, j: (i, 0))
    w_spec = lambda kk: pl.BlockSpec((kk, tn), lambda i, j: (0, j))
    g_spec = lambda off: pl.BlockSpec((tm, tn), lambda i, j: (i, off * nb + j))
    return pl.pallas_call(
        _merge_kernel,
        out_shape=jax.ShapeDtypeStruct((t, dm), BF16),
        grid=(t // tm, nb),
        in_specs=[a_spec(ka), a_spec(ob.shape[1]), a_spec(om.shape[1]),
                  w_spec(ka), w_spec(ob.shape[1]), w_spec(om.shape[1]),
                  g_spec(0), g_spec(1), g_spec(2)],
        out_specs=pl.BlockSpec((tm, tn), lambda i, j: (i, j)),
        compiler_params=_cparams(("arbitrary", "arbitrary")),
        name="merge",
    )(oa, ob, om, wa, wb, wm, gates, gates, gates)


def _layer(x, mem, g_mix, w_in, conv_w, a_log, dt_bias, gdn_norm_g, fox_b_f, fox_q_norm, fox_k_norm,
           g_mem, w_mem_kv, mem_q_norm, mem_k_norm, w_up_gdn, w_up_fox, w_up_mem, w_out, g_mlp, w_ff1, w_ff2):
    b, s, d = x.shape
    t = b * s
    gdn_qk = GDN_HEADS * GDN_DK
    gdn_v = GDN_HEADS * GDN_DV
    fox_w = FOX_HEADS * FOX_DH
    mem_w = MEM_HEADS * MEM_DH
    o_z = 2 * gdn_qk + gdn_v
    o_beta = o_z + gdn_v
    o_dec = o_beta + GDN_HEADS
    o_fq = o_dec + GDN_HEADS
    o_ff = o_fq + 3 * fox_w
    o_mq = o_ff + FOX_HEADS
    o_gate = o_mq + mem_w

    wt = w_in.T
    n_small = 2 * GDN_HEADS + FOX_HEADS
    w_small = jnp.concatenate([wt[o_beta:o_fq], wt[o_ff:o_mq],
                               jnp.zeros((LANES - n_small, d), F32)], axis=0)
    zpad = jnp.zeros((LANES - n_small,), F32)
    bias = jnp.concatenate([jnp.zeros((GDN_HEADS,), F32), dt_bias.astype(F32), fox_b_f.astype(F32), zpad])[None, :]
    alog = jnp.concatenate([jnp.zeros((GDN_HEADS,), F32), a_log.astype(F32), jnp.zeros((FOX_HEADS,), F32), zpad])[None, :]

    h, sm = _norm_small(x, g_mix[None, :], w_small, bias, alog)
    h2d = h.reshape(t, d)

    proj = functools.partial(_matmul, h2d, wt, tile=PROJ_TILE, w_rows_are_outputs=True)
    qkvz = proj(out_dtype=F32, w_row_ranges=[(0, o_beta)], name="proj_gdn").reshape(b, s, -1)
    att = proj(out_dtype=BF16, tile=PROJ_BF16_TILE, w_row_ranges=[(o_fq, 3 * fox_w), (o_mq, mem_w)],
               name="proj_att").reshape(b, s, -1)
    gates = proj(out_dtype=BF16, tile=PROJ_BF16_TILE, act="sigmoid", w_row_ranges=[(o_gate, N_BRANCH * d)],
                 name="proj_gate")

    o_a = _gdn(qkvz, sm, conv_w, gdn_norm_g[None, :])
    o_b = _fox(att, sm, fox_q_norm[None, :], fox_k_norm[None, :])
    kv_m = _memkv(mem, g_mem[None, :], w_mem_kv, mem_k_norm[None, :])
    o_m = _mem_attn(att, (3 * fox_w) // mem_w, kv_m, mem_q_norm[None, :])

    y = _merge(o_a.reshape(t, -1), o_b.reshape(t, -1), o_m.reshape(t, -1), w_up_gdn, w_up_fox, w_up_mem, gates)
    x1, h2 = _out_norm(y, w_out.astype(BF16), x.reshape(t, d), g_mlp[None, :])
    u = _matmul(h2, w_ff1, out_dtype=BF16, tile=FF1_TILE, act="relu2", name="ff1")
    out = _matmul(u, w_ff2.astype(BF16), out_dtype=F32, tile=FF2_TILE, residual=x1, name="ff2")
    return out.reshape(b, s, d)


def kernel(x, mem, g_mix, w_in, conv_w, a_log, dt_bias, gdn_norm_g, fox_b_f, fox_q_norm, fox_k_norm, g_mem, w_mem_kv, mem_q_norm, mem_k_norm, w_up_gdn, w_up_fox, w_up_mem, w_out, g_mlp, w_ff1, w_ff2):
    depth = w_in.shape[0]
    for l in range(depth):
        x = _layer(x, mem, g_mix[l], w_in[l], conv_w[l], a_log[l], dt_bias[l], gdn_norm_g[l], fox_b_f[l],
                   fox_q_norm[l], fox_k_norm[l], g_mem[l], w_mem_kv[l], mem_q_norm[l], mem_k_norm[l],
                   w_up_gdn[l], w_up_fox[l], w_up_mem[l], w_out[l], g_mlp[l], w_ff1[l], w_ff2[l])
    return x
```
